```python
import math
import jax, jax.numpy as jnp
from jax import lax
import numpy as np

D_MODEL = 1024
BATCH = 4
SEQ = 4096
DEPTH = 2
DEC_BATCH = 128
DEC_SEQ = 1
PAST_LEN = 8192
PAGE_SIZE = 128

HEAD_DIM = 64
MIX_WIDTH = D_MODEL // 4
N_BRANCH = 4
Q_BLOCK = 128
EPS = 1e-6
NEG = -1e30
FORCE = 1e30

MLA_HEADS = MIX_WIDTH // HEAD_DIM
MLA_Q_LORA = 3 * D_MODEL // 16
MLA_KV_LORA = D_MODEL // 8
MLA_NOPE = HEAD_DIM
MLA_ROPE = HEAD_DIM // 2
MLA_V = HEAD_DIM
ROPE_THETA = 10000.0

S5_GROUP = 16
S5_GROUPS = MIX_WIDTH // S5_GROUP
S5_STATE = 64
S5_DT_MIN = 1e-3
S5_DT_MAX = 1e-1

FOX_HEADS = MIX_WIDTH // HEAD_DIM
FOX_BIAS_INIT = 4.0

NSA_HEADS = MIX_WIDTH // HEAD_DIM
NSA_CMP_BLOCK = 32
NSA_SEL_BLOCK = 64
NSA_TOPK = 16
NSA_WINDOW = 512

REL_BUCKETS = 32
REL_MAX_DIST = 128

FFN_HIDDEN = (-(-8 * D_MODEL // 3) + 255) // 256 * 256

IN_SPLITS = (MLA_Q_LORA, MLA_KV_LORA, MLA_ROPE,
             MIX_WIDTH,
             MIX_WIDTH, MIX_WIDTH, MIX_WIDTH, FOX_HEADS,
             MIX_WIDTH, HEAD_DIM, HEAD_DIM, HEAD_DIM, HEAD_DIM, HEAD_DIM, HEAD_DIM, 3 * NSA_HEADS,
             N_BRANCH * D_MODEL)
IN_COLS = sum(IN_SPLITS)
SPLIT_POINTS = tuple(int(v) for v in np.cumsum(IN_SPLITS)[:-1])

kernel_name = 'hybrid_mla_s5_fox_nsa_decoder_step'


def _rms_norm(x, g):
    xf = x.astype(jnp.float32)
    y = xf * lax.rsqrt(jnp.mean(xf * xf, axis=-1, keepdims=True) + EPS)
    return (y * g.astype(jnp.float32)).astype(x.dtype)


def _rope(x, pos):
    half = x.shape[-1] // 2
    freq = ROPE_THETA ** (-jnp.arange(half, dtype=jnp.float32) / half)
    ang = pos.astype(jnp.float32)[:, None] * freq[None, :]
    ang = ang.reshape(ang.shape[0], *([1] * (x.ndim - 3)), half)
    cos, sin = jnp.cos(ang), jnp.sin(ang)
    xf = x.astype(jnp.float32)
    x1, x2 = xf[..., :half], xf[..., half:]
    return jnp.concatenate([x1 * cos - x2 * sin, x1 * sin + x2 * cos], axis=-1).astype(x.dtype)


def _rel_bucket(dist):
    n = jnp.maximum(dist, 0)
    exact = REL_BUCKETS // 2
    large = exact + (jnp.log(jnp.maximum(n, 1).astype(jnp.float32) / exact)
                     / math.log(REL_MAX_DIST / exact) * (REL_BUCKETS - exact)).astype(jnp.int32)
    large = jnp.minimum(large, REL_BUCKETS - 1)
    return jnp.where(n < exact, n, large)


def _t5_bias(rel_bias, dist):
    b = rel_bias[_rel_bucket(dist)].astype(jnp.float32)
    return jnp.moveaxis(b, -1, -3)


def _masked_softmax(logits, mask):
    lg = jnp.where(mask, logits, NEG)
    m = jnp.max(lg, axis=-1, keepdims=True)
    e = jnp.where(mask, jnp.exp(lg - m), 0.0)
    return e / jnp.maximum(jnp.sum(e, axis=-1, keepdims=True), 1e-30)


def _attend(logits, values, spec):
    lg = logits[0] if len(logits) == 1 else jnp.concatenate(logits, axis=-1)
    p = jax.nn.softmax(lg, axis=-1)
    out, start = None, 0
    for l_, v_ in zip(logits, values):
        n = l_.shape[-1]
        o = jnp.einsum(spec, p[..., start:start + n].astype(v_.dtype), v_)
        out = o if out is None else out + o
        start += n
    return out


def _map_query_blocks(fn, arrays, qpos):
    T = qpos.shape[0]
    qb = Q_BLOCK if T % Q_BLOCK == 0 else T
    nb = T // qb
    def split(a):
        return jnp.moveaxis(a.reshape(a.shape[0], nb, qb, *a.shape[2:]), 1, 0)
    xs = tuple(split(a) for a in arrays) + (qpos.reshape(nb, qb),)
    out = lax.map(lambda args: fn(*args), xs)
    out = jnp.moveaxis(out, 0, 1)
    return out.reshape(out.shape[0], T, *out.shape[3:])


def _complex_affine_combine(e1, e2):
    a1r, a1i, b1r, b1i = e1
    a2r, a2i, b2r, b2i = e2
    return (a2r * a1r - a2i * a1i, a2r * a1i + a2i * a1r,
            a2r * b1r - a2i * b1i + b2r, a2r * b1i + a2i * b1r + b2i)


def _gather_past(l, cache_mla, cache_fox_k, cache_fox_v, cache_fox_logf, cache_nsa,
                 state_nsa_win, state_s5, page_table):
    nb, n_pages = page_table.shape
    def pages(cache):
        g = cache[l, page_table]
        return g.reshape(nb, n_pages * g.shape[2], *g.shape[3:])
    return {'mla': pages(cache_mla), 'fox_k': pages(cache_fox_k), 'fox_v': pages(cache_fox_v),
            'fox_logf': pages(cache_fox_logf), 'nsa': pages(cache_nsa),
            'win': state_nsa_win[l], 's5': state_s5[l]}


def _mla(q_lat, kv_lat, k_rope, pos, past, l, w):
    B, T, _ = q_lat.shape
    qn = _rms_norm(q_lat, w['mla_g_q'][l])
    q = jnp.einsum('btr,rhe->bthe', qn, w['mla_w_uq'][l])
    q_nope = q[..., :MLA_NOPE]
    q_rope = _rope(q[..., MLA_NOPE:], pos)
    c_new = _rms_norm(kv_lat, w['mla_g_kv'][l])
    kr_new = _rope(k_rope, pos)
    q_abs = jnp.einsum('bthn,lhn->bthl', q_nope, w['mla_w_uk'][l])
    segs = [(c_new, kr_new, pos)]
    if past is not None:
        n_past = past['mla'].shape[1]
        segs = [(past['mla'][..., :MLA_KV_LORA], past['mla'][..., MLA_KV_LORA:],
                 jnp.arange(n_past, dtype=jnp.int32))] + segs
    scale = (MLA_NOPE + MLA_ROPE) ** -0.5
    def block(qa, qr, qp):
        logits = []
        for c_, kr_, kp in segs:
            s = (jnp.einsum('bqhl,bkl->bhqk', qa, c_, preferred_element_type=jnp.float32)
                 + jnp.einsum('bqhr,bkr->bhqk', qr, kr_, preferred_element_type=jnp.float32)) * scale
            logits.append(jnp.where(kp[None, :] <= qp[:, None], s, NEG))
        return _attend(logits, [sg[0] for sg in segs], 'bhqk,bkl->bqhl')
    o_lat = _map_query_blocks(block, (q_abs, q_rope), pos)
    o = jnp.einsum('bthl,lhv->bthv', o_lat, w['mla_w_uv'][l]).reshape(B, T, MLA_HEADS * MLA_V)
    return o, jnp.concatenate([c_new, kr_new], axis=-1)


def _s5(u, past, l, w):
    B, T, _ = u.shape
    f32 = jnp.float32
    lam_re = w['s5_lambda_re'][l].astype(f32)
    lam_im = w['s5_lambda_im'][l].astype(f32)
    dt = jnp.exp(w['s5_log_dt'][l].astype(f32))[:, None]
    mag = jnp.exp(lam_re * dt)
    a_re, a_im = mag * jnp.cos(lam_im * dt), mag * jnp.sin(lam_im * dt)
    den = lam_re * lam_re + lam_im * lam_im
    coef_re = ((a_re - 1.0) * lam_re + a_im * lam_im) / den
    coef_im = (a_im * lam_re - (a_re - 1.0) * lam_im) / den
    b_re, b_im = w['s5_b_re'][l].astype(f32), w['s5_b_im'][l].astype(f32)
    bb_re = coef_re[..., None] * b_re - coef_im[..., None] * b_im
    bb_im = coef_re[..., None] * b_im + coef_im[..., None] * b_re
    uf = u.astype(f32)
    ug = uf.reshape(B, T, S5_GROUPS, S5_GROUP)
    bu_re = jnp.einsum('btgi,gpi->btgp', ug, bb_re)
    bu_im = jnp.einsum('btgi,gpi->btgp', ug, bb_im)
    if past is not None:
        x0_re, x0_im = past['s5'][:, 0].astype(f32), past['s5'][:, 1].astype(f32)
        bu_re = bu_re.at[:, 0].add(a_re * x0_re - a_im * x0_im)
        bu_im = bu_im.at[:, 0].add(a_re * x0_im + a_im * x0_re)
    A_re = jnp.broadcast_to(a_re, bu_re.shape)
    A_im = jnp.broadcast_to(a_im, bu_im.shape)
    _, _, x_re, x_im = lax.associative_scan(_complex_affine_combine, (A_re, A_im, bu_re, bu_im), axis=1)
    y = (jnp.einsum('btgp,gip->btgi', x_re, w['s5_c_re'][l].astype(f32))
         - jnp.einsum('btgp,gip->btgi', x_im, w['s5_c_im'][l].astype(f32)))
    y = y.reshape(B, T, MIX_WIDTH) + w['s5_d'][l].astype(f32) * uf
    y = jax.nn.gelu(y)
    y = y * jax.nn.sigmoid(jnp.einsum('btw,wv->btv', y, w['s5_w_glu'][l].astype(f32)))
    state = jnp.stack([x_re[:, -1], x_im[:, -1]], axis=1)
    return y.astype(u.dtype), state.astype(u.dtype)


def _fox(q, k, v, f_logit, pos, past, l, w):
    B, T, _ = q.shape
    q = q.reshape(B, T, FOX_HEADS, HEAD_DIM)
    k = k.reshape(B, T, FOX_HEADS, HEAD_DIM)
    v = v.reshape(B, T, FOX_HEADS, HEAD_DIM)
    logf = jax.nn.log_sigmoid(f_logit.astype(jnp.float32) + w['fox_b_f'][l].astype(jnp.float32))
    if past is None:
        F_new = jnp.cumsum(logf, axis=1)
        segs = [(k, v, F_new, pos)]
    else:
        F_past = jnp.cumsum(past['fox_logf'].astype(jnp.float32), axis=1)
        F_new = F_past[:, -1:] + jnp.cumsum(logf, axis=1)
        segs = [(past['fox_k'], past['fox_v'], F_past, jnp.arange(F_past.shape[1], dtype=jnp.int32)),
                (k, v, F_new, pos)]
    scale = HEAD_DIM ** -0.5
    def block(qq, fq, qp):
        fq_t = jnp.swapaxes(fq, 1, 2)[..., None]
        logits = []
        for k_, _, fk, kp in segs:
            s = (jnp.einsum('bqhd,bkhd->bhqk', qq, k_, preferred_element_type=jnp.float32) * scale
                 + (fq_t - jnp.swapaxes(fk, 1, 2)[:, :, None, :]))
            logits.append(jnp.where(kp[None, :] <= qp[:, None], s, NEG))
        return _attend(logits, [sg[1] for sg in segs], 'bhqk,bkhd->bqhd')
    o = _map_query_blocks(block, (q, F_new), pos)
    return o.reshape(B, T, MIX_WIDTH), k, v, logf.astype(q.dtype)


def _nsa(q, kc, vc, ks, vs, kw, vw, g_logit, pos, past, l, w):
    B, T, _ = q.shape
    q = q.reshape(B, T, NSA_HEADS, HEAD_DIM)
    g = jax.nn.sigmoid(g_logit).reshape(B, T, 3, NSA_HEADS)
    rows = jnp.stack([kc, vc, ks, vs], axis=2)
    win_rows = jnp.stack([kw, vw], axis=2)
    if past is None:
        kc_all, vc_all, ks_all, vs_all = kc, vc, ks, vs
        win_ctx = win_rows
        keep = min(NSA_WINDOW, T)
    else:
        pr = past['nsa']
        kc_all = jnp.concatenate([pr[:, :, 0], kc], axis=1)
        vc_all = jnp.concatenate([pr[:, :, 1], vc], axis=1)
        ks_all = jnp.concatenate([pr[:, :, 2], ks], axis=1)
        vs_all = jnp.concatenate([pr[:, :, 3], vs], axis=1)
        win_ctx = jnp.concatenate([past['win'], win_rows], axis=1)
        keep = past['win'].shape[1]
    n_keys = kc_all.shape[1]
    win_base = n_keys - win_ctx.shape[1]
    n_cb = -(-n_keys // NSA_CMP_BLOCK)
    n_sb = -(-n_keys // NSA_SEL_BLOCK)
    ratio = NSA_SEL_BLOCK // NSA_CMP_BLOCK
    k_sel = min(NSA_TOPK, n_sb)
    def pad_rows(a, n):
        return jnp.pad(a, ((0, 0), (0, n - a.shape[1]), (0, 0)))
    kcb = jnp.einsum('bnid,i->bnd', pad_rows(kc_all, n_cb * NSA_CMP_BLOCK).reshape(B, n_cb, NSA_CMP_BLOCK, HEAD_DIM),
                     w['nsa_w_cmp_k'][l])
    vcb = jnp.einsum('bnid,i->bnd', pad_rows(vc_all, n_cb * NSA_CMP_BLOCK).reshape(B, n_cb, NSA_CMP_BLOCK, HEAD_DIM),
                     w['nsa_w_cmp_v'][l])
    cb_end = jnp.arange(n_cb, dtype=jnp.int32) * NSA_CMP_BLOCK + (NSA_CMP_BLOCK - 1)
    sb_start = jnp.arange(n_sb, dtype=jnp.int32) * NSA_SEL_BLOCK
    ks_pad = pad_rows(ks_all, n_sb * NSA_SEL_BLOCK)
    vs_pad = pad_rows(vs_all, n_sb * NSA_SEL_BLOCK)
    win_pad = jnp.pad(win_ctx, ((0, 0), (NSA_WINDOW, 0), (0, 0), (0, 0)))
    rel_bias = w['rel_bias']
    scale = HEAD_DIM ** -0.5
    def block(qq, gb, qp):
        nq = qp.shape[0]
        lc = (jnp.einsum('bqhd,bnd->bhqn', qq, kcb, preferred_element_type=jnp.float32) * scale
              + _t5_bias(rel_bias, qp[:, None] - cb_end[None, :]))
        pc = _masked_softmax(lc, cb_end[None, :] <= qp[:, None])
        o_c = jnp.einsum('bhqn,bnd->bqhd', pc.astype(vcb.dtype), vcb)
        imp = jnp.pad(pc.sum(axis=1), ((0, 0), (0, 0), (0, n_sb * ratio - n_cb)))
        imp = imp.reshape(B, nq, n_sb, ratio).sum(-1)
        cur = (qp[:, None] // NSA_SEL_BLOCK) == jnp.arange(n_sb, dtype=jnp.int32)[None, :]
        score = jnp.where(cur, FORCE, jnp.where(sb_start[None, :] <= qp[:, None], imp, NEG))
        top_v, top_i = lax.top_k(score, k_sel)
        idx = (top_i[..., None] * NSA_SEL_BLOCK + jnp.arange(NSA_SEL_BLOCK, dtype=jnp.int32)).reshape(
            B, nq, k_sel * NSA_SEL_BLOCK)
        valid = jnp.repeat(top_v > NEG / 2, NSA_SEL_BLOCK, axis=-1) & (idx <= qp[None, :, None])
        ks_g = jax.vmap(lambda a, i: a[i])(ks_pad, idx)
        vs_g = jax.vmap(lambda a, i: a[i])(vs_pad, idx)
        ls = (jnp.einsum('bqhd,bqkd->bhqk', qq, ks_g, preferred_element_type=jnp.float32) * scale
              + _t5_bias(rel_bias, qp[None, :, None] - idx))
        ps = _masked_softmax(ls, valid[:, None])
        o_s = jnp.einsum('bhqk,bqkd->bqhd', ps.astype(vs_g.dtype), vs_g)
        start = qp[0] - win_base
        wk = lax.dynamic_slice_in_dim(win_pad, start, NSA_WINDOW + nq, axis=1)
        wp = qp[0] - NSA_WINDOW + jnp.arange(NSA_WINDOW + nq, dtype=jnp.int32)
        mask_w = ((wp[None, :] >= 0) & (wp[None, :] <= qp[:, None])
                  & (qp[:, None] - wp[None, :] <= NSA_WINDOW))
        lw = (jnp.einsum('bqhd,bkd->bhqk', qq, wk[:, :, 0], preferred_element_type=jnp.float32) * scale
              + _t5_bias(rel_bias, qp[:, None] - wp[None, :]))
        pw = jax.nn.softmax(jnp.where(mask_w, lw, NEG), axis=-1)
        o_w = jnp.einsum('bhqk,bkd->bqhd', pw.astype(wk.dtype), wk[:, :, 1])
        return (gb[:, :, 0, :, None] * o_c + gb[:, :, 1, :, None] * o_s
                + gb[:, :, 2, :, None] * o_w)
    o = _map_query_blocks(block, (q, g), pos)
    return o.reshape(B, T, MIX_WIDTH), rows, win_ctx[:, win_ctx.shape[1] - keep:]


def _mixer(h, pos, l, w, past):
    B, T, _ = h.shape
    z = jnp.einsum('btd,de->bte', h, w['w_in'][l])
    (q_lat, kv_lat, k_rope, s5_u, fq, fk, fv, ff,
     nq, nkc, nvc, nks, nvs, nkw, nvw, ng, gate_logit) = jnp.split(z, SPLIT_POINTS, axis=-1)
    y_a, mla_rows = _mla(q_lat, kv_lat, k_rope, pos, past, l, w)
    y_b, s5_state = _s5(s5_u, past, l, w)
    y_c, fox_k, fox_v, fox_logf = _fox(fq, fk, fv, ff, pos, past, l, w)
    y_d, nsa_rows, win_state = _nsa(nq, nkc, nvc, nks, nvs, nkw, nvw, ng, pos, past, l, w)
    branches = jnp.stack([y_a, y_b, y_c, y_d], axis=2)
    proj = jnp.einsum('btnw,nwd->btnd', branches, w['w_branch'][l])
    gates = jax.nn.sigmoid(gate_logit.reshape(B, T, N_BRANCH, D_MODEL))
    merged = jnp.einsum('btnd,btnd->btd', gates, proj)
    out = jnp.einsum('btd,de->bte', merged, w['w_out'][l])
    return out, (mla_rows, fox_k, fox_v, fox_logf, nsa_rows, win_state, s5_state)


def _forward(x, c, pos, w, caches):
    outs = [[] for _ in range(7)]
    for l in range(DEPTH):
        past = None if caches is None else _gather_past(l, *caches)
        mod = jnp.einsum('bd,de->be', jax.nn.silu(c), w['w_ada'][l]) + w['b_ada'][l]
        sh1, sc1, g1, sh2, sc2, g2 = jnp.split(mod[:, None, :], 6, axis=-1)
        gains = w['norm_gains'][l]
        h = _rms_norm(x, gains[0]) * (1.0 + sc1) + sh1
        m, states = _mixer(h, pos, l, w, past)
        x = x + g1 * _rms_norm(m, gains[1])
        h = _rms_norm(x, gains[2]) * (1.0 + sc2) + sh2
        f = jnp.einsum('btf,fd->btd',
                       jax.nn.silu(h @ w['w_ffn_gate'][l]) * (h @ w['w_ffn_up'][l]), w['w_ffn_down'][l])
        x = x + g2 * _rms_norm(f, gains[3])
        for o, s in zip(outs, states):
            o.append(s)
    return x, tuple(jnp.stack(o) for o in outs)


def setup_inputs(seed: int = 0) -> dict:
    key = jax.random.key(seed)
    ks = iter(jax.random.split(key, 64))
    f32 = jnp.float32
    def nrm(shape, scale=1.0):
        return jax.random.normal(next(ks), shape, f32) * scale
    d = D_MODEL
    n_pages = PAST_LEN // PAGE_SIZE
    n_pool = (DEC_BATCH * n_pages * 5) // 4
    w_buf = min(NSA_WINDOW, PAST_LEN)
    x_prompt = nrm((BATCH, SEQ, d))
    x_sample = nrm((DEC_BATCH, DEC_SEQ, d))
    c_prompt = nrm((BATCH, d))
    c_sample = nrm((DEC_BATCH, d))
    cache_mla = nrm((DEPTH, n_pool, PAGE_SIZE, MLA_KV_LORA + MLA_ROPE))
    cache_fox_k = nrm((DEPTH, n_pool, PAGE_SIZE, FOX_HEADS, HEAD_DIM))
    cache_fox_v = nrm((DEPTH, n_pool, PAGE_SIZE, FOX_HEADS, HEAD_DIM))
    cache_fox_logf = jax.nn.log_sigmoid(nrm((DEPTH, n_pool, PAGE_SIZE, FOX_HEADS)) + FOX_BIAS_INIT)
    cache_nsa = nrm((DEPTH, n_pool, PAGE_SIZE, 4, HEAD_DIM))
    state_nsa_win = nrm((DEPTH, DEC_BATCH, w_buf, 2, HEAD_DIM))
    state_s5 = nrm((DEPTH, DEC_BATCH, 2, S5_GROUPS, S5_STATE), 0.5)
    page_table = jax.random.permutation(next(ks), n_pool)[: DEC_BATCH * n_pages].reshape(
        DEC_BATCH, n_pages).astype(jnp.int32)
    lam_im = (jnp.pi * jnp.broadcast_to(jnp.arange(S5_STATE, dtype=f32), (DEPTH, S5_GROUPS, S5_STATE))
              + nrm((DEPTH, S5_GROUPS, S5_STATE), 0.01))
    return {
        'x_prompt': x_prompt, 'x_sample': x_sample, 'c_prompt': c_prompt, 'c_sample': c_sample,
        'cache_mla': cache_mla, 'cache_fox_k': cache_fox_k, 'cache_fox_v': cache_fox_v,
        'cache_fox_logf': cache_fox_logf, 'cache_nsa': cache_nsa, 'state_nsa_win': state_nsa_win,
        'state_s5': state_s5, 'page_table': page_table,
        'w_ada': nrm((DEPTH, d, 6 * d), 0.5 * d ** -0.5),
        'b_ada': nrm((DEPTH, 6 * d), 0.01),
        'norm_gains': 1.0 + nrm((DEPTH, 4, d), 0.05),
        'w_in': nrm((DEPTH, d, IN_COLS), d ** -0.5),
        'mla_g_q': 1.0 + nrm((DEPTH, MLA_Q_LORA), 0.05),
        'mla_g_kv': 1.0 + nrm((DEPTH, MLA_KV_LORA), 0.05),
        'mla_w_uq': nrm((DEPTH, MLA_Q_LORA, MLA_HEADS, MLA_NOPE + MLA_ROPE), MLA_Q_LORA ** -0.5),
        'mla_w_uk': nrm((DEPTH, MLA_KV_LORA, MLA_HEADS, MLA_NOPE), MLA_KV_LORA ** -0.5),
        'mla_w_uv': nrm((DEPTH, MLA_KV_LORA, MLA_HEADS, MLA_V), MLA_KV_LORA ** -0.5),
        's5_lambda_re': -0.5 * jnp.exp(nrm((DEPTH, S5_GROUPS, S5_STATE), 0.05)),
        's5_lambda_im': lam_im,
        's5_log_dt': jax.random.uniform(next(ks), (DEPTH, S5_GROUPS), f32,
                                        math.log(S5_DT_MIN), math.log(S5_DT_MAX)),
        's5_b_re': nrm((DEPTH, S5_GROUPS, S5_STATE, S5_GROUP), (2 * S5_GROUP) ** -0.5),
        's5_b_im': nrm((DEPTH, S5_GROUPS, S5_STATE, S5_GROUP), (2 * S5_GROUP) ** -0.5),
        's5_c_re': nrm((DEPTH, S5_GROUPS, S5_GROUP, S5_STATE), S5_STATE ** -0.5),
        's5_c_im': nrm((DEPTH, S5_GROUPS, S5_GROUP, S5_STATE), S5_STATE ** -0.5),
        's5_d': nrm((DEPTH, MIX_WIDTH)),
        's5_w_glu': nrm((DEPTH, MIX_WIDTH, MIX_WIDTH), MIX_WIDTH ** -0.5),
        'fox_b_f': FOX_BIAS_INIT + nrm((DEPTH, FOX_HEADS), 0.1),
        'nsa_w_cmp_k': (1.0 + nrm((DEPTH, NSA_CMP_BLOCK), 0.1)) / NSA_CMP_BLOCK,
        'nsa_w_cmp_v': (1.0 + nrm((DEPTH, NSA_CMP_BLOCK), 0.1)) / NSA_CMP_BLOCK,
        'rel_bias': nrm((REL_BUCKETS, NSA_HEADS), 0.5),
        'w_branch': nrm((DEPTH, N_BRANCH, MIX_WIDTH, d), MIX_WIDTH ** -0.5),
        'w_out': nrm((DEPTH, d, d), d ** -0.5),
        'w_ffn_gate': nrm((DEPTH, d, FFN_HIDDEN), d ** -0.5),
        'w_ffn_up': nrm((DEPTH, d, FFN_HIDDEN), d ** -0.5),
        'w_ffn_down': nrm((DEPTH, FFN_HIDDEN, d), FFN_HIDDEN ** -0.5),
    }


def reference(x_prompt, x_sample, c_prompt, c_sample, cache_mla, cache_fox_k, cache_fox_v,
              cache_fox_logf, cache_nsa, state_nsa_win, state_s5, page_table,
              w_ada, b_ada, norm_gains, w_in, mla_g_q, mla_g_kv, mla_w_uq, mla_w_uk, mla_w_uv,
              s5_lambda_re, s5_lambda_im, s5_log_dt, s5_b_re, s5_b_im, s5_c_re, s5_c_im, s5_d, s5_w_glu,
              fox_b_f, nsa_w_cmp_k, nsa_w_cmp_v, rel_bias, w_branch, w_out,
              w_ffn_gate, w_ffn_up, w_ffn_down):
    w = {'w_ada': w_ada, 'b_ada': b_ada, 'norm_gains': norm_gains, 'w_in': w_in,
         'mla_g_q': mla_g_q, 'mla_g_kv': mla_g_kv, 'mla_w_uq': mla_w_uq, 'mla_w_uk': mla_w_uk,
         'mla_w_uv': mla_w_uv, 's5_lambda_re': s5_lambda_re, 's5_lambda_im': s5_lambda_im,
         's5_log_dt': s5_log_dt, 's5_b_re': s5_b_re, 's5_b_im': s5_b_im, 's5_c_re': s5_c_re,
         's5_c_im': s5_c_im, 's5_d': s5_d, 's5_w_glu': s5_w_glu, 'fox_b_f': fox_b_f,
         'nsa_w_cmp_k': nsa_w_cmp_k, 'nsa_w_cmp_v': nsa_w_cmp_v, 'rel_bias': rel_bias,
         'w_branch': w_branch, 'w_out': w_out, 'w_ffn_gate': w_ffn_gate, 'w_ffn_up': w_ffn_up,
         'w_ffn_down': w_ffn_down}
    past_len = page_table.shape[1] * cache_mla.shape[2]
    pos_p = jnp.arange(x_prompt.shape[1], dtype=jnp.int32)
    pos_s = past_len + jnp.arange(x_sample.shape[1], dtype=jnp.int32)
    caches = (cache_mla, cache_fox_k, cache_fox_v, cache_fox_logf, cache_nsa,
              state_nsa_win, state_s5, page_table)
    y_prompt, st_p = _forward(x_prompt, c_prompt, pos_p, w, None)
    y_sample, st_s = _forward(x_sample, c_sample, pos_s, w, caches)
    mla_p, fox_k_p, fox_v_p, fox_logf_p, nsa_p, win_p, s5_p = st_p
    mla_s, fox_k_s, fox_v_s, fox_logf_s, nsa_s, win_s, s5_s = st_s
    return (y_prompt, y_sample, mla_p, mla_s, fox_k_p, fox_k_s, fox_v_p, fox_v_s,
            fox_logf_p, fox_logf_s, nsa_p, nsa_s, win_p, win_s, s5_p, s5_s)
```

```python
import functools
import math

import numpy as np
import jax
import jax.numpy as jnp
from jax import lax
from jax.experimental import pallas as pl
from jax.experimental.pallas import tpu as pltpu

F32 = jnp.float32
BF16 = jnp.bfloat16

D_MODEL = 1024
DEPTH = 2
HEAD_DIM = 64
MIX_WIDTH = D_MODEL // 4
N_BRANCH = 4
Q_BLOCK = 128
EPS = 1e-6
NEG = -1e30
FORCE = 1e30

MLA_HEADS = MIX_WIDTH // HEAD_DIM
MLA_Q_LORA = 3 * D_MODEL // 16
MLA_KV_LORA = D_MODEL // 8
MLA_NOPE = HEAD_DIM
MLA_ROPE = HEAD_DIM // 2
MLA_V = HEAD_DIM
ROPE_THETA = 10000.0

S5_GROUP = 16
S5_GROUPS = MIX_WIDTH // S5_GROUP
S5_STATE = 64

FOX_HEADS = MIX_WIDTH // HEAD_DIM

NSA_HEADS = MIX_WIDTH // HEAD_DIM
NSA_CMP_BLOCK = 32
NSA_SEL_BLOCK = 64
NSA_TOPK = 16
NSA_WINDOW = 512

REL_BUCKETS = 32
REL_MAX_DIST = 128

FFN_HIDDEN = (-(-8 * D_MODEL // 3) + 255) // 256 * 256

IN_SPLITS = (MLA_Q_LORA, MLA_KV_LORA, MLA_ROPE,
             MIX_WIDTH,
             MIX_WIDTH, MIX_WIDTH, MIX_WIDTH, FOX_HEADS,
             MIX_WIDTH, HEAD_DIM, HEAD_DIM, HEAD_DIM, HEAD_DIM, HEAD_DIM, HEAD_DIM, 3 * NSA_HEADS,
             N_BRANCH * D_MODEL)
IN_COLS = sum(IN_SPLITS)
GATE_COLS = N_BRANCH * D_MODEL
MIX_COLS = IN_COLS - GATE_COLS
MIX_SPLIT_POINTS = tuple(int(v) for v in np.cumsum(IN_SPLITS[:-1])[:-1])

LANE = 128
Z_COLS = -(-IN_COLS // LANE) * LANE
VMEM_LIMIT = 48 * 1024 * 1024


def _rms(x, g):
    return x * lax.rsqrt(jnp.mean(x * x, axis=-1, keepdims=True) + EPS) * g


def _ada_kernel(c_ref, w_ref, b_ref, o_ref):
    c = c_ref[...]
    a = (c * jax.nn.sigmoid(c)).astype(BF16)
    o_ref[0] = jnp.dot(a, w_ref[0], preferred_element_type=F32) + b_ref[0]


def _ada(c_all, w_ada, b_ada):
    rows = c_all.shape[0]
    tn = 1536
    return pl.pallas_call(
        _ada_kernel,
        grid=(DEPTH, 6 * D_MODEL // tn),
        in_specs=[pl.BlockSpec((rows, D_MODEL), lambda l, j: (0, 0)),
                  pl.BlockSpec((1, D_MODEL, tn), lambda l, j: (l, 0, j)),
                  pl.BlockSpec((1, 1, tn), lambda l, j: (l, 0, j))],
        out_specs=pl.BlockSpec((1, rows, tn), lambda l, j: (l, 0, j)),
        out_shape=jax.ShapeDtypeStruct((DEPTH, rows, 6 * D_MODEL), F32),
        compiler_params=pltpu.CompilerParams(dimension_semantics=("arbitrary", "arbitrary"),
                                             vmem_limit_bytes=VMEM_LIMIT),
        name="ada_mod",
    )(c_all, w_ada.astype(BF16), b_ada.reshape(DEPTH, 1, 6 * D_MODEL))


def _inproj_kernel(x_ref, gain_ref, sc_ref, sh_ref, w_ref, o_ref, h_ref):
    @pl.when(pl.program_id(1) == 0)
    def _():
        h = _rms(x_ref[...], gain_ref[...]) * (1.0 + sc_ref[0]) + sh_ref[0]
        h_ref[...] = h.astype(BF16)
    o_ref[...] = jnp.dot(h_ref[...], w_ref[...], preferred_element_type=F32)


def _inproj(x, gain, sc, sh, w, tm):
    n = x.shape[0]
    tiles_per_group = n // tm // sc.shape[0]
    tn = 1536
    mod_spec = pl.BlockSpec((1, sc.shape[1], D_MODEL), lambda i, j: (i // tiles_per_group, 0, 0))
    return pl.pallas_call(
        _inproj_kernel,
        grid=(n // tm, Z_COLS // tn),
        in_specs=[pl.BlockSpec((tm, D_MODEL), lambda i, j: (i, 0)),
                  pl.BlockSpec((1, D_MODEL), lambda i, j: (0, 0)),
                  mod_spec, mod_spec,
                  pl.BlockSpec((D_MODEL, tn), lambda i, j: (0, j))],
        out_specs=pl.BlockSpec((tm, tn), lambda i, j: (i, j)),
        out_shape=jax.ShapeDtypeStruct((n, Z_COLS), F32),
        scratch_shapes=[pltpu.VMEM((tm, D_MODEL), BF16)],
        compiler_params=pltpu.CompilerParams(dimension_semantics=("parallel", "arbitrary"),
                                             vmem_limit_bytes=VMEM_LIMIT),
        name="in_proj",
    )(x, gain.reshape(1, D_MODEL), sc, sh, w)


def _merge_kernel(ya_ref, yb_ref, yc_ref, yd_ref, gl_ref, x_ref, wb_ref, wo_ref, gain_ref, g1_ref, o_ref):
    acc = None
    for n, y_ref in enumerate((ya_ref, yb_ref, yc_ref, yd_ref)):
        p = jnp.dot(y_ref[...].astype(BF16), wb_ref[n], preferred_element_type=F32)
        t = jax.nn.sigmoid(gl_ref[:, n * D_MODEL:(n + 1) * D_MODEL]) * p
        acc = t if acc is None else acc + t
    m = jnp.dot(acc.astype(BF16), wo_ref[...], preferred_element_type=F32)
    o_ref[...] = x_ref[...] + g1_ref[0] * _rms(m, gain_ref[...])


def _merge(ys, z, x, wb, wo, gain, g1, tm):
    n = x.shape[0]
    tiles_per_group = n // tm // g1.shape[0]
    y_spec = pl.BlockSpec((tm, MIX_WIDTH), lambda i: (i, 0))
    return pl.pallas_call(
        _merge_kernel,
        grid=(n // tm,),
        in_specs=[y_spec, y_spec, y_spec, y_spec,
                  pl.BlockSpec((tm, GATE_COLS), lambda i: (i, 0)),
                  pl.BlockSpec((tm, D_MODEL), lambda i: (i, 0)),
                  pl.BlockSpec((N_BRANCH, MIX_WIDTH, D_MODEL), lambda i: (0, 0, 0)),
                  pl.BlockSpec((D_MODEL, D_MODEL), lambda i: (0, 0)),
                  pl.BlockSpec((1, D_MODEL), lambda i: (0, 0)),
                  pl.BlockSpec((1, g1.shape[1], D_MODEL), lambda i: (i // tiles_per_group, 0, 0))],
        out_specs=pl.BlockSpec((tm, D_MODEL), lambda i: (i, 0)),
        out_shape=jax.ShapeDtypeStruct((n, D_MODEL), F32),
        compiler_params=pltpu.CompilerParams(dimension_semantics=("parallel",),
                                             vmem_limit_bytes=VMEM_LIMIT),
        name="merge",
    )(*ys, z, x, wb, wo, gain.reshape(1, D_MODEL), g1)


def _ffn_kernel(x_ref, gain_h_ref, sc_ref, sh_ref, wg_ref, wu_ref, wd_ref, gain_o_ref, g2_ref, o_ref,
                h_ref, acc_ref):
    j = pl.program_id(1)

    @pl.when(j == 0)
    def _():
        h = _rms(x_ref[...], gain_h_ref[...]) * (1.0 + sc_ref[0]) + sh_ref[0]
        h_ref[...] = h.astype(BF16)
        acc_ref[...] = jnp.zeros_like(acc_ref)

    h = h_ref[...]
    a = jnp.dot(h, wg_ref[...], preferred_element_type=F32)
    b = jnp.dot(h, wu_ref[...], preferred_element_type=F32)
    t = (a * jax.nn.sigmoid(a)) * b
    acc_ref[...] += jnp.dot(t.astype(BF16), wd_ref[...], preferred_element_type=F32)

    @pl.when(j == pl.num_programs(1) - 1)
    def _():
        o_ref[...] = x_ref[...] + g2_ref[0] * _rms(acc_ref[...], gain_o_ref[...])


def _ffn(x, gain_h, sc, sh, wg, wu, wd, gain_o, g2, tm):
    n = x.shape[0]
    tiles_per_group = n // tm // sc.shape[0]
    th = FFN_HIDDEN // 2
    mod_spec = pl.BlockSpec((1, sc.shape[1], D_MODEL), lambda i, j: (i // tiles_per_group, 0, 0))
    vec_spec = pl.BlockSpec((1, D_MODEL), lambda i, j: (0, 0))
    return pl.pallas_call(
        _ffn_kernel,
        grid=(n // tm, FFN_HIDDEN // th),
        in_specs=[pl.BlockSpec((tm, D_MODEL), lambda i, j: (i, 0)),
                  vec_spec, mod_spec, mod_spec,
                  pl.BlockSpec((D_MODEL, th), lambda i, j: (0, j)),
                  pl.BlockSpec((D_MODEL, th), lambda i, j: (0, j)),
                  pl.BlockSpec((th, D_MODEL), lambda i, j: (j, 0)),
                  vec_spec, mod_spec],
        out_specs=pl.BlockSpec((tm, D_MODEL), lambda i, j: (i, 0)),
        out_shape=jax.ShapeDtypeStruct((n, D_MODEL), F32),
        scratch_shapes=[pltpu.VMEM((tm, D_MODEL), BF16), pltpu.VMEM((tm, D_MODEL), F32)],
        compiler_params=pltpu.CompilerParams(dimension_semantics=("parallel", "arbitrary"),
                                             vmem_limit_bytes=VMEM_LIMIT),
        name="ffn",
    )(x, gain_h.reshape(1, D_MODEL), sc, sh, wg, wu, wd, gain_o.reshape(1, D_MODEL), g2)


def _rms_norm(x, g):
    xf = x.astype(F32)
    y = xf * lax.rsqrt(jnp.mean(xf * xf, axis=-1, keepdims=True) + EPS)
    return (y * g.astype(F32)).astype(x.dtype)


def _rope(x, pos):
    half = x.shape[-1] // 2
    freq = ROPE_THETA ** (-jnp.arange(half, dtype=F32) / half)
    ang = pos.astype(F32)[:, None] * freq[None, :]
    ang = ang.reshape(ang.shape[0], *([1] * (x.ndim - 3)), half)
    cos, sin = jnp.cos(ang), jnp.sin(ang)
    x1, x2 = x[..., :half], x[..., half:]
    return jnp.concatenate([x1 * cos - x2 * sin, x1 * sin + x2 * cos], axis=-1)


def _rel_bucket(dist):
    n = jnp.maximum(dist, 0)
    exact = REL_BUCKETS // 2
    large = exact + (jnp.log(jnp.maximum(n, 1).astype(F32) / exact)
                     / math.log(REL_MAX_DIST / exact) * (REL_BUCKETS - exact)).astype(jnp.int32)
    large = jnp.minimum(large, REL_BUCKETS - 1)
    return jnp.where(n < exact, n, large)


def _t5_bias(rel_bias, dist):
    b = rel_bias[_rel_bucket(dist)].astype(F32)
    return jnp.moveaxis(b, -1, -3)


def _masked_softmax(logits, mask):
    lg = jnp.where(mask, logits, NEG)
    m = jnp.max(lg, axis=-1, keepdims=True)
    e = jnp.where(mask, jnp.exp(lg - m), 0.0)
    return e / jnp.maximum(jnp.sum(e, axis=-1, keepdims=True), 1e-30)


def _attend(logits, values, spec):
    lg = logits[0] if len(logits) == 1 else jnp.concatenate(logits, axis=-1)
    p = jax.nn.softmax(lg, axis=-1)
    out, start = None, 0
    for l_, v_ in zip(logits, values):
        n = l_.shape[-1]
        o = jnp.einsum(spec, p[..., start:start + n], v_)
        out = o if out is None else out + o
        start += n
    return out


def _map_query_blocks(fn, arrays, qpos):
    T = qpos.shape[0]
    qb = Q_BLOCK if T % Q_BLOCK == 0 else T
    nb = T // qb

    def split(a):
        return jnp.moveaxis(a.reshape(a.shape[0], nb, qb, *a.shape[2:]), 1, 0)
    xs = tuple(split(a) for a in arrays) + (qpos.reshape(nb, qb),)
    out = lax.map(lambda args: fn(*args), xs)
    out = jnp.moveaxis(out, 0, 1)
    return out.reshape(out.shape[0], T, *out.shape[3:])


def _complex_affine_combine(e1, e2):
    a1r, a1i, b1r, b1i = e1
    a2r, a2i, b2r, b2i = e2
    return (a2r * a1r - a2i * a1i, a2r * a1i + a2i * a1r,
            a2r * b1r - a2i * b1i + b2r, a2r * b1i + a2i * b1r + b2i)


def _gather_past(l, cache_mla, cache_fox_k, cache_fox_v, cache_fox_logf, cache_nsa,
                 state_nsa_win, state_s5, page_table):
    nb, n_pages = page_table.shape

    def pages(cache):
        g = cache[l, page_table]
        return g.reshape(nb, n_pages * g.shape[2], *g.shape[3:])
    return {'mla': pages(cache_mla), 'fox_k': pages(cache_fox_k), 'fox_v': pages(cache_fox_v),
            'fox_logf': pages(cache_fox_logf), 'nsa': pages(cache_nsa),
            'win': state_nsa_win[l], 's5': state_s5[l]}


def _mla(q_lat, kv_lat, k_rope, pos, past, l, w):
    B, T, _ = q_lat.shape
    qn = _rms_norm(q_lat, w['mla_g_q'][l])
    q = jnp.einsum('btr,rhe->bthe', qn, w['mla_w_uq'][l])
    q_nope = q[..., :MLA_NOPE]
    q_rope = _rope(q[..., MLA_NOPE:], pos)
    c_new = _rms_norm(kv_lat, w['mla_g_kv'][l])
    kr_new = _rope(k_rope, pos)
    q_abs = jnp.einsum('bthn,lhn->bthl', q_nope, w['mla_w_uk'][l])
    segs = [(c_new, kr_new, pos)]
    if past is not None:
        n_past = past['mla'].shape[1]
        segs = [(past['mla'][..., :MLA_KV_LORA], past['mla'][..., MLA_KV_LORA:],
                 jnp.arange(n_past, dtype=jnp.int32))] + segs
    scale = (MLA_NOPE + MLA_ROPE) ** -0.5

    def block(qa, qr, qp):
        logits = []
        for c_, kr_, kp in segs:
            s = (jnp.einsum('bqhl,bkl->bhqk', qa, c_, preferred_element_type=F32)
                 + jnp.einsum('bqhr,bkr->bhqk', qr, kr_, preferred_element_type=F32)) * scale
            logits.append(jnp.where(kp[None, :] <= qp[:, None], s, NEG))
        return _attend(logits, [sg[0] for sg in segs], 'bhqk,bkl->bqhl')
    o_lat = _map_query_blocks(block, (q_abs, q_rope), pos)
    o = jnp.einsum('bthl,lhv->bthv', o_lat, w['mla_w_uv'][l]).reshape(B, T, MLA_HEADS * MLA_V)
    return o, jnp.concatenate([c_new, kr_new], axis=-1)


def _s5(u, past, l, w):
    B, T, _ = u.shape
    lam_re = w['s5_lambda_re'][l]
    lam_im = w['s5_lambda_im'][l]
    dt = jnp.exp(w['s5_log_dt'][l])[:, None]
    mag = jnp.exp(lam_re * dt)
    a_re, a_im = mag * jnp.cos(lam_im * dt), mag * jnp.sin(lam_im * dt)
    den = lam_re * lam_re + lam_im * lam_im
    coef_re = ((a_re - 1.0) * lam_re + a_im * lam_im) / den
    coef_im = (a_im * lam_re - (a_re - 1.0) * lam_im) / den
    b_re, b_im = w['s5_b_re'][l], w['s5_b_im'][l]
    bb_re = coef_re[..., None] * b_re - coef_im[..., None] * b_im
    bb_im = coef_re[..., None] * b_im + coef_im[..., None] * b_re
    ug = u.reshape(B, T, S5_GROUPS, S5_GROUP)
    bu_re = jnp.einsum('btgi,gpi->btgp', ug, bb_re)
    bu_im = jnp.einsum('btgi,gpi->btgp', ug, bb_im)
    if past is not None:
        x0_re, x0_im = past['s5'][:, 0], past['s5'][:, 1]
        bu_re = bu_re.at[:, 0].add(a_re * x0_re - a_im * x0_im)
        bu_im = bu_im.at[:, 0].add(a_re * x0_im + a_im * x0_re)
    A_re = jnp.broadcast_to(a_re, bu_re.shape)
    A_im = jnp.broadcast_to(a_im, bu_im.shape)
    _, _, x_re, x_im = lax.associative_scan(_complex_affine_combine, (A_re, A_im, bu_re, bu_im), axis=1)
    y = (jnp.einsum('btgp,gip->btgi', x_re, w['s5_c_re'][l])
         - jnp.einsum('btgp,gip->btgi', x_im, w['s5_c_im'][l]))
    y = y.reshape(B, T, MIX_WIDTH) + w['s5_d'][l] * u
    y = jax.nn.gelu(y)
    y = y * jax.nn.sigmoid(jnp.einsum('btw,wv->btv', y, w['s5_w_glu'][l]))
    state = jnp.stack([x_re[:, -1], x_im[:, -1]], axis=1)
    return y, state


def _fox(q, k, v, f_logit, pos, past, l, w):
    B, T, _ = q.shape
    q = q.reshape(B, T, FOX_HEADS, HEAD_DIM)
    k = k.reshape(B, T, FOX_HEADS, HEAD_DIM)
    v = v.reshape(B, T, FOX_HEADS, HEAD_DIM)
    logf = jax.nn.log_sigmoid(f_logit + w['fox_b_f'][l])
    if past is None:
        F_new = jnp.cumsum(logf, axis=1)
        segs = [(k, v, F_new, pos)]
    else:
        F_past = jnp.cumsum(past['fox_logf'], axis=1)
        F_new = F_past[:, -1:] + jnp.cumsum(logf, axis=1)
        segs = [(past['fox_k'], past['fox_v'], F_past, jnp.arange(F_past.shape[1], dtype=jnp.int32)),
                (k, v, F_new, pos)]
    scale = HEAD_DIM ** -0.5

    def block(qq, fq, qp):
        fq_t = jnp.swapaxes(fq, 1, 2)[..., None]
        logits = []
        for k_, _, fk, kp in segs:
            s = (jnp.einsum('bqhd,bkhd->bhqk', qq, k_, preferred_element_type=F32) * scale
                 + (fq_t - jnp.swapaxes(fk, 1, 2)[:, :, None, :]))
            logits.append(jnp.where(kp[None, :] <= qp[:, None], s, NEG))
        return _attend(logits, [sg[1] for sg in segs], 'bhqk,bkhd->bqhd')
    o = _map_query_blocks(block, (q, F_new), pos)
    return o.reshape(B, T, MIX_WIDTH), k, v, logf


def _nsa(q, kc, vc, ks, vs, kw, vw, g_logit, pos, past, l, w):
    B, T, _ = q.shape
    q = q.reshape(B, T, NSA_HEADS, HEAD_DIM)
    g = jax.nn.sigmoid(g_logit).reshape(B, T, 3, NSA_HEADS)
    rows = jnp.stack([kc, vc, ks, vs], axis=2)
    win_rows = jnp.stack([kw, vw], axis=2)
    if past is None:
        kc_all, vc_all, ks_all, vs_all = kc, vc, ks, vs
        win_ctx = win_rows
        keep = min(NSA_WINDOW, T)
    else:
        pr = past['nsa']
        kc_all = jnp.concatenate([pr[:, :, 0], kc], axis=1)
        vc_all = jnp.concatenate([pr[:, :, 1], vc], axis=1)
        ks_all = jnp.concatenate([pr[:, :, 2], ks], axis=1)
        vs_all = jnp.concatenate([pr[:, :, 3], vs], axis=1)
        win_ctx = jnp.concatenate([past['win'], win_rows], axis=1)
        keep = past['win'].shape[1]
    n_keys = kc_all.shape[1]
    win_base = n_keys - win_ctx.shape[1]
    n_cb = -(-n_keys // NSA_CMP_BLOCK)
    n_sb = -(-n_keys // NSA_SEL_BLOCK)
    ratio = NSA_SEL_BLOCK // NSA_CMP_BLOCK
    k_sel = min(NSA_TOPK, n_sb)

    def pad_rows(a, n):
        return jnp.pad(a, ((0, 0), (0, n - a.shape[1]), (0, 0)))
    kcb = jnp.einsum('bnid,i->bnd', pad_rows(kc_all, n_cb * NSA_CMP_BLOCK).reshape(B, n_cb, NSA_CMP_BLOCK, HEAD_DIM),
                     w['nsa_w_cmp_k'][l])
    vcb = jnp.einsum('bnid,i->bnd', pad_rows(vc_all, n_cb * NSA_CMP_BLOCK).reshape(B, n_cb, NSA_CMP_BLOCK, HEAD_DIM),
                     w['nsa_w_cmp_v'][l])
    cb_end = jnp.arange(n_cb, dtype=jnp.int32) * NSA_CMP_BLOCK + (NSA_CMP_BLOCK - 1)
    sb_start = jnp.arange(n_sb, dtype=jnp.int32) * NSA_SEL_BLOCK
    ks_pad = pad_rows(ks_all, n_sb * NSA_SEL_BLOCK)
    vs_pad = pad_rows(vs_all, n_sb * NSA_SEL_BLOCK)
    win_pad = jnp.pad(win_ctx, ((0, 0), (NSA_WINDOW, 0), (0, 0), (0, 0)))
    rel_bias = w['rel_bias']
    scale = HEAD_DIM ** -0.5

    def block(qq, gb, qp):
        nq = qp.shape[0]
        lc = (jnp.einsum('bqhd,bnd->bhqn', qq, kcb, preferred_element_type=F32) * scale
              + _t5_bias(rel_bias, qp[:, None] - cb_end[None, :]))
        pc = _masked_softmax(lc, cb_end[None, :] <= qp[:, None])
        o_c = jnp.einsum('bhqn,bnd->bqhd', pc, vcb)
        imp = jnp.pad(pc.sum(axis=1), ((0, 0), (0, 0), (0, n_sb * ratio - n_cb)))
        imp = imp.reshape(B, nq, n_sb, ratio).sum(-1)
        cur = (qp[:, None] // NSA_SEL_BLOCK) == jnp.arange(n_sb, dtype=jnp.int32)[None, :]
        score = jnp.where(cur, FORCE, jnp.where(sb_start[None, :] <= qp[:, None], imp, NEG))
        top_v, top_i = lax.top_k(score, k_sel)
        idx = (top_i[..., None] * NSA_SEL_BLOCK + jnp.arange(NSA_SEL_BLOCK, dtype=jnp.int32)).reshape(
            B, nq, k_sel * NSA_SEL_BLOCK)
        valid = jnp.repeat(top_v > NEG / 2, NSA_SEL_BLOCK, axis=-1) & (idx <= qp[None, :, None])
        ks_g = jax.vmap(lambda a, i: a[i])(ks_pad, idx)
        vs_g = jax.vmap(lambda a, i: a[i])(vs_pad, idx)
        ls = (jnp.einsum('bqhd,bqkd->bhqk', qq, ks_g, preferred_element_type=F32) * scale
              + _t5_bias(rel_bias, qp[None, :, None] - idx))
        ps = _masked_softmax(ls, valid[:, None])
        o_s = jnp.einsum('bhqk,bqkd->bqhd', ps, vs_g)
        start = qp[0] - win_base
        wk = lax.dynamic_slice_in_dim(win_pad, start, NSA_WINDOW + nq, axis=1)
        wp = qp[0] - NSA_WINDOW + jnp.arange(NSA_WINDOW + nq, dtype=jnp.int32)
        mask_w = ((wp[None, :] >= 0) & (wp[None, :] <= qp[:, None])
                  & (qp[:, None] - wp[None, :] <= NSA_WINDOW))
        lw = (jnp.einsum('bqhd,bkd->bhqk', qq, wk[:, :, 0], preferred_element_type=F32) * scale
              + _t5_bias(rel_bias, qp[:, None] - wp[None, :]))
        pw = jax.nn.softmax(jnp.where(mask_w, lw, NEG), axis=-1)
        o_w = jnp.einsum('bhqk,bkd->bqhd', pw, wk[:, :, 1])
        return (gb[:, :, 0, :, None] * o_c + gb[:, :, 1, :, None] * o_s
                + gb[:, :, 2, :, None] * o_w)
    o = _map_query_blocks(block, (q, g), pos)
    return o.reshape(B, T, MIX_WIDTH), rows, win_ctx[:, win_ctx.shape[1] - keep:]


def _forward(x, mods, pos, w, wc, caches, tm):
    B, T, _ = x.shape
    n = B * T
    xf = x.reshape(n, D_MODEL)
    per_row = T == 1
    outs = [[] for _ in range(7)]
    for l in range(DEPTH):
        past = None if caches is None else _gather_past(l, *caches)
        m6 = mods[l].reshape(B, 6, D_MODEL)
        if per_row:
            sh1, sc1, g1, sh2, sc2, g2 = (m6[:, i].reshape(1, n, D_MODEL) for i in range(6))
        else:
            sh1, sc1, g1, sh2, sc2, g2 = (m6[:, i].reshape(B, 1, D_MODEL) for i in range(6))
        gains = w['norm_gains'][l]
        z = _inproj(xf, gains[0], sc1, sh1, wc['w_in'][l], tm)
        zz = z[:, GATE_COLS:GATE_COLS + MIX_COLS].reshape(B, T, MIX_COLS)
        (q_lat, kv_lat, k_rope, s5_u, fq, fk, fv, ff,
         nq, nkc, nvc, nks, nvs, nkw, nvw, ng) = jnp.split(zz, MIX_SPLIT_POINTS, axis=-1)
        y_a, mla_rows = _mla(q_lat, kv_lat, k_rope, pos, past, l, w)
        y_b, s5_state = _s5(s5_u, past, l, w)
        y_c, fox_k, fox_v, fox_logf = _fox(fq, fk, fv, ff, pos, past, l, w)
        y_d, nsa_rows, win_state = _nsa(nq, nkc, nvc, nks, nvs, nkw, nvw, ng, pos, past, l, w)
        ys = tuple(y.reshape(n, MIX_WIDTH) for y in (y_a, y_b, y_c, y_d))
        xf = _merge(ys, z, xf, wc['w_branch'][l], wc['w_out'][l], gains[1], g1, tm)
        xf = _ffn(xf, gains[2], sc2, sh2, wc['w_ffn_gate'][l], wc['w_ffn_up'][l], wc['w_ffn_down'][l],
                  gains[3], g2, tm)
        for o, s in zip(outs, (mla_rows, fox_k, fox_v, fox_logf, nsa_rows, win_state, s5_state)):
            o.append(s)
    return xf.reshape(B, T, D_MODEL), tuple(jnp.stack(o) for o in outs)


def kernel(x_prompt, x_sample, c_prompt, c_sample, cache_mla, cache_fox_k, cache_fox_v, cache_fox_logf, cache_nsa, state_nsa_win, state_s5, page_table, w_ada, b_ada, norm_gains, w_in, mla_g_q, mla_g_kv, mla_w_uq, mla_w_uk, mla_w_uv, s5_lambda_re, s5_lambda_im, s5_log_dt, s5_b_re, s5_b_im, s5_c_re, s5_c_im, s5_d, s5_w_glu, fox_b_f, nsa_w_cmp_k, nsa_w_cmp_v, rel_bias, w_branch, w_out, w_ffn_gate, w_ffn_up, w_ffn_down):
    w = {'norm_gains': norm_gains,
         'mla_g_q': mla_g_q, 'mla_g_kv': mla_g_kv, 'mla_w_uq': mla_w_uq, 'mla_w_uk': mla_w_uk,
         'mla_w_uv': mla_w_uv, 's5_lambda_re': s5_lambda_re, 's5_lambda_im': s5_lambda_im,
         's5_log_dt': s5_log_dt, 's5_b_re': s5_b_re, 's5_b_im': s5_b_im, 's5_c_re': s5_c_re,
         's5_c_im': s5_c_im, 's5_d': s5_d, 's5_w_glu': s5_w_glu, 'fox_b_f': fox_b_f,
         'nsa_w_cmp_k': nsa_w_cmp_k, 'nsa_w_cmp_v': nsa_w_cmp_v, 'rel_bias': rel_bias}
    w_in_r = jnp.concatenate([w_in[:, :, MIX_COLS:], w_in[:, :, :MIX_COLS],
                              jnp.zeros((DEPTH, D_MODEL, Z_COLS - IN_COLS), w_in.dtype)], axis=-1)
    wc = {'w_in': w_in_r.astype(BF16), 'w_branch': w_branch.astype(BF16), 'w_out': w_out.astype(BF16),
          'w_ffn_gate': w_ffn_gate.astype(BF16), 'w_ffn_up': w_ffn_up.astype(BF16),
          'w_ffn_down': w_ffn_down.astype(BF16)}
    nb_p = c_prompt.shape[0]
    mods = _ada(jnp.concatenate([c_prompt, c_sample], axis=0), w_ada, b_ada)
    past_len = page_table.shape[1] * cache_mla.shape[2]
    pos_p = jnp.arange(x_prompt.shape[1], dtype=jnp.int32)
    pos_s = past_len + jnp.arange(x_sample.shape[1], dtype=jnp.int32)
    caches = (cache_mla, cache_fox_k, cache_fox_v, cache_fox_logf, cache_nsa,
              state_nsa_win, state_s5, page_table)
    y_prompt, st_p = _forward(x_prompt, mods[:, :nb_p], pos_p, w, wc, None, 512)
    y_sample, st_s = _forward(x_sample, mods[:, nb_p:], pos_s, w, wc, caches, x_sample.shape[0])
    mla_p, fox_k_p, fox_v_p, fox_logf_p, nsa_p, win_p, s5_p = st_p
    mla_s, fox_k_s, fox_v_s, fox_logf_s, nsa_s, win_s, s5_s = st_s
    return (y_prompt, y_sample, mla_p, mla_s, fox_k_p, fox_k_s, fox_v_p, fox_v_s,
            fox_logf_p, fox_logf_s, nsa_p, nsa_s, win_p, win_s, s5_p, s5_s)
```

```python
import functools
import math

import numpy as np
import jax
import jax.numpy as jnp
from jax import lax
from jax.experimental import pallas as pl
from jax.experimental.pallas import tpu as pltpu

F32 = jnp.float32
BF16 = jnp.bfloat16

D_MODEL = 1024
DEPTH = 2
HEAD_DIM = 64
MIX_WIDTH = D_MODEL // 4
N_BRANCH = 4
Q_BLOCK = 128
EPS = 1e-6
NEG = -1e30
FORCE = 1e30

MLA_HEADS = MIX_WIDTH // HEAD_DIM
MLA_Q_LORA = 3 * D_MODEL // 16
MLA_KV_LORA = D_MODEL // 8
MLA_NOPE = HEAD_DIM
MLA_ROPE = HEAD_DIM // 2
MLA_V = HEAD_DIM
ROPE_THETA = 10000.0

S5_GROUP = 16
S5_GROUPS = MIX_WIDTH // S5_GROUP
S5_STATE = 64

FOX_HEADS = MIX_WIDTH // HEAD_DIM

NSA_HEADS = MIX_WIDTH // HEAD_DIM
NSA_CMP_BLOCK = 32
NSA_SEL_BLOCK = 64
NSA_TOPK = 16
NSA_WINDOW = 512

REL_BUCKETS = 32
REL_MAX_DIST = 128

FFN_HIDDEN = (-(-8 * D_MODEL // 3) + 255) // 256 * 256

IN_SPLITS = (MLA_Q_LORA, MLA_KV_LORA, MLA_ROPE,
             MIX_WIDTH,
             MIX_WIDTH, MIX_WIDTH, MIX_WIDTH, FOX_HEADS,
             MIX_WIDTH, HEAD_DIM, HEAD_DIM, HEAD_DIM, HEAD_DIM, HEAD_DIM, HEAD_DIM, 3 * NSA_HEADS,
             N_BRANCH * D_MODEL)
IN_COLS = sum(IN_SPLITS)
GATE_COLS = N_BRANCH * D_MODEL
MIX_COLS = IN_COLS - GATE_COLS
MIX_SPLIT_POINTS = tuple(int(v) for v in np.cumsum(IN_SPLITS[:-1])[:-1])

LANE = 128
VMEM_LIMIT = 48 * 1024 * 1024

Z_GATE = 0
Z_FQ = Z_GATE + GATE_COLS
Z_FK = Z_FQ + MIX_WIDTH
Z_FV = Z_FK + MIX_WIDTH
Z_NQ = Z_FV + MIX_WIDTH
Z_S5U = Z_NQ + MIX_WIDTH
Z_QLAT = Z_S5U + MIX_WIDTH
Z_KVLAT = Z_QLAT + 2 * LANE
Z_KROPE = Z_KVLAT + LANE
Z_KROPE_SW = Z_KROPE + LANE
Z_KCVC = Z_KROPE_SW + LANE
Z_KSVS = Z_KCVC + LANE
Z_KWVW = Z_KSVS + LANE
Z_NG = Z_KWVW + LANE
Z_COLS = Z_NG + LANE
Z_TILE = Z_COLS // 3


def _rms(x, g):
    return x * lax.rsqrt(jnp.mean(x * x, axis=-1, keepdims=True) + EPS) * g


def _ada_kernel(c_ref, w_ref, b_ref, o_ref):
    c = c_ref[...]
    a = (c * jax.nn.sigmoid(c)).astype(BF16)
    o_ref[0] = jnp.dot(a, w_ref[0], preferred_element_type=F32) + b_ref[0]


def _ada(c_all, w_ada, b_ada):
    rows = c_all.shape[0]
    tn = 1536
    return pl.pallas_call(
        _ada_kernel,
        grid=(DEPTH, 6 * D_MODEL // tn),
        in_specs=[pl.BlockSpec((rows, D_MODEL), lambda l, j: (0, 0)),
                  pl.BlockSpec((1, D_MODEL, tn), lambda l, j: (l, 0, j)),
                  pl.BlockSpec((1, 1, tn), lambda l, j: (l, 0, j))],
        out_specs=pl.BlockSpec((1, rows, tn), lambda l, j: (l, 0, j)),
        out_shape=jax.ShapeDtypeStruct((DEPTH, rows, 6 * D_MODEL), F32),
        compiler_params=pltpu.CompilerParams(dimension_semantics=("arbitrary", "arbitrary"),
                                             vmem_limit_bytes=VMEM_LIMIT),
        name="ada_mod",
    )(c_all, w_ada.astype(BF16), b_ada.reshape(DEPTH, 1, 6 * D_MODEL))


def _inproj_kernel(x_ref, gain_ref, sc_ref, sh_ref, w_ref, o_ref, h_ref):
    @pl.when(pl.program_id(1) == 0)
    def _():
        h = _rms(x_ref[...], gain_ref[...]) * (1.0 + sc_ref[0]) + sh_ref[0]
        h_ref[...] = h.astype(BF16)
    o_ref[...] = jnp.dot(h_ref[...], w_ref[...], preferred_element_type=F32)


def _inproj(x, gain, sc, sh, w, tm):
    n = x.shape[0]
    tiles_per_group = n // tm // sc.shape[0]
    tn = Z_TILE
    mod_spec = pl.BlockSpec((1, sc.shape[1], D_MODEL), lambda i, j: (i // tiles_per_group, 0, 0))
    return pl.pallas_call(
        _inproj_kernel,
        grid=(n // tm, Z_COLS // tn),
        in_specs=[pl.BlockSpec((tm, D_MODEL), lambda i, j: (i, 0)),
                  pl.BlockSpec((1, D_MODEL), lambda i, j: (0, 0)),
                  mod_spec, mod_spec,
                  pl.BlockSpec((D_MODEL, tn), lambda i, j: (0, j))],
        out_specs=pl.BlockSpec((tm, tn), lambda i, j: (i, j)),
        out_shape=jax.ShapeDtypeStruct((n, Z_COLS), F32),
        scratch_shapes=[pltpu.VMEM((tm, D_MODEL), BF16)],
        compiler_params=pltpu.CompilerParams(dimension_semantics=("parallel", "arbitrary"),
                                             vmem_limit_bytes=VMEM_LIMIT),
        name="in_proj",
    )(x, gain.reshape(1, D_MODEL), sc, sh, w)


def _merge_kernel(ya_ref, yb_ref, yc_ref, yd_ref, gl_ref, x_ref, wb_ref, wo_ref, gain_ref, g1_ref, o_ref):
    acc = None
    for n, y_ref in enumerate((ya_ref, yb_ref, yc_ref, yd_ref)):
        p = jnp.dot(y_ref[...].astype(BF16), wb_ref[n], preferred_element_type=F32)
        t = jax.nn.sigmoid(gl_ref[:, n * D_MODEL:(n + 1) * D_MODEL]) * p
        acc = t if acc is None else acc + t
    m = jnp.dot(acc.astype(BF16), wo_ref[...], preferred_element_type=F32)
    o_ref[...] = x_ref[...] + g1_ref[0] * _rms(m, gain_ref[...])


def _merge(ys, z, x, wb, wo, gain, g1, tm):
    n = x.shape[0]
    tiles_per_group = n // tm // g1.shape[0]
    y_spec = pl.BlockSpec((tm, MIX_WIDTH), lambda i: (i, 0))
    return pl.pallas_call(
        _merge_kernel,
        grid=(n // tm,),
        in_specs=[y_spec, y_spec, y_spec, y_spec,
                  pl.BlockSpec((tm, GATE_COLS), lambda i: (i, 0)),
                  pl.BlockSpec((tm, D_MODEL), lambda i: (i, 0)),
                  pl.BlockSpec((N_BRANCH, MIX_WIDTH, D_MODEL), lambda i: (0, 0, 0)),
                  pl.BlockSpec((D_MODEL, D_MODEL), lambda i: (0, 0)),
                  pl.BlockSpec((1, D_MODEL), lambda i: (0, 0)),
                  pl.BlockSpec((1, g1.shape[1], D_MODEL), lambda i: (i // tiles_per_group, 0, 0))],
        out_specs=pl.BlockSpec((tm, D_MODEL), lambda i: (i, 0)),
        out_shape=jax.ShapeDtypeStruct((n, D_MODEL), F32),
        compiler_params=pltpu.CompilerParams(dimension_semantics=("parallel",),
                                             vmem_limit_bytes=VMEM_LIMIT),
        name="merge",
    )(*ys, z, x, wb, wo, gain.reshape(1, D_MODEL), g1)


def _ffn_kernel(x_ref, gain_h_ref, sc_ref, sh_ref, wg_ref, wu_ref, wd_ref, gain_o_ref, g2_ref, o_ref,
                h_ref, acc_ref):
    j = pl.program_id(1)

    @pl.when(j == 0)
    def _():
        h = _rms(x_ref[...], gain_h_ref[...]) * (1.0 + sc_ref[0]) + sh_ref[0]
        h_ref[...] = h.astype(BF16)
        acc_ref[...] = jnp.zeros_like(acc_ref)

    h = h_ref[...]
    a = jnp.dot(h, wg_ref[...], preferred_element_type=F32)
    b = jnp.dot(h, wu_ref[...], preferred_element_type=F32)
    t = (a * jax.nn.sigmoid(a)) * b
    acc_ref[...] += jnp.dot(t.astype(BF16), wd_ref[...], preferred_element_type=F32)

    @pl.when(j == pl.num_programs(1) - 1)
    def _():
        o_ref[...] = x_ref[...] + g2_ref[0] * _rms(acc_ref[...], gain_o_ref[...])


def _ffn(x, gain_h, sc, sh, wg, wu, wd, gain_o, g2, tm):
    n = x.shape[0]
    tiles_per_group = n // tm // sc.shape[0]
    th = FFN_HIDDEN // 2
    mod_spec = pl.BlockSpec((1, sc.shape[1], D_MODEL), lambda i, j: (i // tiles_per_group, 0, 0))
    vec_spec = pl.BlockSpec((1, D_MODEL), lambda i, j: (0, 0))
    return pl.pallas_call(
        _ffn_kernel,
        grid=(n // tm, FFN_HIDDEN // th),
        in_specs=[pl.BlockSpec((tm, D_MODEL), lambda i, j: (i, 0)),
                  vec_spec, mod_spec, mod_spec,
                  pl.BlockSpec((D_MODEL, th), lambda i, j: (0, j)),
                  pl.BlockSpec((D_MODEL, th), lambda i, j: (0, j)),
                  pl.BlockSpec((th, D_MODEL), lambda i, j: (j, 0)),
                  vec_spec, mod_spec],
        out_specs=pl.BlockSpec((tm, D_MODEL), lambda i, j: (i, 0)),
        out_shape=jax.ShapeDtypeStruct((n, D_MODEL), F32),
        scratch_shapes=[pltpu.VMEM((tm, D_MODEL), BF16), pltpu.VMEM((tm, D_MODEL), F32)],
        compiler_params=pltpu.CompilerParams(dimension_semantics=("parallel", "arbitrary"),
                                             vmem_limit_bytes=VMEM_LIMIT),
        name="ffn",
    )(x, gain_h.reshape(1, D_MODEL), sc, sh, wg, wu, wd, gain_o.reshape(1, D_MODEL), g2)


TQ = 256
N_HEADS = 4
SCALE_64 = HEAD_DIM ** -0.5


def _dot_nt(a, b):
    return lax.dot_general(a, b, (((1,), (1,)), ((), ())), preferred_element_type=F32)


def _softmax_step(s, v, m_ref, l_ref, acc_ref):
    m_old = m_ref[...]
    m_new = jnp.maximum(m_old, jnp.max(s, axis=-1, keepdims=True))
    alpha = jnp.exp(m_old - m_new)
    p = jnp.exp(s - m_new)
    l_ref[...] = alpha * l_ref[...] + jnp.sum(p, axis=-1, keepdims=True)
    acc_ref[...] = alpha * acc_ref[...] + jnp.dot(p.astype(BF16), v, preferred_element_type=F32)
    m_ref[...] = m_new


def _softmax_reset(m_ref, l_ref, acc_ref):
    m_ref[...] = jnp.full(m_ref.shape, NEG, F32)
    l_ref[...] = jnp.zeros(l_ref.shape, F32)
    acc_ref[...] = jnp.zeros(acc_ref.shape, F32)


def _causal_neg(rows):
    i = lax.broadcasted_iota(jnp.int32, (TQ, TQ), 0)
    j = lax.broadcasted_iota(jnp.int32, (TQ, TQ), 1)
    neg = jnp.where(j <= i, 0.0, NEG).astype(F32)
    return jnp.concatenate([neg] * rows, axis=0)


MLA_QW = 2 * LANE


def _mla_prep_kernel(ql_ref, kv_ref, kr_ref, krs_ref, cq_ref, sq_ref, gq_ref, gkv_ref, w1_ref, wuk_ref,
                     qcat_ref, kf_ref, kb_ref):
    ql = ql_ref[...]
    qn = ql * lax.rsqrt(jnp.sum(ql * ql, axis=-1, keepdims=True) * (1.0 / MLA_Q_LORA) + EPS) * gq_ref[...]
    q1 = jnp.dot(qn.astype(BF16), w1_ref[...], preferred_element_type=F32)
    q_abs = jnp.dot(q1[:, :MIX_WIDTH].astype(BF16), wuk_ref[...], preferred_element_type=F32)
    cc, ss = cq_ref[...], sq_ref[...]
    parts = []
    for h in range(N_HEADS):
        r = q1[:, MIX_WIDTH + h * LANE:MIX_WIDTH + (h + 1) * LANE]
        rs = q1[:, MIX_WIDTH + (N_HEADS + h) * LANE:MIX_WIDTH + (N_HEADS + h + 1) * LANE]
        parts += [q_abs[:, h * LANE:(h + 1) * LANE], r * cc + rs * ss]
    qcat_ref[...] = jnp.concatenate(parts, axis=1).astype(BF16)
    c_new = _rms(kv_ref[...], gkv_ref[...])
    kr = kr_ref[...] * cc + krs_ref[...] * ss
    k = jnp.concatenate([c_new, kr], axis=1)
    kf_ref[...] = k
    kb_ref[...] = k.astype(BF16)


def _mla_prep(z, cq, sq, gq, gkv, w1, wuk, T, tm):
    n = z.shape[0]
    tiles_per_seq = T // tm
    row = lambda c: pl.BlockSpec((tm, LANE), lambda i: (i, c))
    tab = pl.BlockSpec((tm, LANE), lambda i: (i % tiles_per_seq, 0))
    full = lambda a: pl.BlockSpec(a.shape, lambda i: (0,) * a.ndim)
    return pl.pallas_call(
        _mla_prep_kernel,
        grid=(n // tm,),
        in_specs=[pl.BlockSpec((tm, 2 * LANE), lambda i: (i, Z_QLAT // (2 * LANE))),
                  row(Z_KVLAT // LANE), row(Z_KROPE // LANE), row(Z_KROPE_SW // LANE),
                  tab, tab, full(gq), full(gkv), full(w1), full(wuk)],
        out_specs=[pl.BlockSpec((tm, N_HEADS * MLA_QW), lambda i: (i, 0)),
                   pl.BlockSpec((tm, MLA_QW), lambda i: (i, 0)),
                   pl.BlockSpec((tm, MLA_QW), lambda i: (i, 0))],
        out_shape=[jax.ShapeDtypeStruct((n, N_HEADS * MLA_QW), BF16),
                   jax.ShapeDtypeStruct((n, MLA_QW), F32),
                   jax.ShapeDtypeStruct((n, MLA_QW), BF16)],
        compiler_params=pltpu.CompilerParams(dimension_semantics=("parallel",), vmem_limit_bytes=VMEM_LIMIT),
        name="mla_prep",
    )(z, z, z, z, cq, sq, gq, gkv, w1, wuk)


def _mla_attn_kernel(q_ref, k_ref, wuv_ref, o_ref, m_ref, l_ref, acc_ref):
    qi = pl.program_id(1)
    q = q_ref[...]
    qs = jnp.concatenate([q[:, h * MLA_QW:(h + 1) * MLA_QW] for h in range(N_HEADS)], axis=0)
    scale = (MLA_NOPE + MLA_ROPE) ** -0.5
    _softmax_reset(m_ref, l_ref, acc_ref)

    def step(kt, neg):
        k = k_ref[pl.ds(pl.multiple_of(kt * TQ, TQ), TQ), :]
        s = _dot_nt(qs, k) * scale
        if neg is not None:
            s = s + neg
        _softmax_step(s, k[:, :MLA_KV_LORA], m_ref, l_ref, acc_ref)

    lax.fori_loop(0, qi, lambda kt, c: (step(kt, None), c)[1], 0)
    step(qi, _causal_neg(N_HEADS))
    o_lat = (acc_ref[...] / l_ref[...]).astype(BF16)
    out = None
    for h in range(N_HEADS):
        t = jnp.dot(o_lat[h * TQ:(h + 1) * TQ], wuv_ref[h], preferred_element_type=F32)
        out = t if out is None else out + t
    o_ref[...] = out


def _mla_attn(qcat, kb, wuv, B, T):
    return pl.pallas_call(
        _mla_attn_kernel,
        grid=(B, T // TQ),
        in_specs=[pl.BlockSpec((TQ, N_HEADS * MLA_QW), lambda b, i: (b * (T // TQ) + i, 0)),
                  pl.BlockSpec((T, MLA_QW), lambda b, i: (b, 0)),
                  pl.BlockSpec(wuv.shape, lambda b, i: (0, 0, 0))],
        out_specs=pl.BlockSpec((TQ, MIX_WIDTH), lambda b, i: (b * (T // TQ) + i, 0)),
        out_shape=jax.ShapeDtypeStruct((B * T, MIX_WIDTH), F32),
        scratch_shapes=[pltpu.VMEM((N_HEADS * TQ, 1), F32), pltpu.VMEM((N_HEADS * TQ, 1), F32),
                        pltpu.VMEM((N_HEADS * TQ, MLA_KV_LORA), F32)],
        compiler_params=pltpu.CompilerParams(dimension_semantics=("parallel", "arbitrary"),
                                             vmem_limit_bytes=VMEM_LIMIT),
        name="mla_attn",
    )(qcat, kb, wuv)


def _fox_attn_kernel(q_ref, k_ref, v_ref, fq_ref, fk_ref, o_ref, fqb_ref, m_ref, l_ref, acc_ref):
    qi = pl.program_id(1)
    q = q_ref[...] * SCALE_64
    head = lax.broadcasted_iota(jnp.int32, (TQ, MIX_WIDTH), 1) // HEAD_DIM
    qs = jnp.concatenate([jnp.where(head == h, q, 0.0) for h in range(N_HEADS)], axis=0).astype(BF16)
    fq = fq_ref[...]
    for h in range(N_HEADS):
        fqb_ref[h * TQ:(h + 1) * TQ, :] = jnp.broadcast_to(fq[:, h:h + 1], (TQ, TQ))
    _softmax_reset(m_ref, l_ref, acc_ref)

    def step(kt, neg):
        off = pl.multiple_of(kt * TQ, TQ)
        k = k_ref[pl.ds(off, TQ), :].astype(BF16)
        v = v_ref[pl.ds(off, TQ), :].astype(BF16)
        fk = fk_ref[kt]
        fkb = jnp.concatenate([jnp.broadcast_to(fk[h:h + 1, :], (TQ, TQ)) for h in range(N_HEADS)], axis=0)
        s = _dot_nt(qs, k) + (fqb_ref[...] - fkb)
        if neg is not None:
            s = s + neg
        _softmax_step(s, v, m_ref, l_ref, acc_ref)

    lax.fori_loop(0, qi, lambda kt, c: (step(kt, None), c)[1], 0)
    step(qi, _causal_neg(N_HEADS))
    o = acc_ref[...] / l_ref[...]
    out = jnp.zeros((TQ, MIX_WIDTH), F32)
    for h in range(N_HEADS):
        out = jnp.where(head == h, o[h * TQ:(h + 1) * TQ], out)
    o_ref[...] = out


def _fox_attn(z, fq, fk, B, T):
    nq = T // TQ
    col = lambda c: pl.BlockSpec((T, MIX_WIDTH), lambda b, i: (b, c))
    return pl.pallas_call(
        _fox_attn_kernel,
        grid=(B, nq),
        in_specs=[pl.BlockSpec((TQ, MIX_WIDTH), lambda b, i: (b * nq + i, Z_FQ // MIX_WIDTH)),
                  col(Z_FK // MIX_WIDTH), col(Z_FV // MIX_WIDTH),
                  pl.BlockSpec((TQ, LANE), lambda b, i: (b * nq + i, 0)),
                  pl.BlockSpec((None, nq, 8, TQ), lambda b, i: (b, 0, 0, 0))],
        out_specs=pl.BlockSpec((TQ, MIX_WIDTH), lambda b, i: (b * nq + i, 0)),
        out_shape=jax.ShapeDtypeStruct((B * T, MIX_WIDTH), F32),
        scratch_shapes=[pltpu.VMEM((N_HEADS * TQ, TQ), F32),
                        pltpu.VMEM((N_HEADS * TQ, 1), F32), pltpu.VMEM((N_HEADS * TQ, 1), F32),
                        pltpu.VMEM((N_HEADS * TQ, MIX_WIDTH), F32)],
        compiler_params=pltpu.CompilerParams(dimension_semantics=("parallel", "arbitrary"),
                                             vmem_limit_bytes=VMEM_LIMIT),
        name="fox_attn",
    )(z, z, z, fq, fk)


def _nsa_tables(rel_bias, T):
    def tab(dist):
        oh = jax.nn.one_hot(_rel_bucket(dist), REL_BUCKETS, dtype=F32)
        b = jnp.einsum('rcb,bh->hrc', oh, rel_bias.astype(F32), precision=lax.Precision.HIGHEST)
        return b.reshape(N_HEADS * dist.shape[0], dist.shape[1])
    i = jnp.arange(TQ, dtype=jnp.int32)[:, None]
    j = jnp.arange(TQ, dtype=jnp.int32)[None, :]
    tile4 = lambda m: jnp.concatenate([m] * N_HEADS, axis=0)
    far = tab(jnp.full((TQ, TQ), REL_MAX_DIST, jnp.int32))
    prev = tab(TQ + i - j)
    diag = jnp.where(tile4(j <= i), tab(i - j), NEG)
    edge = jnp.where(tile4(j >= i), far, NEG)
    tb = jnp.stack([far, prev, diag, edge])
    n_cb = T // NSA_CMP_BLOCK
    pos = jnp.arange(T, dtype=jnp.int32)[:, None]
    cb_end = jnp.arange(n_cb, dtype=jnp.int32)[None, :] * NSA_CMP_BLOCK + (NSA_CMP_BLOCK - 1)
    bc = tab(pos - cb_end).reshape(N_HEADS, T // TQ, TQ, n_cb)
    bc = jnp.moveaxis(bc, 1, 0).reshape(T // TQ, N_HEADS * TQ, n_cb)
    return tb, bc


def _nsa_attn_kernel(q_ref, kcvc_ref, ksvs_ref, kwvw_ref, ng_ref, wkv_ref, tb_ref, bc_ref, o_ref,
                     cb_ref, expand_ref, negsel_ref, m_ref, l_ref, acc_ref):
    qi = pl.program_id(1)
    T = ksvs_ref.shape[0]
    n_cb = T // NSA_CMP_BLOCK
    n_kt = T // TQ

    @pl.when(qi == 0)
    def _():
        x = kcvc_ref[...].reshape(n_cb, NSA_CMP_BLOCK, LANE) * wkv_ref[...][None]
        cb_ref[...] = jnp.sum(x, axis=1)
        r = lax.broadcasted_iota(jnp.int32, (n_cb, T), 0)
        key = lax.broadcasted_iota(jnp.int32, (n_cb, T), 1)
        expand_ref[...] = jnp.where(r == 2 * (key // NSA_SEL_BLOCK), 1.0, 0.0).astype(BF16)

    lane = lax.broadcasted_iota(jnp.int32, (TQ, LANE), 1)
    lo = lane < HEAD_DIM
    q = q_ref[...] * SCALE_64
    t0, t1 = q[:, :LANE], q[:, LANE:]
    qs = jnp.concatenate([jnp.where(lo, t0, 0.0), jnp.where(lo, pltpu.roll(t0, HEAD_DIM, 1), 0.0),
                          jnp.where(lo, t1, 0.0), jnp.where(lo, pltpu.roll(t1, HEAD_DIM, 1), 0.0)],
                         axis=0).astype(BF16)

    cb = cb_ref[...].astype(BF16)
    qpos = qi * TQ + lax.broadcasted_iota(jnp.int32, (TQ, n_cb), 0)
    cb_end = lax.broadcasted_iota(jnp.int32, (TQ, n_cb), 1) * NSA_CMP_BLOCK + (NSA_CMP_BLOCK - 1)
    valid = jnp.concatenate([cb_end <= qpos] * N_HEADS, axis=0)
    lc = jnp.where(valid, _dot_nt(qs, cb) + bc_ref[0], NEG)
    e = jnp.where(valid, jnp.exp(lc - jnp.max(lc, axis=-1, keepdims=True)), 0.0)
    pc = e / jnp.maximum(jnp.sum(e, axis=-1, keepdims=True), 1e-30)
    o_c = jnp.dot(pc.astype(BF16), cb, preferred_element_type=F32)

    psum = pc[0:TQ] + pc[TQ:2 * TQ] + pc[2 * TQ:3 * TQ] + pc[3 * TQ:4 * TQ]
    imp = (psum + pltpu.roll(psum, n_cb - 1, 1)).T
    row = lax.broadcasted_iota(jnp.int32, (n_cb, TQ), 0)
    qpos_t = qi * TQ + lax.broadcasted_iota(jnp.int32, (n_cb, TQ), 1)
    blk = row // 2
    score = jnp.where(qpos_t // NSA_SEL_BLOCK == blk, FORCE,
                      jnp.where(blk * NSA_SEL_BLOCK <= qpos_t, imp, NEG))
    score = jnp.where(row % 2 == 0, score, 2 * NEG)
    cnt = jnp.zeros((n_cb, TQ), F32)
    for i in range(0, n_cb, 2):
        si = score[i:i + 1, :]
        tie = jnp.where(row > i, 1.0, 0.0)
        cnt = cnt + jnp.where(si > score, 1.0, jnp.where(si == score, tie, 0.0))
    sel = jnp.where(cnt < NSA_TOPK, jnp.where(score > NEG / 2, 1.0, 0.0), 0.0).T
    keymask = jnp.dot(sel.astype(BF16), expand_ref[...], preferred_element_type=F32)
    for kt in range(n_kt):
        negsel_ref[kt] = (keymask[:, kt * TQ:(kt + 1) * TQ] - 1.0) * (-NEG)

    def step(kt, tab, kv_ref, use_sel):
        kv = kv_ref[pl.ds(pl.multiple_of(kt * TQ, TQ), TQ), :].astype(BF16)
        s = _dot_nt(qs, kv) + tb_ref[tab]
        if use_sel:
            ns = negsel_ref[kt]
            s = s + jnp.concatenate([ns] * N_HEADS, axis=0)
        _softmax_step(s, kv, m_ref, l_ref, acc_ref)

    _softmax_reset(m_ref, l_ref, acc_ref)
    lax.fori_loop(0, qi + 1,
                  lambda kt, c: (step(kt, jnp.maximum(kt - qi + 2, 0), ksvs_ref, True), c)[1], 0)
    o_s = acc_ref[...] / l_ref[...]
    _softmax_reset(m_ref, l_ref, acc_ref)
    lax.fori_loop(jnp.maximum(qi - 2, 0), qi + 1,
                  lambda kt, c: (step(kt, jnp.where(kt == qi - 2, 3, kt - qi + 2), kwvw_ref, False), c)[1], 0)
    o_w = acc_ref[...] / l_ref[...]

    g = jax.nn.sigmoid(ng_ref[...])
    mixed = []
    for h in range(N_HEADS):
        rows = slice(h * TQ, (h + 1) * TQ)
        mixed.append(g[:, h:h + 1] * o_c[rows] + g[:, N_HEADS + h:N_HEADS + h + 1] * o_s[rows]
                     + g[:, 2 * N_HEADS + h:2 * N_HEADS + h + 1] * o_w[rows])
    o_ref[...] = jnp.concatenate([jnp.where(lo, pltpu.roll(mixed[0], HEAD_DIM, 1), mixed[1]),
                                  jnp.where(lo, pltpu.roll(mixed[2], HEAD_DIM, 1), mixed[3])], axis=1)


def _nsa_attn(z, wkv, tb, bc, B, T):
    nq = T // TQ
    n_cb = T // NSA_CMP_BLOCK
    kv = lambda c: pl.BlockSpec((T, LANE), lambda b, i: (b, c))
    return pl.pallas_call(
        _nsa_attn_kernel,
        grid=(B, nq),
        in_specs=[pl.BlockSpec((TQ, MIX_WIDTH), lambda b, i: (b * nq + i, Z_NQ // MIX_WIDTH)),
                  kv(Z_KCVC // LANE), kv(Z_KSVS // LANE), kv(Z_KWVW // LANE),
                  pl.BlockSpec((TQ, LANE), lambda b, i: (b * nq + i, Z_NG // LANE)),
                  pl.BlockSpec(wkv.shape, lambda b, i: (0, 0)),
                  pl.BlockSpec(tb.shape, lambda b, i: (0, 0, 0)),
                  pl.BlockSpec((1, N_HEADS * TQ, n_cb), lambda b, i: (i, 0, 0))],
        out_specs=pl.BlockSpec((TQ, MIX_WIDTH), lambda b, i: (b * nq + i, 0)),
        out_shape=jax.ShapeDtypeStruct((B * T, MIX_WIDTH), F32),
        scratch_shapes=[pltpu.VMEM((n_cb, LANE), F32), pltpu.VMEM((n_cb, T), BF16),
                        pltpu.VMEM((nq, TQ, TQ), F32),
                        pltpu.VMEM((N_HEADS * TQ, 1), F32), pltpu.VMEM((N_HEADS * TQ, 1), F32),
                        pltpu.VMEM((N_HEADS * TQ, LANE), F32)],
        compiler_params=pltpu.CompilerParams(dimension_semantics=("parallel", "arbitrary"),
                                             vmem_limit_bytes=VMEM_LIMIT),
        name="nsa_attn",
    )(z, z, z, z, z, wkv, tb, bc)


def _rms_norm(x, g):
    xf = x.astype(F32)
    y = xf * lax.rsqrt(jnp.mean(xf * xf, axis=-1, keepdims=True) + EPS)
    return (y * g.astype(F32)).astype(x.dtype)


def _rope(x, pos):
    half = x.shape[-1] // 2
    freq = ROPE_THETA ** (-jnp.arange(half, dtype=F32) / half)
    ang = pos.astype(F32)[:, None] * freq[None, :]
    ang = ang.reshape(ang.shape[0], *([1] * (x.ndim - 3)), half)
    cos, sin = jnp.cos(ang), jnp.sin(ang)
    x1, x2 = x[..., :half], x[..., half:]
    return jnp.concatenate([x1 * cos - x2 * sin, x1 * sin + x2 * cos], axis=-1)


def _rel_bucket(dist):
    n = jnp.maximum(dist, 0)
    exact = REL_BUCKETS // 2
    large = exact + (jnp.log(jnp.maximum(n, 1).astype(F32) / exact)
                     / math.log(REL_MAX_DIST / exact) * (REL_BUCKETS - exact)).astype(jnp.int32)
    large = jnp.minimum(large, REL_BUCKETS - 1)
    return jnp.where(n < exact, n, large)


def _t5_bias(rel_bias, dist):
    b = rel_bias[_rel_bucket(dist)].astype(F32)
    return jnp.moveaxis(b, -1, -3)


def _masked_softmax(logits, mask):
    lg = jnp.where(mask, logits, NEG)
    m = jnp.max(lg, axis=-1, keepdims=True)
    e = jnp.where(mask, jnp.exp(lg - m), 0.0)
    return e / jnp.maximum(jnp.sum(e, axis=-1, keepdims=True), 1e-30)


def _attend(logits, values, spec):
    lg = logits[0] if len(logits) == 1 else jnp.concatenate(logits, axis=-1)
    p = jax.nn.softmax(lg, axis=-1)
    out, start = None, 0
    for l_, v_ in zip(logits, values):
        n = l_.shape[-1]
        o = jnp.einsum(spec, p[..., start:start + n], v_)
        out = o if out is None else out + o
        start += n
    return out


def _map_query_blocks(fn, arrays, qpos):
    T = qpos.shape[0]
    qb = Q_BLOCK if T % Q_BLOCK == 0 else T
    nb = T // qb

    def split(a):
        return jnp.moveaxis(a.reshape(a.shape[0], nb, qb, *a.shape[2:]), 1, 0)
    xs = tuple(split(a) for a in arrays) + (qpos.reshape(nb, qb),)
    out = lax.map(lambda args: fn(*args), xs)
    out = jnp.moveaxis(out, 0, 1)
    return out.reshape(out.shape[0], T, *out.shape[3:])


def _complex_affine_combine(e1, e2):
    a1r, a1i, b1r, b1i = e1
    a2r, a2i, b2r, b2i = e2
    return (a2r * a1r - a2i * a1i, a2r * a1i + a2i * a1r,
            a2r * b1r - a2i * b1i + b2r, a2r * b1i + a2i * b1r + b2i)


def _gather_past(l, cache_mla, cache_fox_k, cache_fox_v, cache_fox_logf, cache_nsa,
                 state_nsa_win, state_s5, page_table):
    nb, n_pages = page_table.shape

    def pages(cache):
        g = cache[l, page_table]
        return g.reshape(nb, n_pages * g.shape[2], *g.shape[3:])
    return {'mla': pages(cache_mla), 'fox_k': pages(cache_fox_k), 'fox_v': pages(cache_fox_v),
            'fox_logf': pages(cache_fox_logf), 'nsa': pages(cache_nsa),
            'win': state_nsa_win[l], 's5': state_s5[l]}


def _mla(q_lat, kv_lat, k_rope, pos, past, l, w):
    B, T, _ = q_lat.shape
    qn = _rms_norm(q_lat, w['mla_g_q'][l])
    q = jnp.einsum('btr,rhe->bthe', qn, w['mla_w_uq'][l])
    q_nope = q[..., :MLA_NOPE]
    q_rope = _rope(q[..., MLA_NOPE:], pos)
    c_new = _rms_norm(kv_lat, w['mla_g_kv'][l])
    kr_new = _rope(k_rope, pos)
    q_abs = jnp.einsum('bthn,lhn->bthl', q_nope, w['mla_w_uk'][l])
    segs = [(c_new, kr_new, pos)]
    if past is not None:
        n_past = past['mla'].shape[1]
        segs = [(past['mla'][..., :MLA_KV_LORA], past['mla'][..., MLA_KV_LORA:],
                 jnp.arange(n_past, dtype=jnp.int32))] + segs
    scale = (MLA_NOPE + MLA_ROPE) ** -0.5

    def block(qa, qr, qp):
        logits = []
        for c_, kr_, kp in segs:
            s = (jnp.einsum('bqhl,bkl->bhqk', qa, c_, preferred_element_type=F32)
                 + jnp.einsum('bqhr,bkr->bhqk', qr, kr_, preferred_element_type=F32)) * scale
            logits.append(jnp.where(kp[None, :] <= qp[:, None], s, NEG))
        return _attend(logits, [sg[0] for sg in segs], 'bhqk,bkl->bqhl')
    o_lat = _map_query_blocks(block, (q_abs, q_rope), pos)
    o = jnp.einsum('bthl,lhv->bthv', o_lat, w['mla_w_uv'][l]).reshape(B, T, MLA_HEADS * MLA_V)
    return o, jnp.concatenate([c_new, kr_new], axis=-1)


def _s5(u, past, l, w):
    B, T, _ = u.shape
    lam_re = w['s5_lambda_re'][l]
    lam_im = w['s5_lambda_im'][l]
    dt = jnp.exp(w['s5_log_dt'][l])[:, None]
    mag = jnp.exp(lam_re * dt)
    a_re, a_im = mag * jnp.cos(lam_im * dt), mag * jnp.sin(lam_im * dt)
    den = lam_re * lam_re + lam_im * lam_im
    coef_re = ((a_re - 1.0) * lam_re + a_im * lam_im) / den
    coef_im = (a_im * lam_re - (a_re - 1.0) * lam_im) / den
    b_re, b_im = w['s5_b_re'][l], w['s5_b_im'][l]
    bb_re = coef_re[..., None] * b_re - coef_im[..., None] * b_im
    bb_im = coef_re[..., None] * b_im + coef_im[..., None] * b_re
    ug = u.reshape(B, T, S5_GROUPS, S5_GROUP)
    bu_re = jnp.einsum('btgi,gpi->btgp', ug, bb_re)
    bu_im = jnp.einsum('btgi,gpi->btgp', ug, bb_im)
    if past is not None:
        x0_re, x0_im = past['s5'][:, 0], past['s5'][:, 1]
        bu_re = bu_re.at[:, 0].add(a_re * x0_re - a_im * x0_im)
        bu_im = bu_im.at[:, 0].add(a_re * x0_im + a_im * x0_re)
    A_re = jnp.broadcast_to(a_re, bu_re.shape)
    A_im = jnp.broadcast_to(a_im, bu_im.shape)
    _, _, x_re, x_im = lax.associative_scan(_complex_affine_combine, (A_re, A_im, bu_re, bu_im), axis=1)
    y = (jnp.einsum('btgp,gip->btgi', x_re, w['s5_c_re'][l])
         - jnp.einsum('btgp,gip->btgi', x_im, w['s5_c_im'][l]))
    y = y.reshape(B, T, MIX_WIDTH) + w['s5_d'][l] * u
    y = jax.nn.gelu(y)
    y = y * jax.nn.sigmoid(jnp.einsum('btw,wv->btv', y, w['s5_w_glu'][l]))
    state = jnp.stack([x_re[:, -1], x_im[:, -1]], axis=1)
    return y, state


def _fox(q, k, v, f_logit, pos, past, l, w):
    B, T, _ = q.shape
    q = q.reshape(B, T, FOX_HEADS, HEAD_DIM)
    k = k.reshape(B, T, FOX_HEADS, HEAD_DIM)
    v = v.reshape(B, T, FOX_HEADS, HEAD_DIM)
    logf = jax.nn.log_sigmoid(f_logit + w['fox_b_f'][l])
    if past is None:
        F_new = jnp.cumsum(logf, axis=1)
        segs = [(k, v, F_new, pos)]
    else:
        F_past = jnp.cumsum(past['fox_logf'], axis=1)
        F_new = F_past[:, -1:] + jnp.cumsum(logf, axis=1)
        segs = [(past['fox_k'], past['fox_v'], F_past, jnp.arange(F_past.shape[1], dtype=jnp.int32)),
                (k, v, F_new, pos)]
    scale = HEAD_DIM ** -0.5

    def block(qq, fq, qp):
        fq_t = jnp.swapaxes(fq, 1, 2)[..., None]
        logits = []
        for k_, _, fk, kp in segs:
            s = (jnp.einsum('bqhd,bkhd->bhqk', qq, k_, preferred_element_type=F32) * scale
                 + (fq_t - jnp.swapaxes(fk, 1, 2)[:, :, None, :]))
            logits.append(jnp.where(kp[None, :] <= qp[:, None], s, NEG))
        return _attend(logits, [sg[1] for sg in segs], 'bhqk,bkhd->bqhd')
    o = _map_query_blocks(block, (q, F_new), pos)
    return o.reshape(B, T, MIX_WIDTH), k, v, logf


def _nsa(q, kc, vc, ks, vs, kw, vw, g_logit, pos, past, l, w):
    B, T, _ = q.shape
    q = q.reshape(B, T, NSA_HEADS, HEAD_DIM)
    g = jax.nn.sigmoid(g_logit).reshape(B, T, 3, NSA_HEADS)
    rows = jnp.stack([kc, vc, ks, vs], axis=2)
    win_rows = jnp.stack([kw, vw], axis=2)
    if past is None:
        kc_all, vc_all, ks_all, vs_all = kc, vc, ks, vs
        win_ctx = win_rows
        keep = min(NSA_WINDOW, T)
    else:
        pr = past['nsa']
        kc_all = jnp.concatenate([pr[:, :, 0], kc], axis=1)
        vc_all = jnp.concatenate([pr[:, :, 1], vc], axis=1)
        ks_all = jnp.concatenate([pr[:, :, 2], ks], axis=1)
        vs_all = jnp.concatenate([pr[:, :, 3], vs], axis=1)
        win_ctx = jnp.concatenate([past['win'], win_rows], axis=1)
        keep = past['win'].shape[1]
    n_keys = kc_all.shape[1]
    win_base = n_keys - win_ctx.shape[1]
    n_cb = -(-n_keys // NSA_CMP_BLOCK)
    n_sb = -(-n_keys // NSA_SEL_BLOCK)
    ratio = NSA_SEL_BLOCK // NSA_CMP_BLOCK
    k_sel = min(NSA_TOPK, n_sb)

    def pad_rows(a, n):
        return jnp.pad(a, ((0, 0), (0, n - a.shape[1]), (0, 0)))
    kcb = jnp.einsum('bnid,i->bnd', pad_rows(kc_all, n_cb * NSA_CMP_BLOCK).reshape(B, n_cb, NSA_CMP_BLOCK, HEAD_DIM),
                     w['nsa_w_cmp_k'][l])
    vcb = jnp.einsum('bnid,i->bnd', pad_rows(vc_all, n_cb * NSA_CMP_BLOCK).reshape(B, n_cb, NSA_CMP_BLOCK, HEAD_DIM),
                     w['nsa_w_cmp_v'][l])
    cb_end = jnp.arange(n_cb, dtype=jnp.int32) * NSA_CMP_BLOCK + (NSA_CMP_BLOCK - 1)
    sb_start = jnp.arange(n_sb, dtype=jnp.int32) * NSA_SEL_BLOCK
    ks_pad = pad_rows(ks_all, n_sb * NSA_SEL_BLOCK)
    vs_pad = pad_rows(vs_all, n_sb * NSA_SEL_BLOCK)
    win_pad = jnp.pad(win_ctx, ((0, 0), (NSA_WINDOW, 0), (0, 0), (0, 0)))
    rel_bias = w['rel_bias']
    scale = HEAD_DIM ** -0.5

    def block(qq, gb, qp):
        nq = qp.shape[0]
        lc = (jnp.einsum('bqhd,bnd->bhqn', qq, kcb, preferred_element_type=F32) * scale
              + _t5_bias(rel_bias, qp[:, None] - cb_end[None, :]))
        pc = _masked_softmax(lc, cb_end[None, :] <= qp[:, None])
        o_c = jnp.einsum('bhqn,bnd->bqhd', pc, vcb)
        imp = jnp.pad(pc.sum(axis=1), ((0, 0), (0, 0), (0, n_sb * ratio - n_cb)))
        imp = imp.reshape(B, nq, n_sb, ratio).sum(-1)
        cur = (qp[:, None] // NSA_SEL_BLOCK) == jnp.arange(n_sb, dtype=jnp.int32)[None, :]
        score = jnp.where(cur, FORCE, jnp.where(sb_start[None, :] <= qp[:, None], imp, NEG))
        top_v, top_i = lax.top_k(score, k_sel)
        idx = (top_i[..., None] * NSA_SEL_BLOCK + jnp.arange(NSA_SEL_BLOCK, dtype=jnp.int32)).reshape(
            B, nq, k_sel * NSA_SEL_BLOCK)
        valid = jnp.repeat(top_v > NEG / 2, NSA_SEL_BLOCK, axis=-1) & (idx <= qp[None, :, None])
        ks_g = jax.vmap(lambda a, i: a[i])(ks_pad, idx)
        vs_g = jax.vmap(lambda a, i: a[i])(vs_pad, idx)
        ls = (jnp.einsum('bqhd,bqkd->bhqk', qq, ks_g, preferred_element_type=F32) * scale
              + _t5_bias(rel_bias, qp[None, :, None] - idx))
        ps = _masked_softmax(ls, valid[:, None])
        o_s = jnp.einsum('bhqk,bqkd->bqhd', ps, vs_g)
        start = qp[0] - win_base
        wk = lax.dynamic_slice_in_dim(win_pad, start, NSA_WINDOW + nq, axis=1)
        wp = qp[0] - NSA_WINDOW + jnp.arange(NSA_WINDOW + nq, dtype=jnp.int32)
        mask_w = ((wp[None, :] >= 0) & (wp[None, :] <= qp[:, None])
                  & (qp[:, None] - wp[None, :] <= NSA_WINDOW))
        lw = (jnp.einsum('bqhd,bkd->bhqk', qq, wk[:, :, 0], preferred_element_type=F32) * scale
              + _t5_bias(rel_bias, qp[:, None] - wp[None, :]))
        pw = jax.nn.softmax(jnp.where(mask_w, lw, NEG), axis=-1)
        o_w = jnp.einsum('bhqk,bkd->bqhd', pw, wk[:, :, 1])
        return (gb[:, :, 0, :, None] * o_c + gb[:, :, 1, :, None] * o_s
                + gb[:, :, 2, :, None] * o_w)
    o = _map_query_blocks(block, (q, g), pos)
    return o.reshape(B, T, MIX_WIDTH), rows, win_ctx[:, win_ctx.shape[1] - keep:]


def _zcol(z, B, T, start, width):
    return z[:, start:start + width].reshape(B, T, width)


def _mixers_prompt(z, B, T, l, w, wc):
    col = functools.partial(_zcol, z, B, T)
    qcat, kf, kb = _mla_prep(z, wc['rope_c'], wc['rope_s'], wc['mla_gq'][l], wc['mla_gkv'][l],
                             wc['mla_w1'][l], wc['mla_wuk'][l], T, 512)
    y_a = _mla_attn(qcat, kb, wc['mla_wuv'][l], B, T)
    mla_rows = kf[:, :MLA_KV_LORA + MLA_ROPE].reshape(B, T, MLA_KV_LORA + MLA_ROPE)

    logf = jax.nn.log_sigmoid(col(Z_NG + 3 * NSA_HEADS, FOX_HEADS) + w['fox_b_f'][l])
    fsum = jnp.cumsum(logf, axis=1)
    fq = jnp.pad(fsum.reshape(B * T, FOX_HEADS), ((0, 0), (0, LANE - FOX_HEADS)))
    fk = jnp.pad(jnp.swapaxes(fsum.reshape(B, T // TQ, TQ, FOX_HEADS), 2, 3),
                 ((0, 0), (0, 0), (0, 8 - FOX_HEADS), (0, 0)))
    y_c = _fox_attn(z, fq, fk, B, T)
    fox_k = col(Z_FK, MIX_WIDTH).reshape(B, T, FOX_HEADS, HEAD_DIM)
    fox_v = col(Z_FV, MIX_WIDTH).reshape(B, T, FOX_HEADS, HEAD_DIM)

    y_d = _nsa_attn(z, wc['nsa_wkv'][l], wc['nsa_tb'], wc['nsa_bc'], B, T)
    nsa_rows = col(Z_KCVC, 2 * LANE).reshape(B, T, 4, HEAD_DIM)
    keep = min(NSA_WINDOW, T)
    win_state = col(Z_KWVW, LANE)[:, T - keep:].reshape(B, keep, 2, HEAD_DIM)

    y_b, s5_state = _s5(col(Z_S5U, MIX_WIDTH), None, l, w)
    return y_a, y_b, y_c, y_d, (mla_rows, fox_k, fox_v, logf, nsa_rows, win_state, s5_state)


def _prep_weights(w_in, mla_g_q, mla_g_kv, mla_w_uq, mla_w_uk, mla_w_uv, nsa_w_cmp_k, nsa_w_cmp_v, rel_bias, T):
    def cols(a, b):
        return w_in[:, :, a:b]

    def zeros(n):
        return jnp.zeros((DEPTH, D_MODEL, n), w_in.dtype)
    o = [0] + [int(v) for v in np.cumsum(IN_SPLITS)]
    (o_ql, o_kv, o_kr, o_s5, o_fq, o_fk, o_fv, o_ff, o_nq, o_kc, o_vc, o_ks, o_vs, o_kw, o_vw, o_ng, o_gate,
     o_end) = o
    half = MLA_ROPE // 2
    w_in_r = jnp.concatenate([
        cols(o_gate, o_end), cols(o_fq, o_fk), cols(o_fk, o_fv), cols(o_fv, o_ff), cols(o_nq, o_kc),
        cols(o_s5, o_fq), cols(o_ql, o_kv), zeros(2 * LANE - MLA_Q_LORA), cols(o_kv, o_kr),
        cols(o_kr, o_s5), zeros(LANE - MLA_ROPE),
        cols(o_kr + half, o_s5), cols(o_kr, o_kr + half), zeros(LANE - MLA_ROPE),
        cols(o_kc, o_ks), cols(o_ks, o_kw), cols(o_kw, o_ng), cols(o_ng, o_gate), cols(o_ff, o_nq),
        zeros(LANE - 3 * NSA_HEADS - FOX_HEADS)], axis=-1)
    assert w_in_r.shape[-1] == Z_COLS

    def rope_cols(r):
        return jnp.pad(r, ((0, 0), (0, 0), (0, 0), (0, LANE - MLA_ROPE))).reshape(DEPTH, MLA_Q_LORA, N_HEADS * LANE)
    nope = mla_w_uq[..., :MLA_NOPE].reshape(DEPTH, MLA_Q_LORA, MIX_WIDTH)
    rope = mla_w_uq[..., MLA_NOPE:]
    rope_sw = jnp.concatenate([rope[..., half:], rope[..., :half]], axis=-1)
    w1 = jnp.concatenate([nope, rope_cols(rope), rope_cols(rope_sw)], axis=-1)
    w1 = jnp.pad(w1, ((0, 0), (0, 2 * LANE - MLA_Q_LORA), (0, 0)))
    eye = jnp.eye(N_HEADS, dtype=mla_w_uk.dtype)
    wuk = jnp.einsum('dlhn,hg->dhngl', mla_w_uk, eye).reshape(DEPTH, MIX_WIDTH, N_HEADS * MLA_KV_LORA)
    wuv = jnp.einsum('dlhv,hg->dhlgv', mla_w_uv, eye).reshape(DEPTH, N_HEADS, MLA_KV_LORA, MIX_WIDTH)

    freq = ROPE_THETA ** (-jnp.arange(half, dtype=F32) / half)
    ang = jnp.arange(T, dtype=jnp.int32).astype(F32)[:, None] * freq[None, :]
    cos, sin = jnp.cos(ang), jnp.sin(ang)
    lane_pad = ((0, 0), (0, LANE - MLA_ROPE))
    wkv = jnp.concatenate([jnp.broadcast_to(nsa_w_cmp_k[:, :, None], (DEPTH, NSA_CMP_BLOCK, HEAD_DIM)),
                           jnp.broadcast_to(nsa_w_cmp_v[:, :, None], (DEPTH, NSA_CMP_BLOCK, HEAD_DIM))], axis=-1)
    tb, bc = _nsa_tables(rel_bias, T)
    return {'w_in': w_in_r.astype(BF16),
            'mla_w1': w1.astype(BF16), 'mla_wuk': wuk.astype(BF16), 'mla_wuv': wuv.astype(BF16),
            'mla_gq': jnp.pad(mla_g_q, ((0, 0), (0, 2 * LANE - MLA_Q_LORA))).reshape(DEPTH, 1, 2 * LANE),
            'mla_gkv': mla_g_kv.reshape(DEPTH, 1, MLA_KV_LORA),
            'rope_c': jnp.pad(jnp.concatenate([cos, cos], axis=-1), lane_pad),
            'rope_s': jnp.pad(jnp.concatenate([-sin, sin], axis=-1), lane_pad),
            'nsa_wkv': wkv.astype(F32), 'nsa_tb': tb, 'nsa_bc': bc}


def _forward(x, mods, pos, w, wc, caches, tm):
    B, T, _ = x.shape
    n = B * T
    xf = x.reshape(n, D_MODEL)
    per_row = T == 1
    outs = [[] for _ in range(7)]
    for l in range(DEPTH):
        past = None if caches is None else _gather_past(l, *caches)
        m6 = mods[l].reshape(B, 6, D_MODEL)
        if per_row:
            sh1, sc1, g1, sh2, sc2, g2 = (m6[:, i].reshape(1, n, D_MODEL) for i in range(6))
        else:
            sh1, sc1, g1, sh2, sc2, g2 = (m6[:, i].reshape(B, 1, D_MODEL) for i in range(6))
        gains = w['norm_gains'][l]
        z = _inproj(xf, gains[0], sc1, sh1, wc['w_in'][l], tm)
        col = functools.partial(_zcol, z, B, T)
        if caches is None:
            y_a, y_b, y_c, y_d, states = _mixers_prompt(z, B, T, l, w, wc)
            mla_rows, fox_k, fox_v, fox_logf, nsa_rows, win_state, s5_state = states
        else:
            y_a, mla_rows = _mla(col(Z_QLAT, MLA_Q_LORA), col(Z_KVLAT, MLA_KV_LORA), col(Z_KROPE, MLA_ROPE),
                                 pos, past, l, w)
            y_b, s5_state = _s5(col(Z_S5U, MIX_WIDTH), past, l, w)
            y_c, fox_k, fox_v, fox_logf = _fox(col(Z_FQ, MIX_WIDTH), col(Z_FK, MIX_WIDTH), col(Z_FV, MIX_WIDTH),
                                               col(Z_NG + 3 * NSA_HEADS, FOX_HEADS), pos, past, l, w)
            y_d, nsa_rows, win_state = _nsa(
                col(Z_NQ, MIX_WIDTH), col(Z_KCVC, HEAD_DIM), col(Z_KCVC + HEAD_DIM, HEAD_DIM),
                col(Z_KSVS, HEAD_DIM), col(Z_KSVS + HEAD_DIM, HEAD_DIM), col(Z_KWVW, HEAD_DIM),
                col(Z_KWVW + HEAD_DIM, HEAD_DIM), col(Z_NG, 3 * NSA_HEADS), pos, past, l, w)
        ys = tuple(y.reshape(n, MIX_WIDTH) for y in (y_a, y_b, y_c, y_d))
        xf = _merge(ys, z, xf, wc['w_branch'][l], wc['w_out'][l], gains[1], g1, tm)
        xf = _ffn(xf, gains[2], sc2, sh2, wc['w_ffn_gate'][l], wc['w_ffn_up'][l], wc['w_ffn_down'][l],
                  gains[3], g2, tm)
        for o, s in zip(outs, (mla_rows, fox_k, fox_v, fox_logf, nsa_rows, win_state, s5_state)):
            o.append(s)
    return xf.reshape(B, T, D_MODEL), tuple(jnp.stack(o) for o in outs)


def kernel(x_prompt, x_sample, c_prompt, c_sample, cache_mla, cache_fox_k, cache_fox_v, cache_fox_logf, cache_nsa, state_nsa_win, state_s5, page_table, w_ada, b_ada, norm_gains, w_in, mla_g_q, mla_g_kv, mla_w_uq, mla_w_uk, mla_w_uv, s5_lambda_re, s5_lambda_im, s5_log_dt, s5_b_re, s5_b_im, s5_c_re, s5_c_im, s5_d, s5_w_glu, fox_b_f, nsa_w_cmp_k, nsa_w_cmp_v, rel_bias, w_branch, w_out, w_ffn_gate, w_ffn_up, w_ffn_down):
    w = {'norm_gains': norm_gains,
         'mla_g_q': mla_g_q, 'mla_g_kv': mla_g_kv, 'mla_w_uq': mla_w_uq, 'mla_w_uk': mla_w_uk,
         'mla_w_uv': mla_w_uv, 's5_lambda_re': s5_lambda_re, 's5_lambda_im': s5_lambda_im,
         's5_log_dt': s5_log_dt, 's5_b_re': s5_b_re, 's5_b_im': s5_b_im, 's5_c_re': s5_c_re,
         's5_c_im': s5_c_im, 's5_d': s5_d, 's5_w_glu': s5_w_glu, 'fox_b_f': fox_b_f,
         'nsa_w_cmp_k': nsa_w_cmp_k, 'nsa_w_cmp_v': nsa_w_cmp_v, 'rel_bias': rel_bias}
    wc = _prep_weights(w_in, mla_g_q, mla_g_kv, mla_w_uq, mla_w_uk, mla_w_uv, nsa_w_cmp_k, nsa_w_cmp_v, rel_bias,
                       x_prompt.shape[1])
    wc.update({'w_branch': w_branch.astype(BF16), 'w_out': w_out.astype(BF16),
               'w_ffn_gate': w_ffn_gate.astype(BF16), 'w_ffn_up': w_ffn_up.astype(BF16),
               'w_ffn_down': w_ffn_down.astype(BF16)})
    nb_p = c_prompt.shape[0]
    mods = _ada(jnp.concatenate([c_prompt, c_sample], axis=0), w_ada, b_ada)
    past_len = page_table.shape[1] * cache_mla.shape[2]
    pos_p = jnp.arange(x_prompt.shape[1], dtype=jnp.int32)
    pos_s = past_len + jnp.arange(x_sample.shape[1], dtype=jnp.int32)
    caches = (cache_mla, cache_fox_k, cache_fox_v, cache_fox_logf, cache_nsa,
              state_nsa_win, state_s5, page_table)
    y_prompt, st_p = _forward(x_prompt, mods[:, :nb_p], pos_p, w, wc, None, 512)
    y_sample, st_s = _forward(x_sample, mods[:, nb_p:], pos_s, w, wc, caches, x_sample.shape[0])
    mla_p, fox_k_p, fox_v_p, fox_logf_p, nsa_p, win_p, s5_p = st_p
    mla_s, fox_k_s, fox_v_s, fox_logf_s, nsa_s, win_s, s5_s = st_s
    return (y_prompt, y_sample, mla_p, mla_s, fox_k_p, fox_k_s, fox_v_p, fox_v_s,
            fox_logf_p, fox_logf_s, nsa_p, nsa_s, win_p, win_s, s5_p, s5_s)
```

```python
import functools
import math

import numpy as np
import jax
import jax.numpy as jnp
from jax import lax
from jax.experimental import pallas as pl
from jax.experimental.pallas import tpu as pltpu

F32 = jnp.float32
BF16 = jnp.bfloat16

D_MODEL = 1024
DEPTH = 2
HEAD_DIM = 64
MIX_WIDTH = D_MODEL // 4
N_BRANCH = 4
Q_BLOCK = 128
PAGE_ROWS = 128
EPS = 1e-6
NEG = -1e30
FORCE = 1e30

MLA_HEADS = MIX_WIDTH // HEAD_DIM
MLA_Q_LORA = 3 * D_MODEL // 16
MLA_KV_LORA = D_MODEL // 8
MLA_NOPE = HEAD_DIM
MLA_ROPE = HEAD_DIM // 2
MLA_V = HEAD_DIM
ROPE_THETA = 10000.0

S5_GROUP = 16
S5_GROUPS = MIX_WIDTH // S5_GROUP
S5_STATE = 64

FOX_HEADS = MIX_WIDTH // HEAD_DIM

NSA_HEADS = MIX_WIDTH // HEAD_DIM
NSA_CMP_BLOCK = 32
NSA_SEL_BLOCK = 64
NSA_TOPK = 16
NSA_WINDOW = 512

REL_BUCKETS = 32
REL_MAX_DIST = 128

FFN_HIDDEN = (-(-8 * D_MODEL // 3) + 255) // 256 * 256

IN_SPLITS = (MLA_Q_LORA, MLA_KV_LORA, MLA_ROPE,
             MIX_WIDTH,
             MIX_WIDTH, MIX_WIDTH, MIX_WIDTH, FOX_HEADS,
             MIX_WIDTH, HEAD_DIM, HEAD_DIM, HEAD_DIM, HEAD_DIM, HEAD_DIM, HEAD_DIM, 3 * NSA_HEADS,
             N_BRANCH * D_MODEL)
IN_COLS = sum(IN_SPLITS)
GATE_COLS = N_BRANCH * D_MODEL
MIX_COLS = IN_COLS - GATE_COLS
MIX_SPLIT_POINTS = tuple(int(v) for v in np.cumsum(IN_SPLITS[:-1])[:-1])

LANE = 128
VMEM_LIMIT = 48 * 1024 * 1024

Z_GATE = 0
Z_FQ = Z_GATE + GATE_COLS
Z_FK = Z_FQ + MIX_WIDTH
Z_FV = Z_FK + MIX_WIDTH
Z_NQ = Z_FV + MIX_WIDTH
Z_S5U = Z_NQ + MIX_WIDTH
Z_QLAT = Z_S5U + MIX_WIDTH
Z_KVLAT = Z_QLAT + 2 * LANE
Z_KROPE = Z_KVLAT + LANE
Z_KROPE_SW = Z_KROPE + LANE
Z_KCVC = Z_KROPE_SW + LANE
Z_KSVS = Z_KCVC + LANE
Z_KWVW = Z_KSVS + LANE
Z_NG = Z_KWVW + LANE
Z_COLS = Z_NG + LANE
Z_TILE = Z_COLS // 3


def _rms(x, g):
    return x * lax.rsqrt(jnp.mean(x * x, axis=-1, keepdims=True) + EPS) * g


def _ada_kernel(c_ref, w_ref, b_ref, o_ref):
    c = c_ref[...]
    a = (c * jax.nn.sigmoid(c)).astype(BF16)
    o_ref[0] = jnp.dot(a, w_ref[0], preferred_element_type=F32) + b_ref[0]


def _ada(c_all, w_ada, b_ada):
    rows = c_all.shape[0]
    tn = 1536
    return pl.pallas_call(
        _ada_kernel,
        grid=(DEPTH, 6 * D_MODEL // tn),
        in_specs=[pl.BlockSpec((rows, D_MODEL), lambda l, j: (0, 0)),
                  pl.BlockSpec((1, D_MODEL, tn), lambda l, j: (l, 0, j)),
                  pl.BlockSpec((1, 1, tn), lambda l, j: (l, 0, j))],
        out_specs=pl.BlockSpec((1, rows, tn), lambda l, j: (l, 0, j)),
        out_shape=jax.ShapeDtypeStruct((DEPTH, rows, 6 * D_MODEL), F32),
        compiler_params=pltpu.CompilerParams(dimension_semantics=("arbitrary", "arbitrary"),
                                             vmem_limit_bytes=VMEM_LIMIT),
        name="ada_mod",
    )(c_all, w_ada.astype(BF16), b_ada.reshape(DEPTH, 1, 6 * D_MODEL))


def _inproj_kernel(x_ref, gain_ref, sc_ref, sh_ref, w_ref, o_ref, h_ref):
    @pl.when(pl.program_id(1) == 0)
    def _():
        h = _rms(x_ref[...], gain_ref[...]) * (1.0 + sc_ref[0]) + sh_ref[0]
        h_ref[...] = h.astype(BF16)
    o_ref[...] = jnp.dot(h_ref[...], w_ref[...], preferred_element_type=F32)


def _inproj(x, gain, sc, sh, w, tm):
    n = x.shape[0]
    tiles_per_group = n // tm // sc.shape[0]
    tn = Z_TILE
    mod_spec = pl.BlockSpec((1, sc.shape[1], D_MODEL), lambda i, j: (i // tiles_per_group, 0, 0))
    return pl.pallas_call(
        _inproj_kernel,
        grid=(n // tm, Z_COLS // tn),
        in_specs=[pl.BlockSpec((tm, D_MODEL), lambda i, j: (i, 0)),
                  pl.BlockSpec((1, D_MODEL), lambda i, j: (0, 0)),
                  mod_spec, mod_spec,
                  pl.BlockSpec((D_MODEL, tn), lambda i, j: (0, j))],
        out_specs=pl.BlockSpec((tm, tn), lambda i, j: (i, j)),
        out_shape=jax.ShapeDtypeStruct((n, Z_COLS), F32),
        scratch_shapes=[pltpu.VMEM((tm, D_MODEL), BF16)],
        compiler_params=pltpu.CompilerParams(dimension_semantics=("parallel", "arbitrary"),
                                             vmem_limit_bytes=VMEM_LIMIT),
        name="in_proj",
    )(x, gain.reshape(1, D_MODEL), sc, sh, w)


def _merge_kernel(ya_ref, yb_ref, yc_ref, yd_ref, gl_ref, x_ref, wb_ref, wo_ref, gain_ref, g1_ref, o_ref):
    acc = None
    for n, y_ref in enumerate((ya_ref, yb_ref, yc_ref, yd_ref)):
        p = jnp.dot(y_ref[...].astype(BF16), wb_ref[n], preferred_element_type=F32)
        t = jax.nn.sigmoid(gl_ref[:, n * D_MODEL:(n + 1) * D_MODEL]) * p
        acc = t if acc is None else acc + t
    m = jnp.dot(acc.astype(BF16), wo_ref[...], preferred_element_type=F32)
    o_ref[...] = x_ref[...] + g1_ref[0] * _rms(m, gain_ref[...])


def _merge(ys, z, x, wb, wo, gain, g1, tm):
    n = x.shape[0]
    tiles_per_group = n // tm // g1.shape[0]
    y_spec = pl.BlockSpec((tm, MIX_WIDTH), lambda i: (i, 0))
    return pl.pallas_call(
        _merge_kernel,
        grid=(n // tm,),
        in_specs=[y_spec, y_spec, y_spec, y_spec,
                  pl.BlockSpec((tm, GATE_COLS), lambda i: (i, 0)),
                  pl.BlockSpec((tm, D_MODEL), lambda i: (i, 0)),
                  pl.BlockSpec((N_BRANCH, MIX_WIDTH, D_MODEL), lambda i: (0, 0, 0)),
                  pl.BlockSpec((D_MODEL, D_MODEL), lambda i: (0, 0)),
                  pl.BlockSpec((1, D_MODEL), lambda i: (0, 0)),
                  pl.BlockSpec((1, g1.shape[1], D_MODEL), lambda i: (i // tiles_per_group, 0, 0))],
        out_specs=pl.BlockSpec((tm, D_MODEL), lambda i: (i, 0)),
        out_shape=jax.ShapeDtypeStruct((n, D_MODEL), F32),
        compiler_params=pltpu.CompilerParams(dimension_semantics=("parallel",),
                                             vmem_limit_bytes=VMEM_LIMIT),
        name="merge",
    )(*ys, z, x, wb, wo, gain.reshape(1, D_MODEL), g1)


def _ffn_kernel(x_ref, gain_h_ref, sc_ref, sh_ref, wg_ref, wu_ref, wd_ref, gain_o_ref, g2_ref, o_ref,
                h_ref, acc_ref):
    j = pl.program_id(1)

    @pl.when(j == 0)
    def _():
        h = _rms(x_ref[...], gain_h_ref[...]) * (1.0 + sc_ref[0]) + sh_ref[0]
        h_ref[...] = h.astype(BF16)
        acc_ref[...] = jnp.zeros_like(acc_ref)

    h = h_ref[...]
    a = jnp.dot(h, wg_ref[...], preferred_element_type=F32)
    b = jnp.dot(h, wu_ref[...], preferred_element_type=F32)
    t = (a * jax.nn.sigmoid(a)) * b
    acc_ref[...] += jnp.dot(t.astype(BF16), wd_ref[...], preferred_element_type=F32)

    @pl.when(j == pl.num_programs(1) - 1)
    def _():
        o_ref[...] = x_ref[...] + g2_ref[0] * _rms(acc_ref[...], gain_o_ref[...])


def _ffn(x, gain_h, sc, sh, wg, wu, wd, gain_o, g2, tm):
    n = x.shape[0]
    tiles_per_group = n // tm // sc.shape[0]
    th = FFN_HIDDEN // 2
    mod_spec = pl.BlockSpec((1, sc.shape[1], D_MODEL), lambda i, j: (i // tiles_per_group, 0, 0))
    vec_spec = pl.BlockSpec((1, D_MODEL), lambda i, j: (0, 0))
    return pl.pallas_call(
        _ffn_kernel,
        grid=(n // tm, FFN_HIDDEN // th),
        in_specs=[pl.BlockSpec((tm, D_MODEL), lambda i, j: (i, 0)),
                  vec_spec, mod_spec, mod_spec,
                  pl.BlockSpec((D_MODEL, th), lambda i, j: (0, j)),
                  pl.BlockSpec((D_MODEL, th), lambda i, j: (0, j)),
                  pl.BlockSpec((th, D_MODEL), lambda i, j: (j, 0)),
                  vec_spec, mod_spec],
        out_specs=pl.BlockSpec((tm, D_MODEL), lambda i, j: (i, 0)),
        out_shape=jax.ShapeDtypeStruct((n, D_MODEL), F32),
        scratch_shapes=[pltpu.VMEM((tm, D_MODEL), BF16), pltpu.VMEM((tm, D_MODEL), F32)],
        compiler_params=pltpu.CompilerParams(dimension_semantics=("parallel", "arbitrary"),
                                             vmem_limit_bytes=VMEM_LIMIT),
        name="ffn",
    )(x, gain_h.reshape(1, D_MODEL), sc, sh, wg, wu, wd, gain_o.reshape(1, D_MODEL), g2)


TQ = 256
N_HEADS = 4
SCALE_64 = HEAD_DIM ** -0.5


def _dot_nt(a, b):
    return lax.dot_general(a, b, (((1,), (1,)), ((), ())), preferred_element_type=F32)


def _softmax_step(s, v, m_ref, l_ref, acc_ref):
    m_old = m_ref[...]
    m_new = jnp.maximum(m_old, jnp.max(s, axis=-1, keepdims=True))
    alpha = jnp.exp(m_old - m_new)
    p = jnp.exp(s - m_new)
    l_ref[...] = alpha * l_ref[...] + jnp.sum(p, axis=-1, keepdims=True)
    acc_ref[...] = alpha * acc_ref[...] + jnp.dot(p.astype(BF16), v, preferred_element_type=F32)
    m_ref[...] = m_new


def _softmax_reset(m_ref, l_ref, acc_ref):
    m_ref[...] = jnp.full(m_ref.shape, NEG, F32)
    l_ref[...] = jnp.zeros(l_ref.shape, F32)
    acc_ref[...] = jnp.zeros(acc_ref.shape, F32)


def _causal_neg(rows):
    i = lax.broadcasted_iota(jnp.int32, (TQ, TQ), 0)
    j = lax.broadcasted_iota(jnp.int32, (TQ, TQ), 1)
    neg = jnp.where(j <= i, 0.0, NEG).astype(F32)
    return jnp.concatenate([neg] * rows, axis=0)


MLA_QW = 2 * LANE


def _mla_prep_kernel(ql_ref, kv_ref, kr_ref, krs_ref, cq_ref, sq_ref, gq_ref, gkv_ref, w1_ref, wuk_ref,
                     qcat_ref, kf_ref, kb_ref):
    ql = ql_ref[...]
    qn = ql * lax.rsqrt(jnp.sum(ql * ql, axis=-1, keepdims=True) * (1.0 / MLA_Q_LORA) + EPS) * gq_ref[...]
    q1 = jnp.dot(qn.astype(BF16), w1_ref[...], preferred_element_type=F32)
    q_abs = jnp.dot(q1[:, :MIX_WIDTH].astype(BF16), wuk_ref[...], preferred_element_type=F32)
    cc, ss = cq_ref[...], sq_ref[...]
    parts = []
    for h in range(N_HEADS):
        r = q1[:, MIX_WIDTH + h * LANE:MIX_WIDTH + (h + 1) * LANE]
        rs = q1[:, MIX_WIDTH + (N_HEADS + h) * LANE:MIX_WIDTH + (N_HEADS + h + 1) * LANE]
        parts += [q_abs[:, h * LANE:(h + 1) * LANE], r * cc + rs * ss]
    qcat_ref[...] = jnp.concatenate(parts, axis=1).astype(BF16)
    c_new = _rms(kv_ref[...], gkv_ref[...])
    kr = kr_ref[...] * cc + krs_ref[...] * ss
    k = jnp.concatenate([c_new, kr], axis=1)
    kf_ref[...] = k
    kb_ref[...] = k.astype(BF16)


def _mla_prep(z, cq, sq, gq, gkv, w1, wuk, T, tm):
    n = z.shape[0]
    tiles_per_seq = T // tm
    row = lambda c: pl.BlockSpec((tm, LANE), lambda i: (i, c))
    tab = pl.BlockSpec((tm, LANE), lambda i: (i % tiles_per_seq, 0))
    full = lambda a: pl.BlockSpec(a.shape, lambda i: (0,) * a.ndim)
    return pl.pallas_call(
        _mla_prep_kernel,
        grid=(n // tm,),
        in_specs=[pl.BlockSpec((tm, 2 * LANE), lambda i: (i, Z_QLAT // (2 * LANE))),
                  row(Z_KVLAT // LANE), row(Z_KROPE // LANE), row(Z_KROPE_SW // LANE),
                  tab, tab, full(gq), full(gkv), full(w1), full(wuk)],
        out_specs=[pl.BlockSpec((tm, N_HEADS * MLA_QW), lambda i: (i, 0)),
                   pl.BlockSpec((tm, MLA_QW), lambda i: (i, 0)),
                   pl.BlockSpec((tm, MLA_QW), lambda i: (i, 0))],
        out_shape=[jax.ShapeDtypeStruct((n, N_HEADS * MLA_QW), BF16),
                   jax.ShapeDtypeStruct((n, MLA_QW), F32),
                   jax.ShapeDtypeStruct((n, MLA_QW), BF16)],
        compiler_params=pltpu.CompilerParams(dimension_semantics=("parallel",), vmem_limit_bytes=VMEM_LIMIT),
        name="mla_prep",
    )(z, z, z, z, cq, sq, gq, gkv, w1, wuk)


def _mla_attn_kernel(q_ref, k_ref, wuv_ref, o_ref, m_ref, l_ref, acc_ref):
    qi = pl.program_id(1)
    q = q_ref[...]
    qs = jnp.concatenate([q[:, h * MLA_QW:(h + 1) * MLA_QW] for h in range(N_HEADS)], axis=0)
    scale = (MLA_NOPE + MLA_ROPE) ** -0.5
    _softmax_reset(m_ref, l_ref, acc_ref)

    def step(kt, neg):
        k = k_ref[pl.ds(pl.multiple_of(kt * TQ, TQ), TQ), :]
        s = _dot_nt(qs, k) * scale
        if neg is not None:
            s = s + neg
        _softmax_step(s, k[:, :MLA_KV_LORA], m_ref, l_ref, acc_ref)

    lax.fori_loop(0, qi, lambda kt, c: (step(kt, None), c)[1], 0)
    step(qi, _causal_neg(N_HEADS))
    o_lat = (acc_ref[...] / l_ref[...]).astype(BF16)
    out = None
    for h in range(N_HEADS):
        t = jnp.dot(o_lat[h * TQ:(h + 1) * TQ], wuv_ref[h], preferred_element_type=F32)
        out = t if out is None else out + t
    o_ref[...] = out


def _mla_attn(qcat, kb, wuv, B, T):
    return pl.pallas_call(
        _mla_attn_kernel,
        grid=(B, T // TQ),
        in_specs=[pl.BlockSpec((TQ, N_HEADS * MLA_QW), lambda b, i: (b * (T // TQ) + i, 0)),
                  pl.BlockSpec((T, MLA_QW), lambda b, i: (b, 0)),
                  pl.BlockSpec(wuv.shape, lambda b, i: (0, 0, 0))],
        out_specs=pl.BlockSpec((TQ, MIX_WIDTH), lambda b, i: (b * (T // TQ) + i, 0)),
        out_shape=jax.ShapeDtypeStruct((B * T, MIX_WIDTH), F32),
        scratch_shapes=[pltpu.VMEM((N_HEADS * TQ, 1), F32), pltpu.VMEM((N_HEADS * TQ, 1), F32),
                        pltpu.VMEM((N_HEADS * TQ, MLA_KV_LORA), F32)],
        compiler_params=pltpu.CompilerParams(dimension_semantics=("parallel", "arbitrary"),
                                             vmem_limit_bytes=VMEM_LIMIT),
        name="mla_attn",
    )(qcat, kb, wuv)


def _fox_attn_kernel(q_ref, k_ref, v_ref, fq_ref, fk_ref, o_ref, fqb_ref, m_ref, l_ref, acc_ref):
    qi = pl.program_id(1)
    q = q_ref[...] * SCALE_64
    head = lax.broadcasted_iota(jnp.int32, (TQ, MIX_WIDTH), 1) // HEAD_DIM
    qs = jnp.concatenate([jnp.where(head == h, q, 0.0) for h in range(N_HEADS)], axis=0).astype(BF16)
    fq = fq_ref[...]
    for h in range(N_HEADS):
        fqb_ref[h * TQ:(h + 1) * TQ, :] = jnp.broadcast_to(fq[:, h:h + 1], (TQ, TQ))
    _softmax_reset(m_ref, l_ref, acc_ref)

    def step(kt, neg):
        off = pl.multiple_of(kt * TQ, TQ)
        k = k_ref[pl.ds(off, TQ), :].astype(BF16)
        v = v_ref[pl.ds(off, TQ), :].astype(BF16)
        fk = fk_ref[kt]
        fkb = jnp.concatenate([jnp.broadcast_to(fk[h:h + 1, :], (TQ, TQ)) for h in range(N_HEADS)], axis=0)
        s = _dot_nt(qs, k) + (fqb_ref[...] - fkb)
        if neg is not None:
            s = s + neg
        _softmax_step(s, v, m_ref, l_ref, acc_ref)

    lax.fori_loop(0, qi, lambda kt, c: (step(kt, None), c)[1], 0)
    step(qi, _causal_neg(N_HEADS))
    o = acc_ref[...] / l_ref[...]
    out = jnp.zeros((TQ, MIX_WIDTH), F32)
    for h in range(N_HEADS):
        out = jnp.where(head == h, o[h * TQ:(h + 1) * TQ], out)
    o_ref[...] = out


def _fox_attn(z, fq, fk, B, T):
    nq = T // TQ
    col = lambda c: pl.BlockSpec((T, MIX_WIDTH), lambda b, i: (b, c))
    return pl.pallas_call(
        _fox_attn_kernel,
        grid=(B, nq),
        in_specs=[pl.BlockSpec((TQ, MIX_WIDTH), lambda b, i: (b * nq + i, Z_FQ // MIX_WIDTH)),
                  col(Z_FK // MIX_WIDTH), col(Z_FV // MIX_WIDTH),
                  pl.BlockSpec((TQ, LANE), lambda b, i: (b * nq + i, 0)),
                  pl.BlockSpec((None, nq, 8, TQ), lambda b, i: (b, 0, 0, 0))],
        out_specs=pl.BlockSpec((TQ, MIX_WIDTH), lambda b, i: (b * nq + i, 0)),
        out_shape=jax.ShapeDtypeStruct((B * T, MIX_WIDTH), F32),
        scratch_shapes=[pltpu.VMEM((N_HEADS * TQ, TQ), F32),
                        pltpu.VMEM((N_HEADS * TQ, 1), F32), pltpu.VMEM((N_HEADS * TQ, 1), F32),
                        pltpu.VMEM((N_HEADS * TQ, MIX_WIDTH), F32)],
        compiler_params=pltpu.CompilerParams(dimension_semantics=("parallel", "arbitrary"),
                                             vmem_limit_bytes=VMEM_LIMIT),
        name="fox_attn",
    )(z, z, z, fq, fk)


def _nsa_tables(rel_bias, T):
    def tab(dist):
        oh = jax.nn.one_hot(_rel_bucket(dist), REL_BUCKETS, dtype=F32)
        b = jnp.einsum('rcb,bh->hrc', oh, rel_bias.astype(F32), precision=lax.Precision.HIGHEST)
        return b.reshape(N_HEADS * dist.shape[0], dist.shape[1])
    i = jnp.arange(TQ, dtype=jnp.int32)[:, None]
    j = jnp.arange(TQ, dtype=jnp.int32)[None, :]
    tile4 = lambda m: jnp.concatenate([m] * N_HEADS, axis=0)
    far = tab(jnp.full((TQ, TQ), REL_MAX_DIST, jnp.int32))
    prev = tab(TQ + i - j)
    diag = jnp.where(tile4(j <= i), tab(i - j), NEG)
    edge = jnp.where(tile4(j >= i), far, NEG)
    tb = jnp.stack([far, prev, diag, edge])
    n_cb = T // NSA_CMP_BLOCK
    pos = jnp.arange(T, dtype=jnp.int32)[:, None]
    cb_end = jnp.arange(n_cb, dtype=jnp.int32)[None, :] * NSA_CMP_BLOCK + (NSA_CMP_BLOCK - 1)
    bc = tab(pos - cb_end).reshape(N_HEADS, T // TQ, TQ, n_cb)
    bc = jnp.moveaxis(bc, 1, 0).reshape(T // TQ, N_HEADS * TQ, n_cb)
    return tb, bc


def _nsa_attn_kernel(q_ref, kcvc_ref, ksvs_ref, kwvw_ref, ng_ref, wkv_ref, tb_ref, bc_ref, o_ref,
                     cb_ref, expand_ref, negsel_ref, m_ref, l_ref, acc_ref):
    qi = pl.program_id(1)
    T = ksvs_ref.shape[0]
    n_cb = T // NSA_CMP_BLOCK
    n_kt = T // TQ

    @pl.when(qi == 0)
    def _():
        x = kcvc_ref[...].reshape(n_cb, NSA_CMP_BLOCK, LANE) * wkv_ref[...][None]
        cb_ref[...] = jnp.sum(x, axis=1)
        r = lax.broadcasted_iota(jnp.int32, (n_cb, T), 0)
        key = lax.broadcasted_iota(jnp.int32, (n_cb, T), 1)
        expand_ref[...] = jnp.where(r == 2 * (key // NSA_SEL_BLOCK), 1.0, 0.0).astype(BF16)

    lane = lax.broadcasted_iota(jnp.int32, (TQ, LANE), 1)
    lo = lane < HEAD_DIM
    q = q_ref[...] * SCALE_64
    t0, t1 = q[:, :LANE], q[:, LANE:]
    qs = jnp.concatenate([jnp.where(lo, t0, 0.0), jnp.where(lo, pltpu.roll(t0, HEAD_DIM, 1), 0.0),
                          jnp.where(lo, t1, 0.0), jnp.where(lo, pltpu.roll(t1, HEAD_DIM, 1), 0.0)],
                         axis=0).astype(BF16)

    cb = cb_ref[...].astype(BF16)
    qpos = qi * TQ + lax.broadcasted_iota(jnp.int32, (TQ, n_cb), 0)
    cb_end = lax.broadcasted_iota(jnp.int32, (TQ, n_cb), 1) * NSA_CMP_BLOCK + (NSA_CMP_BLOCK - 1)
    valid = jnp.concatenate([cb_end <= qpos] * N_HEADS, axis=0)
    lc = jnp.where(valid, _dot_nt(qs, cb) + bc_ref[0], NEG)
    e = jnp.where(valid, jnp.exp(lc - jnp.max(lc, axis=-1, keepdims=True)), 0.0)
    pc = e / jnp.maximum(jnp.sum(e, axis=-1, keepdims=True), 1e-30)
    o_c = jnp.dot(pc.astype(BF16), cb, preferred_element_type=F32)

    psum = pc[0:TQ] + pc[TQ:2 * TQ] + pc[2 * TQ:3 * TQ] + pc[3 * TQ:4 * TQ]
    imp = (psum + pltpu.roll(psum, n_cb - 1, 1)).T
    row = lax.broadcasted_iota(jnp.int32, (n_cb, TQ), 0)
    qpos_t = qi * TQ + lax.broadcasted_iota(jnp.int32, (n_cb, TQ), 1)
    blk = row // 2
    score = jnp.where(qpos_t // NSA_SEL_BLOCK == blk, FORCE,
                      jnp.where(blk * NSA_SEL_BLOCK <= qpos_t, imp, NEG))
    score = jnp.where(row % 2 == 0, score, 2 * NEG)
    cnt = jnp.zeros((n_cb, TQ), F32)
    for i in range(0, n_cb, 2):
        si = score[i:i + 1, :]
        tie = jnp.where(row > i, 1.0, 0.0)
        cnt = cnt + jnp.where(si > score, 1.0, jnp.where(si == score, tie, 0.0))
    sel = jnp.where(cnt < NSA_TOPK, jnp.where(score > NEG / 2, 1.0, 0.0), 0.0).T
    keymask = jnp.dot(sel.astype(BF16), expand_ref[...], preferred_element_type=F32)
    for kt in range(n_kt):
        negsel_ref[kt] = (keymask[:, kt * TQ:(kt + 1) * TQ] - 1.0) * (-NEG)

    def step(kt, tab, kv_ref, use_sel):
        kv = kv_ref[pl.ds(pl.multiple_of(kt * TQ, TQ), TQ), :].astype(BF16)
        s = _dot_nt(qs, kv) + tb_ref[tab]
        if use_sel:
            ns = negsel_ref[kt]
            s = s + jnp.concatenate([ns] * N_HEADS, axis=0)
        _softmax_step(s, kv, m_ref, l_ref, acc_ref)

    _softmax_reset(m_ref, l_ref, acc_ref)
    lax.fori_loop(0, qi + 1,
                  lambda kt, c: (step(kt, jnp.maximum(kt - qi + 2, 0), ksvs_ref, True), c)[1], 0)
    o_s = acc_ref[...] / l_ref[...]
    _softmax_reset(m_ref, l_ref, acc_ref)
    lax.fori_loop(jnp.maximum(qi - 2, 0), qi + 1,
                  lambda kt, c: (step(kt, jnp.where(kt == qi - 2, 3, kt - qi + 2), kwvw_ref, False), c)[1], 0)
    o_w = acc_ref[...] / l_ref[...]

    g = jax.nn.sigmoid(ng_ref[...])
    mixed = []
    for h in range(N_HEADS):
        rows = slice(h * TQ, (h + 1) * TQ)
        mixed.append(g[:, h:h + 1] * o_c[rows] + g[:, N_HEADS + h:N_HEADS + h + 1] * o_s[rows]
                     + g[:, 2 * N_HEADS + h:2 * N_HEADS + h + 1] * o_w[rows])
    o_ref[...] = jnp.concatenate([jnp.where(lo, pltpu.roll(mixed[0], HEAD_DIM, 1), mixed[1]),
                                  jnp.where(lo, pltpu.roll(mixed[2], HEAD_DIM, 1), mixed[3])], axis=1)


def _nsa_attn(z, wkv, tb, bc, B, T):
    nq = T // TQ
    n_cb = T // NSA_CMP_BLOCK
    kv = lambda c: pl.BlockSpec((T, LANE), lambda b, i: (b, c))
    return pl.pallas_call(
        _nsa_attn_kernel,
        grid=(B, nq),
        in_specs=[pl.BlockSpec((TQ, MIX_WIDTH), lambda b, i: (b * nq + i, Z_NQ // MIX_WIDTH)),
                  kv(Z_KCVC // LANE), kv(Z_KSVS // LANE), kv(Z_KWVW // LANE),
                  pl.BlockSpec((TQ, LANE), lambda b, i: (b * nq + i, Z_NG // LANE)),
                  pl.BlockSpec(wkv.shape, lambda b, i: (0, 0)),
                  pl.BlockSpec(tb.shape, lambda b, i: (0, 0, 0)),
                  pl.BlockSpec((1, N_HEADS * TQ, n_cb), lambda b, i: (i, 0, 0))],
        out_specs=pl.BlockSpec((TQ, MIX_WIDTH), lambda b, i: (b * nq + i, 0)),
        out_shape=jax.ShapeDtypeStruct((B * T, MIX_WIDTH), F32),
        scratch_shapes=[pltpu.VMEM((n_cb, LANE), F32), pltpu.VMEM((n_cb, T), BF16),
                        pltpu.VMEM((nq, TQ, TQ), F32),
                        pltpu.VMEM((N_HEADS * TQ, 1), F32), pltpu.VMEM((N_HEADS * TQ, 1), F32),
                        pltpu.VMEM((N_HEADS * TQ, LANE), F32)],
        compiler_params=pltpu.CompilerParams(dimension_semantics=("parallel", "arbitrary"),
                                             vmem_limit_bytes=VMEM_LIMIT),
        name="nsa_attn",
    )(z, z, z, z, z, wkv, tb, bc)


PAGES_PER_STEP = 16
SUB = 8
HIGHEST = lax.Precision.HIGHEST


def _page_specs(layer, rows, width, col_block=0):
    return [pl.BlockSpec((None, None, rows, width),
                         lambda b, j, pt, k=k: (layer, pt[b, j * PAGES_PER_STEP + k], 0, col_block))
            for k in range(PAGES_PER_STEP)]


def _seq_spec(rows, width):
    return pl.BlockSpec((None, rows, width), lambda b, j, pt: (b, 0, 0))


def _const_spec(a):
    return pl.BlockSpec(a.shape, lambda b, j, pt: (0,) * a.ndim)


def _merge_new_key(s_new, v_new, m_ref, l_ref, acc_ref):
    m_old = m_ref[...]
    m_new = jnp.maximum(m_old, s_new)
    alpha = jnp.exp(m_old - m_new)
    p = jnp.exp(s_new - m_new)
    return (alpha * acc_ref[...] + p * v_new) / (alpha * l_ref[...] + p)


def _bf16_round(x):
    return x.astype(BF16).astype(F32)


def _paged_call(kernel_fn, name, page_table, n_steps, in_specs, out_specs, out_shape, scratch_shapes, args):
    nb = page_table.shape[0]
    return pl.pallas_call(
        kernel_fn,
        grid_spec=pltpu.PrefetchScalarGridSpec(num_scalar_prefetch=1, grid=(nb, n_steps), in_specs=in_specs,
                                               out_specs=out_specs, scratch_shapes=scratch_shapes),
        out_shape=out_shape,
        compiler_params=pltpu.CompilerParams(dimension_semantics=("parallel", "arbitrary"),
                                             vmem_limit_bytes=VMEM_LIMIT),
        name=name,
    )(page_table, *args)


def _mla_dec_kernel(pt_ref, q_ref, knew_ref, *refs):
    pages = refs[:PAGES_PER_STEP]
    o_ref, m_ref, l_ref, acc_ref = refs[PAGES_PER_STEP:]
    j = pl.program_id(1)
    scale = (MLA_NOPE + MLA_ROPE) ** -0.5

    @pl.when(j == 0)
    def _():
        _softmax_reset(m_ref, l_ref, acc_ref)

    qs = q_ref[...]
    k = jnp.concatenate([p[...] for p in pages], axis=0).astype(BF16)
    s = _dot_nt(qs[:, :MLA_KV_LORA + MLA_ROPE], k) * scale
    _softmax_step(s, k[:, :MLA_KV_LORA], m_ref, l_ref, acc_ref)

    @pl.when(j == pl.num_programs(1) - 1)
    def _():
        kn = _bf16_round(knew_ref[...])
        s_new = jnp.sum(qs.astype(F32) * kn, axis=-1, keepdims=True) * scale
        o_ref[...] = _merge_new_key(s_new, kn[:, :MLA_KV_LORA], m_ref, l_ref, acc_ref)


def _mla_dec(layer, page_table, cache_mla, q, knew):
    nb, n_pages = page_table.shape
    return _paged_call(
        _mla_dec_kernel, "mla_decode", page_table, n_pages // PAGES_PER_STEP,
        [_seq_spec(SUB, MLA_QW), _seq_spec(1, MLA_QW)] + _page_specs(layer, cache_mla.shape[2], cache_mla.shape[3]),
        _seq_spec(SUB, MLA_KV_LORA), jax.ShapeDtypeStruct((nb, SUB, MLA_KV_LORA), F32),
        [pltpu.VMEM((SUB, 1), F32), pltpu.VMEM((SUB, 1), F32), pltpu.VMEM((SUB, MLA_KV_LORA), F32)],
        [q, knew] + [cache_mla] * PAGES_PER_STEP)


def _fox_dec_kernel(pt_ref, q_ref, knew_ref, vnew_ref, fnew_ref, dt_ref, *refs):
    P = PAGES_PER_STEP
    k_pages, v_pages, f_pages = refs[:P], refs[P:2 * P], refs[2 * P:3 * P]
    o_ref, fsum_ref, m_ref, l_ref, acc_ref = refs[3 * P:]
    j = pl.program_id(1)

    @pl.when(j == 0)
    def _():
        _softmax_reset(m_ref, l_ref, acc_ref)
        fsum_ref[...] = jnp.zeros(fsum_ref.shape, F32)

    qs = q_ref[...]
    row = lax.broadcasted_iota(jnp.int32, (SUB, FOX_HEADS * PAGE_ROWS), 0)
    lane = lax.broadcasted_iota(jnp.int32, (SUB, FOX_HEADS * PAGE_ROWS), 1)
    mine = lane % FOX_HEADS == row
    carry = fsum_ref[...]
    fks = []
    for p in f_pages:
        x = jnp.where(mine, p[...], 0.0)
        fk = jnp.dot(x, dt_ref[...], preferred_element_type=F32, precision=HIGHEST) + carry
        carry = fk[:, PAGE_ROWS - 1:PAGE_ROWS]
        fks.append(fk)
    fsum_ref[...] = carry
    k = jnp.concatenate([p[...] for p in k_pages], axis=0).astype(BF16)
    v = jnp.concatenate([p[...] for p in v_pages], axis=0).astype(BF16)
    s = _dot_nt(qs, k) - jnp.concatenate(fks, axis=1)
    _softmax_step(s, v, m_ref, l_ref, acc_ref)

    @pl.when(j == pl.num_programs(1) - 1)
    def _():
        m_ref[...] = m_ref[...] + (carry + fnew_ref[...][:, 0:1])
        s_new = jnp.sum(qs.astype(F32) * _bf16_round(knew_ref[...]), axis=-1, keepdims=True)
        o = _merge_new_key(s_new, _bf16_round(vnew_ref[...]), m_ref, l_ref, acc_ref)
        head = lax.broadcasted_iota(jnp.int32, (SUB, MIX_WIDTH), 1) // HEAD_DIM
        hrow = lax.broadcasted_iota(jnp.int32, (SUB, MIX_WIDTH), 0)
        o_ref[...] = jnp.sum(jnp.where(head == hrow, o, 0.0), axis=0, keepdims=True)


def _fox_dec(layer, page_table, cache_k, cache_v, cache_f, q, knew, vnew, fnew, dt):
    nb, n_pages = page_table.shape
    return _paged_call(
        _fox_dec_kernel, "fox_decode", page_table, n_pages // PAGES_PER_STEP,
        [_seq_spec(SUB, MIX_WIDTH), _seq_spec(1, MIX_WIDTH), _seq_spec(1, MIX_WIDTH), _seq_spec(SUB, LANE),
         _const_spec(dt)]
        + _page_specs(layer, PAGE_ROWS, MIX_WIDTH) + _page_specs(layer, PAGE_ROWS, MIX_WIDTH)
        + _page_specs(layer, 1, FOX_HEADS * PAGE_ROWS),
        _seq_spec(1, MIX_WIDTH), jax.ShapeDtypeStruct((nb, 1, MIX_WIDTH), F32),
        [pltpu.VMEM((SUB, 1), F32), pltpu.VMEM((SUB, 1), F32), pltpu.VMEM((SUB, 1), F32),
         pltpu.VMEM((SUB, MIX_WIDTH), F32)],
        [q, knew, vnew, fnew, dt] + [cache_k] * PAGES_PER_STEP + [cache_v] * PAGES_PER_STEP
        + [cache_f] * PAGES_PER_STEP)


def _nsa_cmp_dec_kernel(pt_ref, q_ref, wt_ref, bias_ref, pair_ref, *refs):
    P = PAGES_PER_STEP
    pages = refs[:P]
    oc_ref, idx_ref, cb_ref = refs[P:]
    j = pl.program_id(1)
    per_page = PAGE_ROWS // NSA_CMP_BLOCK
    blocks = []
    for p in pages:
        x = (p[...] * wt_ref[...]).reshape(per_page, NSA_CMP_BLOCK, LANE)
        blocks.append(jnp.sum(x, axis=1))
    rows = P * per_page
    cb_ref[pl.ds(pl.multiple_of(j * rows, rows), rows), :] = jnp.concatenate(blocks, axis=0)

    @pl.when(j == pl.num_programs(1) - 1)
    def _():
        qs = q_ref[...]
        cb = cb_ref[...].astype(BF16)
        lc = _dot_nt(qs, cb) + bias_ref[...]
        e = jnp.exp(lc - jnp.max(lc, axis=-1, keepdims=True))
        pc = e / jnp.maximum(jnp.sum(e, axis=-1, keepdims=True), 1e-30)
        oc_ref[...] = jnp.dot(pc.astype(BF16), cb, preferred_element_type=F32)
        psum = jnp.sum(pc[0:N_HEADS], axis=0, keepdims=True)
        imp = jnp.dot(jnp.broadcast_to(psum, (SUB, psum.shape[1])), pair_ref[...],
                      preferred_element_type=F32, precision=HIGHEST)
        n_sel = imp.shape[1]
        col = jnp.broadcast_to(imp.T[:, 0:1], (n_sel, n_sel))
        rowv = jnp.broadcast_to(imp[0:1, :], (n_sel, n_sel))
        i_idx = lax.broadcasted_iota(jnp.int32, (n_sel, n_sel), 0)
        j_idx = lax.broadcasted_iota(jnp.int32, (n_sel, n_sel), 1)
        ahead = jnp.where(col > rowv, 1.0, jnp.where(col == rowv, jnp.where(i_idx < j_idx, 1.0, 0.0), 0.0))
        rank = jnp.sum(ahead, axis=0, keepdims=True)
        r = lax.broadcasted_iota(jnp.int32, (2 * SUB, n_sel), 0).astype(F32)
        blk = lax.broadcasted_iota(jnp.int32, (2 * SUB, n_sel), 1).astype(F32)
        pick = jnp.sum(jnp.where(rank == r, blk, 0.0), axis=-1, keepdims=True)
        idx_ref[...] = jnp.broadcast_to(pick, (2 * SUB, LANE)).astype(jnp.int32)


def _nsa_cmp_dec(layer, page_table, cache_nsa, q, wt, bias, pair):
    nb, n_pages = page_table.shape
    n_cb = n_pages * PAGE_ROWS // NSA_CMP_BLOCK
    return _paged_call(
        _nsa_cmp_dec_kernel, "nsa_cmp_decode", page_table, n_pages // PAGES_PER_STEP,
        [_seq_spec(SUB, LANE), _const_spec(wt), _const_spec(bias), _const_spec(pair)]
        + _page_specs(layer, PAGE_ROWS, LANE, 0),
        [_seq_spec(SUB, LANE), _seq_spec(2 * SUB, LANE)],
        [jax.ShapeDtypeStruct((nb, SUB, LANE), F32), jax.ShapeDtypeStruct((nb, 2 * SUB, LANE), jnp.int32)],
        [pltpu.VMEM((n_cb, LANE), F32)],
        [q, wt, bias, pair] + [cache_nsa] * PAGES_PER_STEP)


N_SEL_PAST = NSA_TOPK - 1


def _nsa_sel_dec_kernel(sel_ref, id_ref, q_ref, oc_ref, snew_ref, wnew_ref, gate_ref, win_ref, bsel_ref, bwin_ref,
                        b0_ref, *refs):
    blocks = refs[:N_SEL_PAST]
    o_ref = refs[N_SEL_PAST]
    b = pl.program_id(0)
    qs = q_ref[...]
    qf = qs.astype(F32)
    b0 = b0_ref[...]

    def attend(keys, s_list, new_ref):
        new = _bf16_round(new_ref[...])
        s_new = jnp.sum(qf * new, axis=-1, keepdims=True) + b0[:, 0:1]
        m = s_new
        for s in s_list:
            m = jnp.maximum(m, jnp.max(s, axis=-1, keepdims=True))
        p_new = jnp.exp(s_new - m)
        l, acc = p_new, p_new * new
        for s, kv in zip(s_list, keys):
            p = jnp.exp(s - m)
            l = l + jnp.sum(p, axis=-1, keepdims=True)
            acc = acc + jnp.dot(p.astype(BF16), kv, preferred_element_type=F32)
        return acc / l

    sel_kv = [r[...].astype(BF16) for r in blocks]
    sel_s = [_dot_nt(qs, kv) + bsel_ref[id_ref[b, r]] for r, kv in enumerate(sel_kv)]
    o_s = attend(sel_kv, sel_s, snew_ref)
    win = win_ref[...].astype(BF16)
    o_w = attend([win], [_dot_nt(qs, win) + bwin_ref[...]], wnew_ref)
    g = gate_ref[...]
    mixed = g[0] * oc_ref[...] + g[1] * o_s + g[2] * o_w
    lo = lax.broadcasted_iota(jnp.int32, (1, LANE), 1) < HEAD_DIM
    o_ref[...] = jnp.concatenate(
        [jnp.where(lo, pltpu.roll(mixed[0:1], HEAD_DIM, 1), mixed[1:2]),
         jnp.where(lo, pltpu.roll(mixed[2:3], HEAD_DIM, 1), mixed[3:4])], axis=1)


def _nsa_sel_dec(layer, sel_half, sel_id, cache_nsa_half, q, oc, snew, wnew, gates, win, bsel, bwin, b0):
    nb = q.shape[0]
    seq = lambda rows, width: pl.BlockSpec((None, rows, width), lambda b, sh, si: (b, 0, 0))
    const = lambda a: pl.BlockSpec(a.shape, lambda b, sh, si: (0,) * a.ndim)
    blk_specs = [pl.BlockSpec((None, None, NSA_SEL_BLOCK, LANE), lambda b, sh, si, r=r: (layer, sh[b, r], 0, 1))
                 for r in range(N_SEL_PAST)]
    return pl.pallas_call(
        _nsa_sel_dec_kernel,
        grid_spec=pltpu.PrefetchScalarGridSpec(
            num_scalar_prefetch=2, grid=(nb,),
            in_specs=[seq(SUB, LANE), seq(SUB, LANE), seq(1, LANE), seq(1, LANE),
                      pl.BlockSpec((None, 3, SUB, LANE), lambda b, sh, si: (b, 0, 0, 0)),
                      pl.BlockSpec((None, None, win.shape[2], LANE), lambda b, sh, si: (layer, b, 0, 0)),
                      const(bsel), const(bwin), const(b0)] + blk_specs,
            out_specs=seq(1, MIX_WIDTH)),
        out_shape=jax.ShapeDtypeStruct((nb, 1, MIX_WIDTH), F32),
        compiler_params=pltpu.CompilerParams(dimension_semantics=("parallel",), vmem_limit_bytes=VMEM_LIMIT),
        name="nsa_sel_decode",
    )(sel_half, sel_id, q, oc, snew, wnew, gates, win, bsel, bwin, b0, *([cache_nsa_half] * N_SEL_PAST))


def _rowmm_kernel(x_ref, w_ref, o_ref):
    o_ref[...] = jnp.dot(x_ref[...].astype(BF16), w_ref[...], preferred_element_type=F32)


def _rowmm(x, w):
    return pl.pallas_call(_rowmm_kernel, out_shape=jax.ShapeDtypeStruct((x.shape[0], w.shape[1]), F32),
                          name="row_matmul")(x, w)


def _rms_norm(x, g):
    xf = x.astype(F32)
    y = xf * lax.rsqrt(jnp.mean(xf * xf, axis=-1, keepdims=True) + EPS)
    return (y * g.astype(F32)).astype(x.dtype)


def _rope(x, pos):
    half = x.shape[-1] // 2
    freq = ROPE_THETA ** (-jnp.arange(half, dtype=F32) / half)
    ang = pos.astype(F32)[:, None] * freq[None, :]
    ang = ang.reshape(ang.shape[0], *([1] * (x.ndim - 3)), half)
    cos, sin = jnp.cos(ang), jnp.sin(ang)
    x1, x2 = x[..., :half], x[..., half:]
    return jnp.concatenate([x1 * cos - x2 * sin, x1 * sin + x2 * cos], axis=-1)


def _rel_bucket(dist):
    n = jnp.maximum(dist, 0)
    exact = REL_BUCKETS // 2
    large = exact + (jnp.log(jnp.maximum(n, 1).astype(F32) / exact)
                     / math.log(REL_MAX_DIST / exact) * (REL_BUCKETS - exact)).astype(jnp.int32)
    large = jnp.minimum(large, REL_BUCKETS - 1)
    return jnp.where(n < exact, n, large)


def _t5_bias(rel_bias, dist):
    b = rel_bias[_rel_bucket(dist)].astype(F32)
    return jnp.moveaxis(b, -1, -3)


def _masked_softmax(logits, mask):
    lg = jnp.where(mask, logits, NEG)
    m = jnp.max(lg, axis=-1, keepdims=True)
    e = jnp.where(mask, jnp.exp(lg - m), 0.0)
    return e / jnp.maximum(jnp.sum(e, axis=-1, keepdims=True), 1e-30)


def _attend(logits, values, spec):
    lg = logits[0] if len(logits) == 1 else jnp.concatenate(logits, axis=-1)
    p = jax.nn.softmax(lg, axis=-1)
    out, start = None, 0
    for l_, v_ in zip(logits, values):
        n = l_.shape[-1]
        o = jnp.einsum(spec, p[..., start:start + n], v_)
        out = o if out is None else out + o
        start += n
    return out


def _map_query_blocks(fn, arrays, qpos):
    T = qpos.shape[0]
    qb = Q_BLOCK if T % Q_BLOCK == 0 else T
    nb = T // qb

    def split(a):
        return jnp.moveaxis(a.reshape(a.shape[0], nb, qb, *a.shape[2:]), 1, 0)
    xs = tuple(split(a) for a in arrays) + (qpos.reshape(nb, qb),)
    out = lax.map(lambda args: fn(*args), xs)
    out = jnp.moveaxis(out, 0, 1)
    return out.reshape(out.shape[0], T, *out.shape[3:])


def _complex_affine_combine(e1, e2):
    a1r, a1i, b1r, b1i = e1
    a2r, a2i, b2r, b2i = e2
    return (a2r * a1r - a2i * a1i, a2r * a1i + a2i * a1r,
            a2r * b1r - a2i * b1i + b2r, a2r * b1i + a2i * b1r + b2i)


def _gather_past(l, cache_mla, cache_fox_k, cache_fox_v, cache_fox_logf, cache_nsa,
                 state_nsa_win, state_s5, page_table):
    nb, n_pages = page_table.shape

    def pages(cache):
        g = cache[l, page_table]
        return g.reshape(nb, n_pages * g.shape[2], *g.shape[3:])
    return {'mla': pages(cache_mla), 'fox_k': pages(cache_fox_k), 'fox_v': pages(cache_fox_v),
            'fox_logf': pages(cache_fox_logf), 'nsa': pages(cache_nsa),
            'win': state_nsa_win[l], 's5': state_s5[l]}


def _mla(q_lat, kv_lat, k_rope, pos, past, l, w):
    B, T, _ = q_lat.shape
    qn = _rms_norm(q_lat, w['mla_g_q'][l])
    q = jnp.einsum('btr,rhe->bthe', qn, w['mla_w_uq'][l])
    q_nope = q[..., :MLA_NOPE]
    q_rope = _rope(q[..., MLA_NOPE:], pos)
    c_new = _rms_norm(kv_lat, w['mla_g_kv'][l])
    kr_new = _rope(k_rope, pos)
    q_abs = jnp.einsum('bthn,lhn->bthl', q_nope, w['mla_w_uk'][l])
    segs = [(c_new, kr_new, pos)]
    if past is not None:
        n_past = past['mla'].shape[1]
        segs = [(past['mla'][..., :MLA_KV_LORA], past['mla'][..., MLA_KV_LORA:],
                 jnp.arange(n_past, dtype=jnp.int32))] + segs
    scale = (MLA_NOPE + MLA_ROPE) ** -0.5

    def block(qa, qr, qp):
        logits = []
        for c_, kr_, kp in segs:
            s = (jnp.einsum('bqhl,bkl->bhqk', qa, c_, preferred_element_type=F32)
                 + jnp.einsum('bqhr,bkr->bhqk', qr, kr_, preferred_element_type=F32)) * scale
            logits.append(jnp.where(kp[None, :] <= qp[:, None], s, NEG))
        return _attend(logits, [sg[0] for sg in segs], 'bhqk,bkl->bqhl')
    o_lat = _map_query_blocks(block, (q_abs, q_rope), pos)
    o = jnp.einsum('bthl,lhv->bthv', o_lat, w['mla_w_uv'][l]).reshape(B, T, MLA_HEADS * MLA_V)
    return o, jnp.concatenate([c_new, kr_new], axis=-1)


def _s5(u, past, l, w):
    B, T, _ = u.shape
    lam_re = w['s5_lambda_re'][l]
    lam_im = w['s5_lambda_im'][l]
    dt = jnp.exp(w['s5_log_dt'][l])[:, None]
    mag = jnp.exp(lam_re * dt)
    a_re, a_im = mag * jnp.cos(lam_im * dt), mag * jnp.sin(lam_im * dt)
    den = lam_re * lam_re + lam_im * lam_im
    coef_re = ((a_re - 1.0) * lam_re + a_im * lam_im) / den
    coef_im = (a_im * lam_re - (a_re - 1.0) * lam_im) / den
    b_re, b_im = w['s5_b_re'][l], w['s5_b_im'][l]
    bb_re = coef_re[..., None] * b_re - coef_im[..., None] * b_im
    bb_im = coef_re[..., None] * b_im + coef_im[..., None] * b_re
    ug = u.reshape(B, T, S5_GROUPS, S5_GROUP)
    bu_re = jnp.einsum('btgi,gpi->btgp', ug, bb_re)
    bu_im = jnp.einsum('btgi,gpi->btgp', ug, bb_im)
    if past is not None:
        x0_re, x0_im = past['s5'][:, 0], past['s5'][:, 1]
        bu_re = bu_re.at[:, 0].add(a_re * x0_re - a_im * x0_im)
        bu_im = bu_im.at[:, 0].add(a_re * x0_im + a_im * x0_re)
    A_re = jnp.broadcast_to(a_re, bu_re.shape)
    A_im = jnp.broadcast_to(a_im, bu_im.shape)
    _, _, x_re, x_im = lax.associative_scan(_complex_affine_combine, (A_re, A_im, bu_re, bu_im), axis=1)
    y = (jnp.einsum('btgp,gip->btgi', x_re, w['s5_c_re'][l])
         - jnp.einsum('btgp,gip->btgi', x_im, w['s5_c_im'][l]))
    y = y.reshape(B, T, MIX_WIDTH) + w['s5_d'][l] * u
    y = jax.nn.gelu(y)
    y = y * jax.nn.sigmoid(jnp.einsum('btw,wv->btv', y, w['s5_w_glu'][l]))
    state = jnp.stack([x_re[:, -1], x_im[:, -1]], axis=1)
    return y, state


def _fox(q, k, v, f_logit, pos, past, l, w):
    B, T, _ = q.shape
    q = q.reshape(B, T, FOX_HEADS, HEAD_DIM)
    k = k.reshape(B, T, FOX_HEADS, HEAD_DIM)
    v = v.reshape(B, T, FOX_HEADS, HEAD_DIM)
    logf = jax.nn.log_sigmoid(f_logit + w['fox_b_f'][l])
    if past is None:
        F_new = jnp.cumsum(logf, axis=1)
        segs = [(k, v, F_new, pos)]
    else:
        F_past = jnp.cumsum(past['fox_logf'], axis=1)
        F_new = F_past[:, -1:] + jnp.cumsum(logf, axis=1)
        segs = [(past['fox_k'], past['fox_v'], F_past, jnp.arange(F_past.shape[1], dtype=jnp.int32)),
                (k, v, F_new, pos)]
    scale = HEAD_DIM ** -0.5

    def block(qq, fq, qp):
        fq_t = jnp.swapaxes(fq, 1, 2)[..., None]
        logits = []
        for k_, _, fk, kp in segs:
            s = (jnp.einsum('bqhd,bkhd->bhqk', qq, k_, preferred_element_type=F32) * scale
                 + (fq_t - jnp.swapaxes(fk, 1, 2)[:, :, None, :]))
            logits.append(jnp.where(kp[None, :] <= qp[:, None], s, NEG))
        return _attend(logits, [sg[1] for sg in segs], 'bhqk,bkhd->bqhd')
    o = _map_query_blocks(block, (q, F_new), pos)
    return o.reshape(B, T, MIX_WIDTH), k, v, logf


def _nsa(q, kc, vc, ks, vs, kw, vw, g_logit, pos, past, l, w):
    B, T, _ = q.shape
    q = q.reshape(B, T, NSA_HEADS, HEAD_DIM)
    g = jax.nn.sigmoid(g_logit).reshape(B, T, 3, NSA_HEADS)
    rows = jnp.stack([kc, vc, ks, vs], axis=2)
    win_rows = jnp.stack([kw, vw], axis=2)
    if past is None:
        kc_all, vc_all, ks_all, vs_all = kc, vc, ks, vs
        win_ctx = win_rows
        keep = min(NSA_WINDOW, T)
    else:
        pr = past['nsa']
        kc_all = jnp.concatenate([pr[:, :, 0], kc], axis=1)
        vc_all = jnp.concatenate([pr[:, :, 1], vc], axis=1)
        ks_all = jnp.concatenate([pr[:, :, 2], ks], axis=1)
        vs_all = jnp.concatenate([pr[:, :, 3], vs], axis=1)
        win_ctx = jnp.concatenate([past['win'], win_rows], axis=1)
        keep = past['win'].shape[1]
    n_keys = kc_all.shape[1]
    win_base = n_keys - win_ctx.shape[1]
    n_cb = -(-n_keys // NSA_CMP_BLOCK)
    n_sb = -(-n_keys // NSA_SEL_BLOCK)
    ratio = NSA_SEL_BLOCK // NSA_CMP_BLOCK
    k_sel = min(NSA_TOPK, n_sb)

    def pad_rows(a, n):
        return jnp.pad(a, ((0, 0), (0, n - a.shape[1]), (0, 0)))
    kcb = jnp.einsum('bnid,i->bnd', pad_rows(kc_all, n_cb * NSA_CMP_BLOCK).reshape(B, n_cb, NSA_CMP_BLOCK, HEAD_DIM),
                     w['nsa_w_cmp_k'][l])
    vcb = jnp.einsum('bnid,i->bnd', pad_rows(vc_all, n_cb * NSA_CMP_BLOCK).reshape(B, n_cb, NSA_CMP_BLOCK, HEAD_DIM),
                     w['nsa_w_cmp_v'][l])
    cb_end = jnp.arange(n_cb, dtype=jnp.int32) * NSA_CMP_BLOCK + (NSA_CMP_BLOCK - 1)
    sb_start = jnp.arange(n_sb, dtype=jnp.int32) * NSA_SEL_BLOCK
    ks_pad = pad_rows(ks_all, n_sb * NSA_SEL_BLOCK)
    vs_pad = pad_rows(vs_all, n_sb * NSA_SEL_BLOCK)
    win_pad = jnp.pad(win_ctx, ((0, 0), (NSA_WINDOW, 0), (0, 0), (0, 0)))
    rel_bias = w['rel_bias']
    scale = HEAD_DIM ** -0.5

    def block(qq, gb, qp):
        nq = qp.shape[0]
        lc = (jnp.einsum('bqhd,bnd->bhqn', qq, kcb, preferred_element_type=F32) * scale
              + _t5_bias(rel_bias, qp[:, None] - cb_end[None, :]))
        pc = _masked_softmax(lc, cb_end[None, :] <= qp[:, None])
        o_c = jnp.einsum('bhqn,bnd->bqhd', pc, vcb)
        imp = jnp.pad(pc.sum(axis=1), ((0, 0), (0, 0), (0, n_sb * ratio - n_cb)))
        imp = imp.reshape(B, nq, n_sb, ratio).sum(-1)
        cur = (qp[:, None] // NSA_SEL_BLOCK) == jnp.arange(n_sb, dtype=jnp.int32)[None, :]
        score = jnp.where(cur, FORCE, jnp.where(sb_start[None, :] <= qp[:, None], imp, NEG))
        top_v, top_i = lax.top_k(score, k_sel)
        idx = (top_i[..., None] * NSA_SEL_BLOCK + jnp.arange(NSA_SEL_BLOCK, dtype=jnp.int32)).reshape(
            B, nq, k_sel * NSA_SEL_BLOCK)
        valid = jnp.repeat(top_v > NEG / 2, NSA_SEL_BLOCK, axis=-1) & (idx <= qp[None, :, None])
        ks_g = jax.vmap(lambda a, i: a[i])(ks_pad, idx)
        vs_g = jax.vmap(lambda a, i: a[i])(vs_pad, idx)
        ls = (jnp.einsum('bqhd,bqkd->bhqk', qq, ks_g, preferred_element_type=F32) * scale
              + _t5_bias(rel_bias, qp[None, :, None] - idx))
        ps = _masked_softmax(ls, valid[:, None])
        o_s = jnp.einsum('bhqk,bqkd->bqhd', ps, vs_g)
        start = qp[0] - win_base
        wk = lax.dynamic_slice_in_dim(win_pad, start, NSA_WINDOW + nq, axis=1)
        wp = qp[0] - NSA_WINDOW + jnp.arange(NSA_WINDOW + nq, dtype=jnp.int32)
        mask_w = ((wp[None, :] >= 0) & (wp[None, :] <= qp[:, None])
                  & (qp[:, None] - wp[None, :] <= NSA_WINDOW))
        lw = (jnp.einsum('bqhd,bkd->bhqk', qq, wk[:, :, 0], preferred_element_type=F32) * scale
              + _t5_bias(rel_bias, qp[:, None] - wp[None, :]))
        pw = jax.nn.softmax(jnp.where(mask_w, lw, NEG), axis=-1)
        o_w = jnp.einsum('bhqk,bkd->bqhd', pw, wk[:, :, 1])
        return (gb[:, :, 0, :, None] * o_c + gb[:, :, 1, :, None] * o_s
                + gb[:, :, 2, :, None] * o_w)
    o = _map_query_blocks(block, (q, g), pos)
    return o.reshape(B, T, MIX_WIDTH), rows, win_ctx[:, win_ctx.shape[1] - keep:]


def _zcol(z, B, T, start, width):
    return z[:, start:start + width].reshape(B, T, width)


def _mixers_prompt(z, B, T, l, w, wc):
    col = functools.partial(_zcol, z, B, T)
    qcat, kf, kb = _mla_prep(z, wc['rope_c'], wc['rope_s'], wc['mla_gq'][l], wc['mla_gkv'][l],
                             wc['mla_w1'][l], wc['mla_wuk'][l], T, 512)
    y_a = _mla_attn(qcat, kb, wc['mla_wuv'][l], B, T)
    mla_rows = kf[:, :MLA_KV_LORA + MLA_ROPE].reshape(B, T, MLA_KV_LORA + MLA_ROPE)

    logf = jax.nn.log_sigmoid(col(Z_NG + 3 * NSA_HEADS, FOX_HEADS) + w['fox_b_f'][l])
    fsum = jnp.cumsum(logf, axis=1)
    fq = jnp.pad(fsum.reshape(B * T, FOX_HEADS), ((0, 0), (0, LANE - FOX_HEADS)))
    fk = jnp.pad(jnp.swapaxes(fsum.reshape(B, T // TQ, TQ, FOX_HEADS), 2, 3),
                 ((0, 0), (0, 0), (0, 8 - FOX_HEADS), (0, 0)))
    y_c = _fox_attn(z, fq, fk, B, T)
    fox_k = col(Z_FK, MIX_WIDTH).reshape(B, T, FOX_HEADS, HEAD_DIM)
    fox_v = col(Z_FV, MIX_WIDTH).reshape(B, T, FOX_HEADS, HEAD_DIM)

    y_d = _nsa_attn(z, wc['nsa_wkv'][l], wc['nsa_tb'], wc['nsa_bc'], B, T)
    nsa_rows = col(Z_KCVC, 2 * LANE).reshape(B, T, 4, HEAD_DIM)
    keep = min(NSA_WINDOW, T)
    win_state = col(Z_KWVW, LANE)[:, T - keep:].reshape(B, keep, 2, HEAD_DIM)

    y_b, s5_state = _s5(col(Z_S5U, MIX_WIDTH), None, l, w)
    return y_a, y_b, y_c, y_d, (mla_rows, fox_k, fox_v, logf, nsa_rows, win_state, s5_state)


def _head_rows(a):
    return jnp.pad(a, ((0, 0), (0, SUB - a.shape[1])) + ((0, 0),) * (a.ndim - 2))


def _mixers_decode(z, l, w, wc, caches):
    (cache_mla, cache_fox_k, cache_fox_v, cache_fox_logf, cache_nsa, state_nsa_win, state_s5, page_table) = caches
    nb, n_pages = page_table.shape
    pool = cache_mla.shape[1]
    assert cache_mla.shape[2] == PAGE_ROWS and n_pages % PAGES_PER_STEP == 0
    dec = wc['dec']
    col = lambda start, width: z[:, start:start + width]

    qcat, kf, _ = _mla_prep(z, dec['rope_c'], dec['rope_s'], wc['mla_gq'][l], wc['mla_gkv'][l],
                            wc['mla_w1'][l], wc['mla_wuk'][l], nb, nb)
    o_lat = _mla_dec(l, page_table, cache_mla, _head_rows(qcat.reshape(nb, N_HEADS, MLA_QW)),
                     kf.reshape(nb, 1, MLA_QW))
    y_a = _rowmm(o_lat[:, :N_HEADS].reshape(nb, N_HEADS * MLA_KV_LORA),
                 wc['mla_wuv'][l].reshape(N_HEADS * MLA_KV_LORA, MIX_WIDTH))
    mla_rows = kf[:, :MLA_KV_LORA + MLA_ROPE].reshape(nb, 1, MLA_KV_LORA + MLA_ROPE)

    head_of_lane = jnp.arange(MIX_WIDTH, dtype=jnp.int32) // HEAD_DIM
    fq = col(Z_FQ, MIX_WIDTH) * SCALE_64
    fq = jnp.where(head_of_lane[None, None, :] == jnp.arange(N_HEADS, dtype=jnp.int32)[None, :, None],
                   fq[:, None, :], 0.0)
    logf = jax.nn.log_sigmoid(col(Z_NG + 3 * NSA_HEADS, FOX_HEADS) + w['fox_b_f'][l])
    fnew = jnp.broadcast_to(_head_rows(logf[:, :, None]), (nb, SUB, LANE))
    y_c = _fox_dec(l, page_table, cache_fox_k.reshape(DEPTH, pool, PAGE_ROWS, MIX_WIDTH),
                   cache_fox_v.reshape(DEPTH, pool, PAGE_ROWS, MIX_WIDTH),
                   cache_fox_logf.reshape(DEPTH, pool, 1, PAGE_ROWS * FOX_HEADS),
                   _head_rows(fq).astype(BF16), col(Z_FK, MIX_WIDTH).reshape(nb, 1, MIX_WIDTH),
                   col(Z_FV, MIX_WIDTH).reshape(nb, 1, MIX_WIDTH), fnew, dec['fox_dt'])
    fox_k = col(Z_FK, MIX_WIDTH).reshape(nb, 1, FOX_HEADS, HEAD_DIM)
    fox_v = col(Z_FV, MIX_WIDTH).reshape(nb, 1, FOX_HEADS, HEAD_DIM)

    nq = (col(Z_NQ, MIX_WIDTH) * SCALE_64).reshape(nb, N_HEADS, HEAD_DIM)
    nq = _head_rows(jnp.pad(nq, ((0, 0), (0, 0), (0, LANE - HEAD_DIM)))).astype(BF16)
    o_c, idx = _nsa_cmp_dec(l, page_table, cache_nsa.reshape(DEPTH, pool, PAGE_ROWS, 2 * LANE), nq,
                            jnp.tile(wc['nsa_wkv'][l], (PAGE_ROWS // NSA_CMP_BLOCK, 1)), dec['nsa_bc'], dec['nsa_pair'])
    sel_id = idx[:, :N_SEL_PAST, 0]
    halves = PAGE_ROWS // NSA_SEL_BLOCK
    sel_half = jnp.take_along_axis(page_table, sel_id // halves, axis=1) * halves + sel_id % halves
    gates = jax.nn.sigmoid(col(Z_NG, 3 * NSA_HEADS)).reshape(nb, 3, NSA_HEADS, 1)
    gates = jnp.broadcast_to(jnp.pad(gates, ((0, 0), (0, 0), (0, SUB - NSA_HEADS), (0, 0))), (nb, 3, SUB, LANE))
    wnew = col(Z_KWVW, LANE)
    win_past = state_nsa_win.reshape(DEPTH, nb, state_nsa_win.shape[2], LANE)
    y_d = _nsa_sel_dec(l, sel_half, sel_id, cache_nsa.reshape(DEPTH, pool * halves, NSA_SEL_BLOCK, 2 * LANE), nq,
                       o_c, col(Z_KSVS, LANE).reshape(nb, 1, LANE), wnew.reshape(nb, 1, LANE), gates, win_past,
                       dec['nsa_bsel'], dec['nsa_bwin'], dec['nsa_b0'])
    nsa_rows = col(Z_KCVC, 2 * LANE).reshape(nb, 1, 4, HEAD_DIM)
    win_state = jnp.concatenate([state_nsa_win[l][:, 1:], wnew.reshape(nb, 1, 2, HEAD_DIM)], axis=1)

    y_b, s5_state = _s5(col(Z_S5U, MIX_WIDTH).reshape(nb, 1, MIX_WIDTH), {'s5': state_s5[l]}, l, w)
    return y_a, y_b, y_c, y_d, (mla_rows, fox_k, fox_v, logf.reshape(nb, 1, FOX_HEADS), nsa_rows, win_state,
                                s5_state)


def _decode_tables(rel_bias, past_len, nb):
    half = MLA_ROPE // 2
    freq = ROPE_THETA ** (-jnp.arange(half, dtype=F32) / half)
    ang = jnp.full((nb, 1), past_len, jnp.int32).astype(F32) * freq[None, :]
    cos, sin = jnp.cos(ang), jnp.sin(ang)
    lane_pad = ((0, 0), (0, LANE - MLA_ROPE))

    def tab(dist):
        oh = jax.nn.one_hot(_rel_bucket(dist), REL_BUCKETS, dtype=F32)
        return _head_rows(jnp.einsum('rcb,bh->hrc', oh, rel_bias.astype(F32), precision=HIGHEST)[None])[0]
    n_cb = past_len // NSA_CMP_BLOCK
    n_sb = past_len // NSA_SEL_BLOCK
    cb_end = jnp.arange(n_cb, dtype=jnp.int32) * NSA_CMP_BLOCK + (NSA_CMP_BLOCK - 1)
    key_pos = (jnp.arange(n_sb, dtype=jnp.int32)[:, None] * NSA_SEL_BLOCK
               + jnp.arange(NSA_SEL_BLOCK, dtype=jnp.int32)[None, :])
    win_pos = past_len - NSA_WINDOW + jnp.arange(NSA_WINDOW, dtype=jnp.int32)
    j = np.arange(PAGE_ROWS * FOX_HEADS)[:, None] // FOX_HEADS
    k = np.arange(PAGE_ROWS)[None, :]
    r = np.arange(n_cb)[:, None] // (NSA_SEL_BLOCK // NSA_CMP_BLOCK)
    c = np.arange(n_sb)[None, :]
    return {'rope_c': jnp.pad(jnp.concatenate([cos, cos], axis=-1), lane_pad),
            'rope_s': jnp.pad(jnp.concatenate([-sin, sin], axis=-1), lane_pad),
            'fox_dt': jnp.asarray((j <= k).astype(np.float32)),
            'nsa_pair': jnp.asarray((r == c).astype(np.float32)),
            'nsa_bc': tab((past_len - cb_end)[None, :])[:, 0, :],
            'nsa_bsel': jnp.moveaxis(tab(past_len - key_pos), 0, 1),
            'nsa_bwin': tab((past_len - win_pos)[None, :])[:, 0, :],
            'nsa_b0': jnp.broadcast_to(tab(jnp.zeros((1, 1), jnp.int32))[:, 0, :], (SUB, LANE))}


def _prep_weights(w_in, mla_g_q, mla_g_kv, mla_w_uq, mla_w_uk, mla_w_uv, nsa_w_cmp_k, nsa_w_cmp_v, rel_bias, T):
    def cols(a, b):
        return w_in[:, :, a:b]

    def zeros(n):
        return jnp.zeros((DEPTH, D_MODEL, n), w_in.dtype)
    o = [0] + [int(v) for v in np.cumsum(IN_SPLITS)]
    (o_ql, o_kv, o_kr, o_s5, o_fq, o_fk, o_fv, o_ff, o_nq, o_kc, o_vc, o_ks, o_vs, o_kw, o_vw, o_ng, o_gate,
     o_end) = o
    half = MLA_ROPE // 2
    w_in_r = jnp.concatenate([
        cols(o_gate, o_end), cols(o_fq, o_fk), cols(o_fk, o_fv), cols(o_fv, o_ff), cols(o_nq, o_kc),
        cols(o_s5, o_fq), cols(o_ql, o_kv), zeros(2 * LANE - MLA_Q_LORA), cols(o_kv, o_kr),
        cols(o_kr, o_s5), zeros(LANE - MLA_ROPE),
        cols(o_kr + half, o_s5), cols(o_kr, o_kr + half), zeros(LANE - MLA_ROPE),
        cols(o_kc, o_ks), cols(o_ks, o_kw), cols(o_kw, o_ng), cols(o_ng, o_gate), cols(o_ff, o_nq),
        zeros(LANE - 3 * NSA_HEADS - FOX_HEADS)], axis=-1)
    assert w_in_r.shape[-1] == Z_COLS

    def rope_cols(r):
        return jnp.pad(r, ((0, 0), (0, 0), (0, 0), (0, LANE - MLA_ROPE))).reshape(DEPTH, MLA_Q_LORA, N_HEADS * LANE)
    nope = mla_w_uq[..., :MLA_NOPE].reshape(DEPTH, MLA_Q_LORA, MIX_WIDTH)
    rope = mla_w_uq[..., MLA_NOPE:]
    rope_sw = jnp.concatenate([rope[..., half:], rope[..., :half]], axis=-1)
    w1 = jnp.concatenate([nope, rope_cols(rope), rope_cols(rope_sw)], axis=-1)
    w1 = jnp.pad(w1, ((0, 0), (0, 2 * LANE - MLA_Q_LORA), (0, 0)))
    eye = jnp.eye(N_HEADS, dtype=mla_w_uk.dtype)
    wuk = jnp.einsum('dlhn,hg->dhngl', mla_w_uk, eye).reshape(DEPTH, MIX_WIDTH, N_HEADS * MLA_KV_LORA)
    wuv = jnp.einsum('dlhv,hg->dhlgv', mla_w_uv, eye).reshape(DEPTH, N_HEADS, MLA_KV_LORA, MIX_WIDTH)

    freq = ROPE_THETA ** (-jnp.arange(half, dtype=F32) / half)
    ang = jnp.arange(T, dtype=jnp.int32).astype(F32)[:, None] * freq[None, :]
    cos, sin = jnp.cos(ang), jnp.sin(ang)
    lane_pad = ((0, 0), (0, LANE - MLA_ROPE))
    wkv = jnp.concatenate([jnp.broadcast_to(nsa_w_cmp_k[:, :, None], (DEPTH, NSA_CMP_BLOCK, HEAD_DIM)),
                           jnp.broadcast_to(nsa_w_cmp_v[:, :, None], (DEPTH, NSA_CMP_BLOCK, HEAD_DIM))], axis=-1)
    tb, bc = _nsa_tables(rel_bias, T)
    return {'w_in': w_in_r.astype(BF16),
            'mla_w1': w1.astype(BF16), 'mla_wuk': wuk.astype(BF16), 'mla_wuv': wuv.astype(BF16),
            'mla_gq': jnp.pad(mla_g_q, ((0, 0), (0, 2 * LANE - MLA_Q_LORA))).reshape(DEPTH, 1, 2 * LANE),
            'mla_gkv': mla_g_kv.reshape(DEPTH, 1, MLA_KV_LORA),
            'rope_c': jnp.pad(jnp.concatenate([cos, cos], axis=-1), lane_pad),
            'rope_s': jnp.pad(jnp.concatenate([-sin, sin], axis=-1), lane_pad),
            'nsa_wkv': wkv.astype(F32), 'nsa_tb': tb, 'nsa_bc': bc}


def _forward(x, mods, pos, w, wc, caches, tm):
    B, T, _ = x.shape
    n = B * T
    xf = x.reshape(n, D_MODEL)
    per_row = T == 1
    outs = [[] for _ in range(7)]
    for l in range(DEPTH):
        m6 = mods[l].reshape(B, 6, D_MODEL)
        if per_row:
            sh1, sc1, g1, sh2, sc2, g2 = (m6[:, i].reshape(1, n, D_MODEL) for i in range(6))
        else:
            sh1, sc1, g1, sh2, sc2, g2 = (m6[:, i].reshape(B, 1, D_MODEL) for i in range(6))
        gains = w['norm_gains'][l]
        z = _inproj(xf, gains[0], sc1, sh1, wc['w_in'][l], tm)
        if caches is None:
            y_a, y_b, y_c, y_d, states = _mixers_prompt(z, B, T, l, w, wc)
        else:
            y_a, y_b, y_c, y_d, states = _mixers_decode(z, l, w, wc, caches)
        mla_rows, fox_k, fox_v, fox_logf, nsa_rows, win_state, s5_state = states
        ys = tuple(y.reshape(n, MIX_WIDTH) for y in (y_a, y_b, y_c, y_d))
        xf = _merge(ys, z, xf, wc['w_branch'][l], wc['w_out'][l], gains[1], g1, tm)
        xf = _ffn(xf, gains[2], sc2, sh2, wc['w_ffn_gate'][l], wc['w_ffn_up'][l], wc['w_ffn_down'][l],
                  gains[3], g2, tm)
        for o, s in zip(outs, (mla_rows, fox_k, fox_v, fox_logf, nsa_rows, win_state, s5_state)):
            o.append(s)
    return xf.reshape(B, T, D_MODEL), tuple(jnp.stack(o) for o in outs)


def kernel(x_prompt, x_sample, c_prompt, c_sample, cache_mla, cache_fox_k, cache_fox_v, cache_fox_logf, cache_nsa, state_nsa_win, state_s5, page_table, w_ada, b_ada, norm_gains, w_in, mla_g_q, mla_g_kv, mla_w_uq, mla_w_uk, mla_w_uv, s5_lambda_re, s5_lambda_im, s5_log_dt, s5_b_re, s5_b_im, s5_c_re, s5_c_im, s5_d, s5_w_glu, fox_b_f, nsa_w_cmp_k, nsa_w_cmp_v, rel_bias, w_branch, w_out, w_ffn_gate, w_ffn_up, w_ffn_down):
    w = {'norm_gains': norm_gains,
         'mla_g_q': mla_g_q, 'mla_g_kv': mla_g_kv, 'mla_w_uq': mla_w_uq, 'mla_w_uk': mla_w_uk,
         'mla_w_uv': mla_w_uv, 's5_lambda_re': s5_lambda_re, 's5_lambda_im': s5_lambda_im,
         's5_log_dt': s5_log_dt, 's5_b_re': s5_b_re, 's5_b_im': s5_b_im, 's5_c_re': s5_c_re,
         's5_c_im': s5_c_im, 's5_d': s5_d, 's5_w_glu': s5_w_glu, 'fox_b_f': fox_b_f,
         'nsa_w_cmp_k': nsa_w_cmp_k, 'nsa_w_cmp_v': nsa_w_cmp_v, 'rel_bias': rel_bias}
    wc = _prep_weights(w_in, mla_g_q, mla_g_kv, mla_w_uq, mla_w_uk, mla_w_uv, nsa_w_cmp_k, nsa_w_cmp_v, rel_bias,
                       x_prompt.shape[1])
    wc.update({'w_branch': w_branch.astype(BF16), 'w_out': w_out.astype(BF16),
               'w_ffn_gate': w_ffn_gate.astype(BF16), 'w_ffn_up': w_ffn_up.astype(BF16),
               'w_ffn_down': w_ffn_down.astype(BF16)})
    wc['dec'] = _decode_tables(rel_bias, page_table.shape[1] * cache_mla.shape[2], x_sample.shape[0])
    nb_p = c_prompt.shape[0]
    mods = _ada(jnp.concatenate([c_prompt, c_sample], axis=0), w_ada, b_ada)
    past_len = page_table.shape[1] * cache_mla.shape[2]
    pos_p = jnp.arange(x_prompt.shape[1], dtype=jnp.int32)
    pos_s = past_len + jnp.arange(x_sample.shape[1], dtype=jnp.int32)
    caches = (cache_mla, cache_fox_k, cache_fox_v, cache_fox_logf, cache_nsa,
              state_nsa_win, state_s5, page_table)
    y_prompt, st_p = _forward(x_prompt, mods[:, :nb_p], pos_p, w, wc, None, 512)
    y_sample, st_s = _forward(x_sample, mods[:, nb_p:], pos_s, w, wc, caches, x_sample.shape[0])
    mla_p, fox_k_p, fox_v_p, fox_logf_p, nsa_p, win_p, s5_p = st_p
    mla_s, fox_k_s, fox_v_s, fox_logf_s, nsa_s, win_s, s5_s = st_s
    return (y_prompt, y_sample, mla_p, mla_s, fox_k_p, fox_k_s, fox_v_p, fox_v_s,
            fox_logf_p, fox_logf_s, nsa_p, nsa_s, win_p, win_s, s5_p, s5_s)
```

```python
import functools
import math

import numpy as np
import jax
import jax.numpy as jnp
from jax import lax
from jax.experimental import pallas as pl
from jax.experimental.pallas import tpu as pltpu

F32 = jnp.float32
BF16 = jnp.bfloat16

D_MODEL = 1024
DEPTH = 2
HEAD_DIM = 64
MIX_WIDTH = D_MODEL // 4
N_BRANCH = 4
Q_BLOCK = 128
PAGE_ROWS = 128
EPS = 1e-6
NEG = -1e30
FORCE = 1e30

MLA_HEADS = MIX_WIDTH // HEAD_DIM
MLA_Q_LORA = 3 * D_MODEL // 16
MLA_KV_LORA = D_MODEL // 8
MLA_NOPE = HEAD_DIM
MLA_ROPE = HEAD_DIM // 2
MLA_V = HEAD_DIM
ROPE_THETA = 10000.0

S5_GROUP = 16
S5_GROUPS = MIX_WIDTH // S5_GROUP
S5_STATE = 64

FOX_HEADS = MIX_WIDTH // HEAD_DIM

NSA_HEADS = MIX_WIDTH // HEAD_DIM
NSA_CMP_BLOCK = 32
NSA_SEL_BLOCK = 64
NSA_TOPK = 16
NSA_WINDOW = 512

REL_BUCKETS = 32
REL_MAX_DIST = 128

FFN_HIDDEN = (-(-8 * D_MODEL // 3) + 255) // 256 * 256

IN_SPLITS = (MLA_Q_LORA, MLA_KV_LORA, MLA_ROPE,
             MIX_WIDTH,
             MIX_WIDTH, MIX_WIDTH, MIX_WIDTH, FOX_HEADS,
             MIX_WIDTH, HEAD_DIM, HEAD_DIM, HEAD_DIM, HEAD_DIM, HEAD_DIM, HEAD_DIM, 3 * NSA_HEADS,
             N_BRANCH * D_MODEL)
IN_COLS = sum(IN_SPLITS)
GATE_COLS = N_BRANCH * D_MODEL
MIX_COLS = IN_COLS - GATE_COLS
MIX_SPLIT_POINTS = tuple(int(v) for v in np.cumsum(IN_SPLITS[:-1])[:-1])

LANE = 128
VMEM_LIMIT = 48 * 1024 * 1024

Z_GATE = 0
Z_FQ = Z_GATE + GATE_COLS
Z_FK = Z_FQ + MIX_WIDTH
Z_FV = Z_FK + MIX_WIDTH
Z_NQ = Z_FV + MIX_WIDTH
Z_S5U = Z_NQ + MIX_WIDTH
Z_QLAT = Z_S5U + MIX_WIDTH
Z_KVLAT = Z_QLAT + 2 * LANE
Z_KROPE = Z_KVLAT + LANE
Z_KROPE_SW = Z_KROPE + LANE
Z_KCVC = Z_KROPE_SW + LANE
Z_KSVS = Z_KCVC + LANE
Z_KWVW = Z_KSVS + LANE
Z_NG = Z_KWVW + LANE
Z_COLS = Z_NG + LANE
Z_TILE = Z_COLS // 3


def _rms(x, g):
    return x * lax.rsqrt(jnp.mean(x * x, axis=-1, keepdims=True) + EPS) * g


def _ada_kernel(c_ref, w_ref, b_ref, o_ref):
    c = c_ref[...]
    a = (c * jax.nn.sigmoid(c)).astype(BF16)
    o_ref[0] = jnp.dot(a, w_ref[0], preferred_element_type=F32) + b_ref[0]


def _ada(c_all, w_ada, b_ada):
    rows = c_all.shape[0]
    tn = 1536
    return pl.pallas_call(
        _ada_kernel,
        grid=(DEPTH, 6 * D_MODEL // tn),
        in_specs=[pl.BlockSpec((rows, D_MODEL), lambda l, j: (0, 0)),
                  pl.BlockSpec((1, D_MODEL, tn), lambda l, j: (l, 0, j)),
                  pl.BlockSpec((1, 1, tn), lambda l, j: (l, 0, j))],
        out_specs=pl.BlockSpec((1, rows, tn), lambda l, j: (l, 0, j)),
        out_shape=jax.ShapeDtypeStruct((DEPTH, rows, 6 * D_MODEL), F32),
        compiler_params=pltpu.CompilerParams(dimension_semantics=("arbitrary", "arbitrary"),
                                             vmem_limit_bytes=VMEM_LIMIT),
        name="ada_mod",
    )(c_all, w_ada.astype(BF16), b_ada.reshape(DEPTH, 1, 6 * D_MODEL))


def _inproj_kernel(x_ref, gain_ref, sc_ref, sh_ref, w_ref, o_ref, h_ref):
    @pl.when(pl.program_id(1) == 0)
    def _():
        h = _rms(x_ref[...], gain_ref[...]) * (1.0 + sc_ref[0]) + sh_ref[0]
        h_ref[...] = h.astype(BF16)
    o_ref[...] = jnp.dot(h_ref[...], w_ref[...], preferred_element_type=F32)


def _inproj(x, gain, sc, sh, w, tm):
    n = x.shape[0]
    tiles_per_group = n // tm // sc.shape[0]
    tn = Z_TILE
    mod_spec = pl.BlockSpec((1, sc.shape[1], D_MODEL), lambda i, j: (i // tiles_per_group, 0, 0))
    return pl.pallas_call(
        _inproj_kernel,
        grid=(n // tm, Z_COLS // tn),
        in_specs=[pl.BlockSpec((tm, D_MODEL), lambda i, j: (i, 0)),
                  pl.BlockSpec((1, D_MODEL), lambda i, j: (0, 0)),
                  mod_spec, mod_spec,
                  pl.BlockSpec((D_MODEL, tn), lambda i, j: (0, j))],
        out_specs=pl.BlockSpec((tm, tn), lambda i, j: (i, j)),
        out_shape=jax.ShapeDtypeStruct((n, Z_COLS), F32),
        scratch_shapes=[pltpu.VMEM((tm, D_MODEL), BF16)],
        compiler_params=pltpu.CompilerParams(dimension_semantics=("parallel", "arbitrary"),
                                             vmem_limit_bytes=VMEM_LIMIT),
        name="in_proj",
    )(x, gain.reshape(1, D_MODEL), sc, sh, w)


def _merge_kernel(ya_ref, yb_ref, yc_ref, yd_ref, gl_ref, x_ref, wb_ref, wo_ref, gain_ref, g1_ref, o_ref):
    acc = None
    for n, y_ref in enumerate((ya_ref, yb_ref, yc_ref, yd_ref)):
        p = jnp.dot(y_ref[...].astype(BF16), wb_ref[n], preferred_element_type=F32)
        t = jax.nn.sigmoid(gl_ref[:, n * D_MODEL:(n + 1) * D_MODEL]) * p
        acc = t if acc is None else acc + t
    m = jnp.dot(acc.astype(BF16), wo_ref[...], preferred_element_type=F32)
    o_ref[...] = x_ref[...] + g1_ref[0] * _rms(m, gain_ref[...])


def _merge(ys, z, x, wb, wo, gain, g1, tm):
    n = x.shape[0]
    tiles_per_group = n // tm // g1.shape[0]
    y_spec = pl.BlockSpec((tm, MIX_WIDTH), lambda i: (i, 0))
    return pl.pallas_call(
        _merge_kernel,
        grid=(n // tm,),
        in_specs=[y_spec, y_spec, y_spec, y_spec,
                  pl.BlockSpec((tm, GATE_COLS), lambda i: (i, 0)),
                  pl.BlockSpec((tm, D_MODEL), lambda i: (i, 0)),
                  pl.BlockSpec((N_BRANCH, MIX_WIDTH, D_MODEL), lambda i: (0, 0, 0)),
                  pl.BlockSpec((D_MODEL, D_MODEL), lambda i: (0, 0)),
                  pl.BlockSpec((1, D_MODEL), lambda i: (0, 0)),
                  pl.BlockSpec((1, g1.shape[1], D_MODEL), lambda i: (i // tiles_per_group, 0, 0))],
        out_specs=pl.BlockSpec((tm, D_MODEL), lambda i: (i, 0)),
        out_shape=jax.ShapeDtypeStruct((n, D_MODEL), F32),
        compiler_params=pltpu.CompilerParams(dimension_semantics=("parallel",),
                                             vmem_limit_bytes=VMEM_LIMIT),
        name="merge",
    )(*ys, z, x, wb, wo, gain.reshape(1, D_MODEL), g1)


def _ffn_kernel(x_ref, gain_h_ref, sc_ref, sh_ref, wg_ref, wu_ref, wd_ref, gain_o_ref, g2_ref, o_ref,
                h_ref, acc_ref):
    j = pl.program_id(1)

    @pl.when(j == 0)
    def _():
        h = _rms(x_ref[...], gain_h_ref[...]) * (1.0 + sc_ref[0]) + sh_ref[0]
        h_ref[...] = h.astype(BF16)
        acc_ref[...] = jnp.zeros_like(acc_ref)

    h = h_ref[...]
    a = jnp.dot(h, wg_ref[...], preferred_element_type=F32)
    b = jnp.dot(h, wu_ref[...], preferred_element_type=F32)
    t = (a * jax.nn.sigmoid(a)) * b
    acc_ref[...] += jnp.dot(t.astype(BF16), wd_ref[...], preferred_element_type=F32)

    @pl.when(j == pl.num_programs(1) - 1)
    def _():
        o_ref[...] = x_ref[...] + g2_ref[0] * _rms(acc_ref[...], gain_o_ref[...])


def _ffn(x, gain_h, sc, sh, wg, wu, wd, gain_o, g2, tm):
    n = x.shape[0]
    tiles_per_group = n // tm // sc.shape[0]
    th = FFN_HIDDEN // 2
    mod_spec = pl.BlockSpec((1, sc.shape[1], D_MODEL), lambda i, j: (i // tiles_per_group, 0, 0))
    vec_spec = pl.BlockSpec((1, D_MODEL), lambda i, j: (0, 0))
    return pl.pallas_call(
        _ffn_kernel,
        grid=(n // tm, FFN_HIDDEN // th),
        in_specs=[pl.BlockSpec((tm, D_MODEL), lambda i, j: (i, 0)),
                  vec_spec, mod_spec, mod_spec,
                  pl.BlockSpec((D_MODEL, th), lambda i, j: (0, j)),
                  pl.BlockSpec((D_MODEL, th), lambda i, j: (0, j)),
                  pl.BlockSpec((th, D_MODEL), lambda i, j: (j, 0)),
                  vec_spec, mod_spec],
        out_specs=pl.BlockSpec((tm, D_MODEL), lambda i, j: (i, 0)),
        out_shape=jax.ShapeDtypeStruct((n, D_MODEL), F32),
        scratch_shapes=[pltpu.VMEM((tm, D_MODEL), BF16), pltpu.VMEM((tm, D_MODEL), F32)],
        compiler_params=pltpu.CompilerParams(dimension_semantics=("parallel", "arbitrary"),
                                             vmem_limit_bytes=VMEM_LIMIT),
        name="ffn",
    )(x, gain_h.reshape(1, D_MODEL), sc, sh, wg, wu, wd, gain_o.reshape(1, D_MODEL), g2)


TQ = 256
N_HEADS = 4
SCALE_64 = HEAD_DIM ** -0.5


def _dot_nt(a, b):
    return lax.dot_general(a, b, (((1,), (1,)), ((), ())), preferred_element_type=F32)


def _softmax_step(s, v, m_ref, l_ref, acc_ref):
    m_old = m_ref[...]
    m_new = jnp.maximum(m_old, jnp.max(s, axis=-1, keepdims=True))
    alpha = jnp.exp(m_old - m_new)
    p = jnp.exp(s - m_new)
    l_ref[...] = alpha * l_ref[...] + jnp.sum(p, axis=-1, keepdims=True)
    acc_ref[...] = alpha * acc_ref[...] + jnp.dot(p.astype(BF16), v, preferred_element_type=F32)
    m_ref[...] = m_new


def _softmax_reset(m_ref, l_ref, acc_ref):
    m_ref[...] = jnp.full(m_ref.shape, NEG, F32)
    l_ref[...] = jnp.zeros(l_ref.shape, F32)
    acc_ref[...] = jnp.zeros(acc_ref.shape, F32)


def _causal_neg(rows):
    i = lax.broadcasted_iota(jnp.int32, (TQ, TQ), 0)
    j = lax.broadcasted_iota(jnp.int32, (TQ, TQ), 1)
    neg = jnp.where(j <= i, 0.0, NEG).astype(F32)
    return jnp.concatenate([neg] * rows, axis=0)


MLA_QW = 2 * LANE


def _mla_prep_kernel(ql_ref, kv_ref, kr_ref, krs_ref, cq_ref, sq_ref, gq_ref, gkv_ref, w1_ref, wuk_ref,
                     qcat_ref, kf_ref, kb_ref):
    ql = ql_ref[...]
    qn = ql * lax.rsqrt(jnp.sum(ql * ql, axis=-1, keepdims=True) * (1.0 / MLA_Q_LORA) + EPS) * gq_ref[...]
    q1 = jnp.dot(qn.astype(BF16), w1_ref[...], preferred_element_type=F32)
    q_abs = jnp.dot(q1[:, :MIX_WIDTH].astype(BF16), wuk_ref[...], preferred_element_type=F32)
    cc, ss = cq_ref[...], sq_ref[...]
    parts = []
    for h in range(N_HEADS):
        r = q1[:, MIX_WIDTH + h * LANE:MIX_WIDTH + (h + 1) * LANE]
        rs = q1[:, MIX_WIDTH + (N_HEADS + h) * LANE:MIX_WIDTH + (N_HEADS + h + 1) * LANE]
        parts += [q_abs[:, h * LANE:(h + 1) * LANE], r * cc + rs * ss]
    qcat_ref[...] = jnp.concatenate(parts, axis=1).astype(BF16)
    c_new = _rms(kv_ref[...], gkv_ref[...])
    kr = kr_ref[...] * cc + krs_ref[...] * ss
    k = jnp.concatenate([c_new, kr], axis=1)
    kf_ref[...] = k
    kb_ref[...] = k.astype(BF16)


def _mla_prep(z, cq, sq, gq, gkv, w1, wuk, T, tm):
    n = z.shape[0]
    tiles_per_seq = T // tm
    row = lambda c: pl.BlockSpec((tm, LANE), lambda i: (i, c))
    tab = pl.BlockSpec((tm, LANE), lambda i: (i % tiles_per_seq, 0))
    full = lambda a: pl.BlockSpec(a.shape, lambda i: (0,) * a.ndim)
    return pl.pallas_call(
        _mla_prep_kernel,
        grid=(n // tm,),
        in_specs=[pl.BlockSpec((tm, 2 * LANE), lambda i: (i, Z_QLAT // (2 * LANE))),
                  row(Z_KVLAT // LANE), row(Z_KROPE // LANE), row(Z_KROPE_SW // LANE),
                  tab, tab, full(gq), full(gkv), full(w1), full(wuk)],
        out_specs=[pl.BlockSpec((tm, N_HEADS * MLA_QW), lambda i: (i, 0)),
                   pl.BlockSpec((tm, MLA_QW), lambda i: (i, 0)),
                   pl.BlockSpec((tm, MLA_QW), lambda i: (i, 0))],
        out_shape=[jax.ShapeDtypeStruct((n, N_HEADS * MLA_QW), BF16),
                   jax.ShapeDtypeStruct((n, MLA_QW), F32),
                   jax.ShapeDtypeStruct((n, MLA_QW), BF16)],
        compiler_params=pltpu.CompilerParams(dimension_semantics=("parallel",), vmem_limit_bytes=VMEM_LIMIT),
        name="mla_prep",
    )(z, z, z, z, cq, sq, gq, gkv, w1, wuk)


def _mla_attn_kernel(q_ref, k_ref, wuv_ref, o_ref, m_ref, l_ref, acc_ref):
    qi = pl.program_id(1)
    q = q_ref[...]
    qs = jnp.concatenate([q[:, h * MLA_QW:(h + 1) * MLA_QW] for h in range(N_HEADS)], axis=0)
    scale = (MLA_NOPE + MLA_ROPE) ** -0.5
    _softmax_reset(m_ref, l_ref, acc_ref)

    def step(kt, neg):
        k = k_ref[pl.ds(pl.multiple_of(kt * TQ, TQ), TQ), :]
        s = _dot_nt(qs, k) * scale
        if neg is not None:
            s = s + neg
        _softmax_step(s, k[:, :MLA_KV_LORA], m_ref, l_ref, acc_ref)

    lax.fori_loop(0, qi, lambda kt, c: (step(kt, None), c)[1], 0)
    step(qi, _causal_neg(N_HEADS))
    o_lat = (acc_ref[...] / l_ref[...]).astype(BF16)
    out = None
    for h in range(N_HEADS):
        t = jnp.dot(o_lat[h * TQ:(h + 1) * TQ], wuv_ref[h], preferred_element_type=F32)
        out = t if out is None else out + t
    o_ref[...] = out


def _mla_attn(qcat, kb, wuv, B, T):
    return pl.pallas_call(
        _mla_attn_kernel,
        grid=(B, T // TQ),
        in_specs=[pl.BlockSpec((TQ, N_HEADS * MLA_QW), lambda b, i: (b * (T // TQ) + i, 0)),
                  pl.BlockSpec((T, MLA_QW), lambda b, i: (b, 0)),
                  pl.BlockSpec(wuv.shape, lambda b, i: (0, 0, 0))],
        out_specs=pl.BlockSpec((TQ, MIX_WIDTH), lambda b, i: (b * (T // TQ) + i, 0)),
        out_shape=jax.ShapeDtypeStruct((B * T, MIX_WIDTH), F32),
        scratch_shapes=[pltpu.VMEM((N_HEADS * TQ, 1), F32), pltpu.VMEM((N_HEADS * TQ, 1), F32),
                        pltpu.VMEM((N_HEADS * TQ, MLA_KV_LORA), F32)],
        compiler_params=pltpu.CompilerParams(dimension_semantics=("parallel", "arbitrary"),
                                             vmem_limit_bytes=VMEM_LIMIT),
        name="mla_attn",
    )(qcat, kb, wuv)


def _fox_attn_kernel(q_ref, k_ref, v_ref, fq_ref, fk_ref, o_ref, fqb_ref, m_ref, l_ref, acc_ref):
    qi = pl.program_id(1)
    q = q_ref[...] * SCALE_64
    head = lax.broadcasted_iota(jnp.int32, (TQ, MIX_WIDTH), 1) // HEAD_DIM
    qs = jnp.concatenate([jnp.where(head == h, q, 0.0) for h in range(N_HEADS)], axis=0).astype(BF16)
    fq = fq_ref[...]
    for h in range(N_HEADS):
        fqb_ref[h * TQ:(h + 1) * TQ, :] = jnp.broadcast_to(fq[:, h:h + 1], (TQ, TQ))
    _softmax_reset(m_ref, l_ref, acc_ref)

    def step(kt, neg):
        off = pl.multiple_of(kt * TQ, TQ)
        k = k_ref[pl.ds(off, TQ), :].astype(BF16)
        v = v_ref[pl.ds(off, TQ), :].astype(BF16)
        fk = fk_ref[kt]
        fkb = jnp.concatenate([jnp.broadcast_to(fk[h:h + 1, :], (TQ, TQ)) for h in range(N_HEADS)], axis=0)
        s = _dot_nt(qs, k) + (fqb_ref[...] - fkb)
        if neg is not None:
            s = s + neg
        _softmax_step(s, v, m_ref, l_ref, acc_ref)

    lax.fori_loop(0, qi, lambda kt, c: (step(kt, None), c)[1], 0)
    step(qi, _causal_neg(N_HEADS))
    o = acc_ref[...] / l_ref[...]
    out = jnp.zeros((TQ, MIX_WIDTH), F32)
    for h in range(N_HEADS):
        out = jnp.where(head == h, o[h * TQ:(h + 1) * TQ], out)
    o_ref[...] = out


def _fox_attn(z, fq, fk, B, T):
    nq = T // TQ
    col = lambda c: pl.BlockSpec((T, MIX_WIDTH), lambda b, i: (b, c))
    return pl.pallas_call(
        _fox_attn_kernel,
        grid=(B, nq),
        in_specs=[pl.BlockSpec((TQ, MIX_WIDTH), lambda b, i: (b * nq + i, Z_FQ // MIX_WIDTH)),
                  col(Z_FK // MIX_WIDTH), col(Z_FV // MIX_WIDTH),
                  pl.BlockSpec((TQ, LANE), lambda b, i: (b * nq + i, 0)),
                  pl.BlockSpec((None, nq, 8, TQ), lambda b, i: (b, 0, 0, 0))],
        out_specs=pl.BlockSpec((TQ, MIX_WIDTH), lambda b, i: (b * nq + i, 0)),
        out_shape=jax.ShapeDtypeStruct((B * T, MIX_WIDTH), F32),
        scratch_shapes=[pltpu.VMEM((N_HEADS * TQ, TQ), F32),
                        pltpu.VMEM((N_HEADS * TQ, 1), F32), pltpu.VMEM((N_HEADS * TQ, 1), F32),
                        pltpu.VMEM((N_HEADS * TQ, MIX_WIDTH), F32)],
        compiler_params=pltpu.CompilerParams(dimension_semantics=("parallel", "arbitrary"),
                                             vmem_limit_bytes=VMEM_LIMIT),
        name="fox_attn",
    )(z, z, z, fq, fk)


def _nsa_tables(rel_bias, T):
    def tab(dist):
        oh = jax.nn.one_hot(_rel_bucket(dist), REL_BUCKETS, dtype=F32)
        b = jnp.einsum('rcb,bh->hrc', oh, rel_bias.astype(F32), precision=lax.Precision.HIGHEST)
        return b.reshape(N_HEADS * dist.shape[0], dist.shape[1])
    i = jnp.arange(TQ, dtype=jnp.int32)[:, None]
    j = jnp.arange(TQ, dtype=jnp.int32)[None, :]
    tile4 = lambda m: jnp.concatenate([m] * N_HEADS, axis=0)
    far = tab(jnp.full((TQ, TQ), REL_MAX_DIST, jnp.int32))
    prev = tab(TQ + i - j)
    diag = jnp.where(tile4(j <= i), tab(i - j), NEG)
    edge = jnp.where(tile4(j >= i), far, NEG)
    tb = jnp.stack([far, prev, diag, edge])
    n_cb = T // NSA_CMP_BLOCK
    pos = jnp.arange(T, dtype=jnp.int32)[:, None]
    cb_end = jnp.arange(n_cb, dtype=jnp.int32)[None, :] * NSA_CMP_BLOCK + (NSA_CMP_BLOCK - 1)
    bc = tab(pos - cb_end).reshape(N_HEADS, T // TQ, TQ, n_cb)
    bc = jnp.moveaxis(bc, 1, 0).reshape(T // TQ, N_HEADS * TQ, n_cb)
    return tb, bc


def _nsa_attn_kernel(q_ref, kcvc_ref, ksvs_ref, kwvw_ref, ng_ref, wkv_ref, tb_ref, bc_ref, o_ref,
                     cb_ref, expand_ref, negsel_ref, m_ref, l_ref, acc_ref):
    qi = pl.program_id(1)
    T = ksvs_ref.shape[0]
    n_cb = T // NSA_CMP_BLOCK
    n_kt = T // TQ

    @pl.when(qi == 0)
    def _():
        x = kcvc_ref[...].reshape(n_cb, NSA_CMP_BLOCK, LANE) * wkv_ref[...][None]
        cb_ref[...] = jnp.sum(x, axis=1)
        r = lax.broadcasted_iota(jnp.int32, (n_cb, T), 0)
        key = lax.broadcasted_iota(jnp.int32, (n_cb, T), 1)
        expand_ref[...] = jnp.where(r == 2 * (key // NSA_SEL_BLOCK), 1.0, 0.0).astype(BF16)

    lane = lax.broadcasted_iota(jnp.int32, (TQ, LANE), 1)
    lo = lane < HEAD_DIM
    q = q_ref[...] * SCALE_64
    t0, t1 = q[:, :LANE], q[:, LANE:]
    qs = jnp.concatenate([jnp.where(lo, t0, 0.0), jnp.where(lo, pltpu.roll(t0, HEAD_DIM, 1), 0.0),
                          jnp.where(lo, t1, 0.0), jnp.where(lo, pltpu.roll(t1, HEAD_DIM, 1), 0.0)],
                         axis=0).astype(BF16)

    cb = cb_ref[...].astype(BF16)
    qpos = qi * TQ + lax.broadcasted_iota(jnp.int32, (TQ, n_cb), 0)
    cb_end = lax.broadcasted_iota(jnp.int32, (TQ, n_cb), 1) * NSA_CMP_BLOCK + (NSA_CMP_BLOCK - 1)
    valid = jnp.concatenate([cb_end <= qpos] * N_HEADS, axis=0)
    lc = jnp.where(valid, _dot_nt(qs, cb) + bc_ref[0], NEG)
    e = jnp.where(valid, jnp.exp(lc - jnp.max(lc, axis=-1, keepdims=True)), 0.0)
    pc = e / jnp.maximum(jnp.sum(e, axis=-1, keepdims=True), 1e-30)
    o_c = jnp.dot(pc.astype(BF16), cb, preferred_element_type=F32)

    psum = pc[0:TQ] + pc[TQ:2 * TQ] + pc[2 * TQ:3 * TQ] + pc[3 * TQ:4 * TQ]
    imp = (psum + pltpu.roll(psum, n_cb - 1, 1)).T
    row = lax.broadcasted_iota(jnp.int32, (n_cb, TQ), 0)
    qpos_t = qi * TQ + lax.broadcasted_iota(jnp.int32, (n_cb, TQ), 1)
    blk = row // 2
    score = jnp.where(qpos_t // NSA_SEL_BLOCK == blk, FORCE,
                      jnp.where(blk * NSA_SEL_BLOCK <= qpos_t, imp, NEG))
    score = jnp.where(row % 2 == 0, score, 2 * NEG)
    cnt = jnp.zeros((n_cb, TQ), F32)
    for i in range(0, n_cb, 2):
        si = score[i:i + 1, :]
        tie = jnp.where(row > i, 1.0, 0.0)
        cnt = cnt + jnp.where(si > score, 1.0, jnp.where(si == score, tie, 0.0))
    sel = jnp.where(cnt < NSA_TOPK, jnp.where(score > NEG / 2, 1.0, 0.0), 0.0).T
    keymask = jnp.dot(sel.astype(BF16), expand_ref[...], preferred_element_type=F32)
    for kt in range(n_kt):
        negsel_ref[kt] = (keymask[:, kt * TQ:(kt + 1) * TQ] - 1.0) * (-NEG)

    def step(kt, tab, kv_ref, use_sel):
        kv = kv_ref[pl.ds(pl.multiple_of(kt * TQ, TQ), TQ), :].astype(BF16)
        s = _dot_nt(qs, kv) + tb_ref[tab]
        if use_sel:
            ns = negsel_ref[kt]
            s = s + jnp.concatenate([ns] * N_HEADS, axis=0)
        _softmax_step(s, kv, m_ref, l_ref, acc_ref)

    _softmax_reset(m_ref, l_ref, acc_ref)
    lax.fori_loop(0, qi + 1,
                  lambda kt, c: (step(kt, jnp.maximum(kt - qi + 2, 0), ksvs_ref, True), c)[1], 0)
    o_s = acc_ref[...] / l_ref[...]
    _softmax_reset(m_ref, l_ref, acc_ref)
    lax.fori_loop(jnp.maximum(qi - 2, 0), qi + 1,
                  lambda kt, c: (step(kt, jnp.where(kt == qi - 2, 3, kt - qi + 2), kwvw_ref, False), c)[1], 0)
    o_w = acc_ref[...] / l_ref[...]

    g = jax.nn.sigmoid(ng_ref[...])
    mixed = []
    for h in range(N_HEADS):
        rows = slice(h * TQ, (h + 1) * TQ)
        mixed.append(g[:, h:h + 1] * o_c[rows] + g[:, N_HEADS + h:N_HEADS + h + 1] * o_s[rows]
                     + g[:, 2 * N_HEADS + h:2 * N_HEADS + h + 1] * o_w[rows])
    o_ref[...] = jnp.concatenate([jnp.where(lo, pltpu.roll(mixed[0], HEAD_DIM, 1), mixed[1]),
                                  jnp.where(lo, pltpu.roll(mixed[2], HEAD_DIM, 1), mixed[3])], axis=1)


def _nsa_attn(z, wkv, tb, bc, B, T):
    nq = T // TQ
    n_cb = T // NSA_CMP_BLOCK
    kv = lambda c: pl.BlockSpec((T, LANE), lambda b, i: (b, c))
    return pl.pallas_call(
        _nsa_attn_kernel,
        grid=(B, nq),
        in_specs=[pl.BlockSpec((TQ, MIX_WIDTH), lambda b, i: (b * nq + i, Z_NQ // MIX_WIDTH)),
                  kv(Z_KCVC // LANE), kv(Z_KSVS // LANE), kv(Z_KWVW // LANE),
                  pl.BlockSpec((TQ, LANE), lambda b, i: (b * nq + i, Z_NG // LANE)),
                  pl.BlockSpec(wkv.shape, lambda b, i: (0, 0)),
                  pl.BlockSpec(tb.shape, lambda b, i: (0, 0, 0)),
                  pl.BlockSpec((1, N_HEADS * TQ, n_cb), lambda b, i: (i, 0, 0))],
        out_specs=pl.BlockSpec((TQ, MIX_WIDTH), lambda b, i: (b * nq + i, 0)),
        out_shape=jax.ShapeDtypeStruct((B * T, MIX_WIDTH), F32),
        scratch_shapes=[pltpu.VMEM((n_cb, LANE), F32), pltpu.VMEM((n_cb, T), BF16),
                        pltpu.VMEM((nq, TQ, TQ), F32),
                        pltpu.VMEM((N_HEADS * TQ, 1), F32), pltpu.VMEM((N_HEADS * TQ, 1), F32),
                        pltpu.VMEM((N_HEADS * TQ, LANE), F32)],
        compiler_params=pltpu.CompilerParams(dimension_semantics=("parallel", "arbitrary"),
                                             vmem_limit_bytes=VMEM_LIMIT),
        name="nsa_attn",
    )(z, z, z, z, z, wkv, tb, bc)


PAGES_PER_STEP = 16
SUB = 8
HIGHEST = lax.Precision.HIGHEST


def _page_specs(layer, rows, row_block=0):
    return [pl.BlockSpec((None, None, rows, PAGE_ROWS),
                         lambda b, j, pt, k=k: (layer, pt[b, j * PAGES_PER_STEP + k], row_block, 0))
            for k in range(PAGES_PER_STEP)]


def _softmax_step_t(s, vt, m_ref, l_ref, acc_ref):
    m_old = m_ref[...]
    m_new = jnp.maximum(m_old, jnp.max(s, axis=-1, keepdims=True))
    alpha = jnp.exp(m_old - m_new)
    p = jnp.exp(s - m_new)
    l_ref[...] = alpha * l_ref[...] + jnp.sum(p, axis=-1, keepdims=True)
    acc_ref[...] = alpha * acc_ref[...] + _dot_nt(p.astype(BF16), vt)
    m_ref[...] = m_new


def _seq_spec(rows, width):
    return pl.BlockSpec((None, rows, width), lambda b, j, pt: (b, 0, 0))


def _const_spec(a):
    return pl.BlockSpec(a.shape, lambda b, j, pt: (0,) * a.ndim)


def _merge_new_key(s_new, v_new, m_ref, l_ref, acc_ref):
    m_old = m_ref[...]
    m_new = jnp.maximum(m_old, s_new)
    alpha = jnp.exp(m_old - m_new)
    p = jnp.exp(s_new - m_new)
    return (alpha * acc_ref[...] + p * v_new) / (alpha * l_ref[...] + p)


def _bf16_round(x):
    return x.astype(BF16).astype(F32)


def _paged_call(kernel_fn, name, page_table, n_steps, in_specs, out_specs, out_shape, scratch_shapes, args):
    nb = page_table.shape[0]
    return pl.pallas_call(
        kernel_fn,
        grid_spec=pltpu.PrefetchScalarGridSpec(num_scalar_prefetch=1, grid=(nb, n_steps), in_specs=in_specs,
                                               out_specs=out_specs, scratch_shapes=scratch_shapes),
        out_shape=out_shape,
        compiler_params=pltpu.CompilerParams(dimension_semantics=("parallel", "arbitrary"),
                                             vmem_limit_bytes=VMEM_LIMIT),
        name=name,
    )(page_table, *args)


def _mla_dec_kernel(pt_ref, q_ref, knew_ref, *refs):
    pages = refs[:PAGES_PER_STEP]
    o_ref, m_ref, l_ref, acc_ref = refs[PAGES_PER_STEP:]
    j = pl.program_id(1)
    scale = (MLA_NOPE + MLA_ROPE) ** -0.5

    @pl.when(j == 0)
    def _():
        _softmax_reset(m_ref, l_ref, acc_ref)

    qs = q_ref[...]
    kt = jnp.concatenate([p[...] for p in pages], axis=1).astype(BF16)
    s = jnp.dot(qs[:, :MLA_KV_LORA + MLA_ROPE], kt, preferred_element_type=F32) * scale
    _softmax_step_t(s, kt[:MLA_KV_LORA], m_ref, l_ref, acc_ref)

    @pl.when(j == pl.num_programs(1) - 1)
    def _():
        kn = _bf16_round(knew_ref[...])
        s_new = jnp.sum(qs.astype(F32) * kn, axis=-1, keepdims=True) * scale
        o_ref[...] = _merge_new_key(s_new, kn[:, :MLA_KV_LORA], m_ref, l_ref, acc_ref)


def _mla_dec(layer, page_table, cache_mla, q, knew):
    nb, n_pages = page_table.shape
    return _paged_call(
        _mla_dec_kernel, "mla_decode", page_table, n_pages // PAGES_PER_STEP,
        [_seq_spec(SUB, MLA_QW), _seq_spec(1, MLA_QW)] + _page_specs(layer, cache_mla.shape[2]),
        _seq_spec(SUB, MLA_KV_LORA), jax.ShapeDtypeStruct((nb, SUB, MLA_KV_LORA), F32),
        [pltpu.VMEM((SUB, 1), F32), pltpu.VMEM((SUB, 1), F32), pltpu.VMEM((SUB, MLA_KV_LORA), F32)],
        [q, knew] + [cache_mla] * PAGES_PER_STEP)


def _fox_dec_kernel(pt_ref, q_ref, knew_ref, vnew_ref, fnew_ref, dt_ref, *refs):
    P = PAGES_PER_STEP
    k_pages, v_pages, f_pages = refs[:P], refs[P:2 * P], refs[2 * P:3 * P]
    o_ref, fsum_ref, m_ref, l_ref, acc_ref = refs[3 * P:]
    j = pl.program_id(1)

    @pl.when(j == 0)
    def _():
        _softmax_reset(m_ref, l_ref, acc_ref)
        fsum_ref[...] = jnp.zeros(fsum_ref.shape, F32)

    qs = q_ref[...]
    carry = fsum_ref[...]
    pad = jnp.zeros((SUB - FOX_HEADS, PAGE_ROWS), F32)
    fks = []
    for p in f_pages:
        x = jnp.concatenate([p[...], pad], axis=0)
        fk = jnp.dot(x, dt_ref[...], preferred_element_type=F32, precision=HIGHEST) + carry
        carry = fk[:, PAGE_ROWS - 1:PAGE_ROWS]
        fks.append(fk)
    fsum_ref[...] = carry
    kt = jnp.concatenate([p[...] for p in k_pages], axis=1).astype(BF16)
    vt = jnp.concatenate([p[...] for p in v_pages], axis=1).astype(BF16)
    s = jnp.dot(qs, kt, preferred_element_type=F32) - jnp.concatenate(fks, axis=1)
    _softmax_step_t(s, vt, m_ref, l_ref, acc_ref)

    @pl.when(j == pl.num_programs(1) - 1)
    def _():
        m_ref[...] = m_ref[...] + (carry + fnew_ref[...][:, 0:1])
        s_new = jnp.sum(qs.astype(F32) * _bf16_round(knew_ref[...]), axis=-1, keepdims=True)
        o = _merge_new_key(s_new, _bf16_round(vnew_ref[...]), m_ref, l_ref, acc_ref)
        head = lax.broadcasted_iota(jnp.int32, (SUB, MIX_WIDTH), 1) // HEAD_DIM
        hrow = lax.broadcasted_iota(jnp.int32, (SUB, MIX_WIDTH), 0)
        o_ref[...] = jnp.sum(jnp.where(head == hrow, o, 0.0), axis=0, keepdims=True)


def _fox_dec(layer, page_table, cache_k, cache_v, cache_f, q, knew, vnew, fnew, dt):
    nb, n_pages = page_table.shape
    return _paged_call(
        _fox_dec_kernel, "fox_decode", page_table, n_pages // PAGES_PER_STEP,
        [_seq_spec(SUB, MIX_WIDTH), _seq_spec(1, MIX_WIDTH), _seq_spec(1, MIX_WIDTH), _seq_spec(SUB, LANE),
         _const_spec(dt)]
        + _page_specs(layer, MIX_WIDTH) + _page_specs(layer, MIX_WIDTH) + _page_specs(layer, FOX_HEADS),
        _seq_spec(1, MIX_WIDTH), jax.ShapeDtypeStruct((nb, 1, MIX_WIDTH), F32),
        [pltpu.VMEM((SUB, 1), F32), pltpu.VMEM((SUB, 1), F32), pltpu.VMEM((SUB, 1), F32),
         pltpu.VMEM((SUB, MIX_WIDTH), F32)],
        [q, knew, vnew, fnew, dt] + [cache_k] * PAGES_PER_STEP + [cache_v] * PAGES_PER_STEP
        + [cache_f] * PAGES_PER_STEP)


def _nsa_cmp_dec_kernel(pt_ref, q_ref, wt_ref, bias_ref, pair_ref, *refs):
    P = PAGES_PER_STEP
    pages = refs[:P]
    oc_ref, idx_ref, cb_ref = refs[P:]
    j = pl.program_id(1)
    per_page = PAGE_ROWS // NSA_CMP_BLOCK
    wk, wv = wt_ref[0].astype(BF16), wt_ref[1].astype(BF16)
    blocks = []
    for p in pages:
        x = p[...].astype(BF16)
        blocks.append(jnp.concatenate([_dot_nt(wk, x[:HEAD_DIM])[:per_page],
                                       _dot_nt(wv, x[HEAD_DIM:])[:per_page]], axis=1))
    rows = P * per_page
    cb_ref[pl.ds(pl.multiple_of(j * rows, rows), rows), :] = jnp.concatenate(blocks, axis=0)

    @pl.when(j == pl.num_programs(1) - 1)
    def _():
        qs = q_ref[...]
        cb = cb_ref[...].astype(BF16)
        lc = _dot_nt(qs, cb) + bias_ref[...]
        e = jnp.exp(lc - jnp.max(lc, axis=-1, keepdims=True))
        pc = e / jnp.maximum(jnp.sum(e, axis=-1, keepdims=True), 1e-30)
        oc_ref[...] = jnp.dot(pc.astype(BF16), cb, preferred_element_type=F32)
        psum = jnp.sum(pc[0:N_HEADS], axis=0, keepdims=True)
        imp = jnp.dot(jnp.broadcast_to(psum, (SUB, psum.shape[1])), pair_ref[...],
                      preferred_element_type=F32, precision=HIGHEST)
        n_sel = imp.shape[1]
        col = jnp.broadcast_to(imp.T[:, 0:1], (n_sel, n_sel))
        rowv = jnp.broadcast_to(imp[0:1, :], (n_sel, n_sel))
        i_idx = lax.broadcasted_iota(jnp.int32, (n_sel, n_sel), 0)
        j_idx = lax.broadcasted_iota(jnp.int32, (n_sel, n_sel), 1)
        ahead = jnp.where(col > rowv, 1.0, jnp.where(col == rowv, jnp.where(i_idx < j_idx, 1.0, 0.0), 0.0))
        rank = jnp.sum(ahead, axis=0, keepdims=True)
        r = lax.broadcasted_iota(jnp.int32, (2 * SUB, n_sel), 0).astype(F32)
        blk = lax.broadcasted_iota(jnp.int32, (2 * SUB, n_sel), 1).astype(F32)
        pick = jnp.sum(jnp.where(rank == r, blk, 0.0), axis=-1, keepdims=True)
        idx_ref[...] = jnp.broadcast_to(pick, (2 * SUB, LANE)).astype(jnp.int32)


def _nsa_cmp_dec(layer, page_table, cache_nsa, q, wt, bias, pair):
    nb, n_pages = page_table.shape
    n_cb = n_pages * PAGE_ROWS // NSA_CMP_BLOCK
    return _paged_call(
        _nsa_cmp_dec_kernel, "nsa_cmp_decode", page_table, n_pages // PAGES_PER_STEP,
        [_seq_spec(SUB, LANE), _const_spec(wt), _const_spec(bias), _const_spec(pair)]
        + _page_specs(layer, 2 * HEAD_DIM, 0),
        [_seq_spec(SUB, LANE), _seq_spec(2 * SUB, LANE)],
        [jax.ShapeDtypeStruct((nb, SUB, LANE), F32), jax.ShapeDtypeStruct((nb, 2 * SUB, LANE), jnp.int32)],
        [pltpu.VMEM((n_cb, LANE), F32)],
        [q, wt, bias, pair] + [cache_nsa] * PAGES_PER_STEP)


N_SEL_PAST = NSA_TOPK - 1


def _nsa_sel_dec_kernel(sel_ref, id_ref, q_ref, oc_ref, snew_ref, wnew_ref, gate_ref, win_ref, bsel_ref, bwin_ref,
                        b0_ref, *refs):
    blocks = refs[:N_SEL_PAST]
    o_ref = refs[N_SEL_PAST]
    b = pl.program_id(0)
    qs = q_ref[...][:, :HEAD_DIM]
    qf = qs.astype(F32)
    b0 = b0_ref[...]
    half_of_lane = lax.broadcasted_iota(jnp.int32, (SUB, PAGE_ROWS), 1) // NSA_SEL_BLOCK

    def attend(vts, s_list, new_ref):
        new = _bf16_round(new_ref[...])
        s_new = jnp.sum(qf * new[:, :HEAD_DIM], axis=-1, keepdims=True) + b0[:, 0:1]
        m = s_new
        for s in s_list:
            m = jnp.maximum(m, jnp.max(s, axis=-1, keepdims=True))
        p_new = jnp.exp(s_new - m)
        l, acc = p_new, p_new * new[:, HEAD_DIM:]
        for s, vt in zip(s_list, vts):
            p = jnp.exp(s - m)
            l = l + jnp.sum(p, axis=-1, keepdims=True)
            acc = acc + _dot_nt(p.astype(BF16), vt)
        return acc / l

    sel_s, sel_vt = [], []
    for r, ref in enumerate(blocks):
        x = ref[...].astype(BF16)
        blk = id_ref[b, r]
        s = jnp.dot(qs, x[:HEAD_DIM], preferred_element_type=F32) + bsel_ref[blk]
        sel_s.append(jnp.where(half_of_lane == blk % (PAGE_ROWS // NSA_SEL_BLOCK), s, NEG))
        sel_vt.append(x[HEAD_DIM:])
    o_s = attend(sel_vt, sel_s, snew_ref)
    win = win_ref[...].astype(BF16)
    o_w = attend([win[HEAD_DIM:]], [jnp.dot(qs, win[:HEAD_DIM], preferred_element_type=F32) + bwin_ref[...]],
                 wnew_ref)
    g = gate_ref[...]
    mixed = (g[0][:, :HEAD_DIM] * oc_ref[...][:, HEAD_DIM:] + g[1][:, :HEAD_DIM] * o_s
             + g[2][:, :HEAD_DIM] * o_w)
    o_ref[...] = jnp.concatenate([mixed[h:h + 1] for h in range(N_HEADS)], axis=1)


def _nsa_sel_dec(layer, sel_page, sel_id, cache_nsa, q, oc, snew, wnew, gates, win, bsel, bwin, b0):
    nb = q.shape[0]
    seq = lambda rows, width: pl.BlockSpec((None, rows, width), lambda b, sp, si: (b, 0, 0))
    const = lambda a: pl.BlockSpec(a.shape, lambda b, sp, si: (0,) * a.ndim)
    blk_specs = [pl.BlockSpec((None, None, 2 * HEAD_DIM, PAGE_ROWS), lambda b, sp, si, r=r: (layer, sp[b, r], 1, 0))
                 for r in range(N_SEL_PAST)]
    return pl.pallas_call(
        _nsa_sel_dec_kernel,
        grid_spec=pltpu.PrefetchScalarGridSpec(
            num_scalar_prefetch=2, grid=(nb,),
            in_specs=[seq(SUB, LANE), seq(SUB, LANE), seq(1, LANE), seq(1, LANE),
                      pl.BlockSpec((None, 3, SUB, LANE), lambda b, sp, si: (b, 0, 0, 0)),
                      pl.BlockSpec((None, None, 2 * HEAD_DIM, win.shape[3]), lambda b, sp, si: (layer, b, 0, 0)),
                      const(bsel), const(bwin), const(b0)] + blk_specs,
            out_specs=seq(1, MIX_WIDTH)),
        out_shape=jax.ShapeDtypeStruct((nb, 1, MIX_WIDTH), F32),
        compiler_params=pltpu.CompilerParams(dimension_semantics=("parallel",), vmem_limit_bytes=VMEM_LIMIT),
        name="nsa_sel_decode",
    )(sel_page, sel_id, q, oc, snew, wnew, gates, win, bsel, bwin, b0, *([cache_nsa] * N_SEL_PAST))


S5_CHUNK = 256
S5_WIDTH = S5_GROUPS * S5_STATE


def _s5_scan_kernel(u_ref, bbr_ref, bbi_ref, a_ref, cr_ref, ci_ref, d_ref, wg_ref, y_ref, st_ref,
                    xr_ref, xi_ref, sr_ref, si_ref):
    j = pl.program_id(0)

    @pl.when(j == 0)
    def _():
        sr_ref[...] = jnp.zeros(sr_ref.shape, F32)
        si_ref[...] = jnp.zeros(si_ref.shape, F32)

    u = u_ref[...]
    ub = u.astype(BF16)
    xr_ref[...] = jnp.dot(ub, bbr_ref[...], preferred_element_type=F32)
    xi_ref[...] = jnp.dot(ub, bbi_ref[...], preferred_element_type=F32)
    ar = jnp.broadcast_to(a_ref[0:1, :], (SUB, S5_WIDTH))
    ai = jnp.broadcast_to(a_ref[1:2, :], (SUB, S5_WIDTH))

    def step(t, carry):
        xr, xi = carry
        rows = pl.ds(pl.multiple_of(t * SUB, SUB), SUB)
        nr = ar * xr - ai * xi + xr_ref[rows, :]
        ni = ar * xi + ai * xr + xi_ref[rows, :]
        xr_ref[rows, :] = nr
        xi_ref[rows, :] = ni
        return nr, ni

    xr, xi = lax.fori_loop(0, S5_CHUNK, step, (sr_ref[...], si_ref[...]), unroll=4)
    sr_ref[...] = xr
    si_ref[...] = xi
    st_ref[0] = xr
    st_ref[1] = xi
    y = (jnp.dot(xr_ref[...].astype(BF16), cr_ref[...], preferred_element_type=F32)
         - jnp.dot(xi_ref[...].astype(BF16), ci_ref[...], preferred_element_type=F32)) + d_ref[...] * u
    y = jax.nn.gelu(y)
    y_ref[...] = y * jax.nn.sigmoid(jnp.dot(y.astype(BF16), wg_ref[...], preferred_element_type=F32))


def _s5_scan(u_tb, bbr, bbi, a, cr, ci, dvec, wg):
    n = u_tb.shape[0]
    rows = S5_CHUNK * SUB
    full = lambda x: pl.BlockSpec(x.shape, lambda j: (0,) * x.ndim)
    return pl.pallas_call(
        _s5_scan_kernel,
        grid=(n // rows,),
        in_specs=[pl.BlockSpec((rows, MIX_WIDTH), lambda j: (j, 0)), full(bbr), full(bbi), full(a), full(cr),
                  full(ci), full(dvec), full(wg)],
        out_specs=[pl.BlockSpec((rows, MIX_WIDTH), lambda j: (j, 0)),
                   pl.BlockSpec((2, SUB, S5_WIDTH), lambda j: (0, 0, 0))],
        out_shape=[jax.ShapeDtypeStruct((n, MIX_WIDTH), F32), jax.ShapeDtypeStruct((2, SUB, S5_WIDTH), F32)],
        scratch_shapes=[pltpu.VMEM((rows, S5_WIDTH), F32), pltpu.VMEM((rows, S5_WIDTH), F32),
                        pltpu.VMEM((SUB, S5_WIDTH), F32), pltpu.VMEM((SUB, S5_WIDTH), F32)],
        compiler_params=pltpu.CompilerParams(dimension_semantics=("arbitrary",), vmem_limit_bytes=VMEM_LIMIT),
        name="s5_scan",
    )(u_tb, bbr, bbi, a, cr, ci, dvec, wg)


def _s5_weights(w, l):
    lam_re, lam_im = w['s5_lambda_re'][l], w['s5_lambda_im'][l]
    dt = jnp.exp(w['s5_log_dt'][l])[:, None]
    mag = jnp.exp(lam_re * dt)
    a_re, a_im = mag * jnp.cos(lam_im * dt), mag * jnp.sin(lam_im * dt)
    den = lam_re * lam_re + lam_im * lam_im
    coef_re = ((a_re - 1.0) * lam_re + a_im * lam_im) / den
    coef_im = (a_im * lam_re - (a_re - 1.0) * lam_im) / den
    b_re, b_im = w['s5_b_re'][l], w['s5_b_im'][l]
    bb_re = coef_re[..., None] * b_re - coef_im[..., None] * b_im
    bb_im = coef_re[..., None] * b_im + coef_im[..., None] * b_re
    eye = jnp.eye(S5_GROUPS, dtype=F32)
    bd_in = lambda bb: jnp.einsum('gpi,gh->gihp', bb, eye).reshape(MIX_WIDTH, S5_WIDTH)
    bd_out = lambda c: jnp.einsum('gip,gh->gphi', c, eye).reshape(S5_WIDTH, MIX_WIDTH)
    a = jnp.stack([a_re.reshape(S5_WIDTH), a_im.reshape(S5_WIDTH)])
    return (bd_in(bb_re).astype(BF16), bd_in(bb_im).astype(BF16), a,
            bd_out(w['s5_c_re'][l]).astype(BF16), bd_out(w['s5_c_im'][l]).astype(BF16),
            w['s5_d'][l].reshape(1, MIX_WIDTH), w['s5_w_glu'][l].astype(BF16))


def _s5_prompt(u, w, l):
    B, T, _ = u.shape
    u_tb = jnp.pad(jnp.swapaxes(u, 0, 1), ((0, 0), (0, SUB - B), (0, 0))).reshape(T * SUB, MIX_WIDTH)
    y, st = _s5_scan(u_tb, *_s5_weights(w, l))
    y = jnp.swapaxes(y.reshape(T, SUB, MIX_WIDTH)[:, :B], 0, 1)
    return y, jnp.swapaxes(st[:, :B], 0, 1).reshape(B, 2, S5_GROUPS, S5_STATE)


def _rowmm_kernel(x_ref, w_ref, o_ref):
    o_ref[...] = jnp.dot(x_ref[...].astype(BF16), w_ref[...], preferred_element_type=F32)


def _rowmm(x, w):
    return pl.pallas_call(_rowmm_kernel, out_shape=jax.ShapeDtypeStruct((x.shape[0], w.shape[1]), F32),
                          name="row_matmul")(x, w)


def _rms_norm(x, g):
    xf = x.astype(F32)
    y = xf * lax.rsqrt(jnp.mean(xf * xf, axis=-1, keepdims=True) + EPS)
    return (y * g.astype(F32)).astype(x.dtype)


def _rope(x, pos):
    half = x.shape[-1] // 2
    freq = ROPE_THETA ** (-jnp.arange(half, dtype=F32) / half)
    ang = pos.astype(F32)[:, None] * freq[None, :]
    ang = ang.reshape(ang.shape[0], *([1] * (x.ndim - 3)), half)
    cos, sin = jnp.cos(ang), jnp.sin(ang)
    x1, x2 = x[..., :half], x[..., half:]
    return jnp.concatenate([x1 * cos - x2 * sin, x1 * sin + x2 * cos], axis=-1)


def _rel_bucket(dist):
    n = jnp.maximum(dist, 0)
    exact = REL_BUCKETS // 2
    large = exact + (jnp.log(jnp.maximum(n, 1).astype(F32) / exact)
                     / math.log(REL_MAX_DIST / exact) * (REL_BUCKETS - exact)).astype(jnp.int32)
    large = jnp.minimum(large, REL_BUCKETS - 1)
    return jnp.where(n < exact, n, large)


def _t5_bias(rel_bias, dist):
    b = rel_bias[_rel_bucket(dist)].astype(F32)
    return jnp.moveaxis(b, -1, -3)


def _masked_softmax(logits, mask):
    lg = jnp.where(mask, logits, NEG)
    m = jnp.max(lg, axis=-1, keepdims=True)
    e = jnp.where(mask, jnp.exp(lg - m), 0.0)
    return e / jnp.maximum(jnp.sum(e, axis=-1, keepdims=True), 1e-30)


def _attend(logits, values, spec):
    lg = logits[0] if len(logits) == 1 else jnp.concatenate(logits, axis=-1)
    p = jax.nn.softmax(lg, axis=-1)
    out, start = None, 0
    for l_, v_ in zip(logits, values):
        n = l_.shape[-1]
        o = jnp.einsum(spec, p[..., start:start + n], v_)
        out = o if out is None else out + o
        start += n
    return out


def _map_query_blocks(fn, arrays, qpos):
    T = qpos.shape[0]
    qb = Q_BLOCK if T % Q_BLOCK == 0 else T
    nb = T // qb

    def split(a):
        return jnp.moveaxis(a.reshape(a.shape[0], nb, qb, *a.shape[2:]), 1, 0)
    xs = tuple(split(a) for a in arrays) + (qpos.reshape(nb, qb),)
    out = lax.map(lambda args: fn(*args), xs)
    out = jnp.moveaxis(out, 0, 1)
    return out.reshape(out.shape[0], T, *out.shape[3:])


def _complex_affine_combine(e1, e2):
    a1r, a1i, b1r, b1i = e1
    a2r, a2i, b2r, b2i = e2
    return (a2r * a1r - a2i * a1i, a2r * a1i + a2i * a1r,
            a2r * b1r - a2i * b1i + b2r, a2r * b1i + a2i * b1r + b2i)


def _gather_past(l, cache_mla, cache_fox_k, cache_fox_v, cache_fox_logf, cache_nsa,
                 state_nsa_win, state_s5, page_table):
    nb, n_pages = page_table.shape

    def pages(cache):
        g = cache[l, page_table]
        return g.reshape(nb, n_pages * g.shape[2], *g.shape[3:])
    return {'mla': pages(cache_mla), 'fox_k': pages(cache_fox_k), 'fox_v': pages(cache_fox_v),
            'fox_logf': pages(cache_fox_logf), 'nsa': pages(cache_nsa),
            'win': state_nsa_win[l], 's5': state_s5[l]}


def _mla(q_lat, kv_lat, k_rope, pos, past, l, w):
    B, T, _ = q_lat.shape
    qn = _rms_norm(q_lat, w['mla_g_q'][l])
    q = jnp.einsum('btr,rhe->bthe', qn, w['mla_w_uq'][l])
    q_nope = q[..., :MLA_NOPE]
    q_rope = _rope(q[..., MLA_NOPE:], pos)
    c_new = _rms_norm(kv_lat, w['mla_g_kv'][l])
    kr_new = _rope(k_rope, pos)
    q_abs = jnp.einsum('bthn,lhn->bthl', q_nope, w['mla_w_uk'][l])
    segs = [(c_new, kr_new, pos)]
    if past is not None:
        n_past = past['mla'].shape[1]
        segs = [(past['mla'][..., :MLA_KV_LORA], past['mla'][..., MLA_KV_LORA:],
                 jnp.arange(n_past, dtype=jnp.int32))] + segs
    scale = (MLA_NOPE + MLA_ROPE) ** -0.5

    def block(qa, qr, qp):
        logits = []
        for c_, kr_, kp in segs:
            s = (jnp.einsum('bqhl,bkl->bhqk', qa, c_, preferred_element_type=F32)
                 + jnp.einsum('bqhr,bkr->bhqk', qr, kr_, preferred_element_type=F32)) * scale
            logits.append(jnp.where(kp[None, :] <= qp[:, None], s, NEG))
        return _attend(logits, [sg[0] for sg in segs], 'bhqk,bkl->bqhl')
    o_lat = _map_query_blocks(block, (q_abs, q_rope), pos)
    o = jnp.einsum('bthl,lhv->bthv', o_lat, w['mla_w_uv'][l]).reshape(B, T, MLA_HEADS * MLA_V)
    return o, jnp.concatenate([c_new, kr_new], axis=-1)


def _s5(u, past, l, w):
    B, T, _ = u.shape
    lam_re = w['s5_lambda_re'][l]
    lam_im = w['s5_lambda_im'][l]
    dt = jnp.exp(w['s5_log_dt'][l])[:, None]
    mag = jnp.exp(lam_re * dt)
    a_re, a_im = mag * jnp.cos(lam_im * dt), mag * jnp.sin(lam_im * dt)
    den = lam_re * lam_re + lam_im * lam_im
    coef_re = ((a_re - 1.0) * lam_re + a_im * lam_im) / den
    coef_im = (a_im * lam_re - (a_re - 1.0) * lam_im) / den
    b_re, b_im = w['s5_b_re'][l], w['s5_b_im'][l]
    bb_re = coef_re[..., None] * b_re - coef_im[..., None] * b_im
    bb_im = coef_re[..., None] * b_im + coef_im[..., None] * b_re
    ug = u.reshape(B, T, S5_GROUPS, S5_GROUP)
    bu_re = jnp.einsum('btgi,gpi->btgp', ug, bb_re)
    bu_im = jnp.einsum('btgi,gpi->btgp', ug, bb_im)
    if past is not None:
        x0_re, x0_im = past['s5'][:, 0], past['s5'][:, 1]
        bu_re = bu_re.at[:, 0].add(a_re * x0_re - a_im * x0_im)
        bu_im = bu_im.at[:, 0].add(a_re * x0_im + a_im * x0_re)
    A_re = jnp.broadcast_to(a_re, bu_re.shape)
    A_im = jnp.broadcast_to(a_im, bu_im.shape)
    _, _, x_re, x_im = lax.associative_scan(_complex_affine_combine, (A_re, A_im, bu_re, bu_im), axis=1)
    y = (jnp.einsum('btgp,gip->btgi', x_re, w['s5_c_re'][l])
         - jnp.einsum('btgp,gip->btgi', x_im, w['s5_c_im'][l]))
    y = y.reshape(B, T, MIX_WIDTH) + w['s5_d'][l] * u
    y = jax.nn.gelu(y)
    y = y * jax.nn.sigmoid(jnp.einsum('btw,wv->btv', y, w['s5_w_glu'][l]))
    state = jnp.stack([x_re[:, -1], x_im[:, -1]], axis=1)
    return y, state


def _fox(q, k, v, f_logit, pos, past, l, w):
    B, T, _ = q.shape
    q = q.reshape(B, T, FOX_HEADS, HEAD_DIM)
    k = k.reshape(B, T, FOX_HEADS, HEAD_DIM)
    v = v.reshape(B, T, FOX_HEADS, HEAD_DIM)
    logf = jax.nn.log_sigmoid(f_logit + w['fox_b_f'][l])
    if past is None:
        F_new = jnp.cumsum(logf, axis=1)
        segs = [(k, v, F_new, pos)]
    else:
        F_past = jnp.cumsum(past['fox_logf'], axis=1)
        F_new = F_past[:, -1:] + jnp.cumsum(logf, axis=1)
        segs = [(past['fox_k'], past['fox_v'], F_past, jnp.arange(F_past.shape[1], dtype=jnp.int32)),
                (k, v, F_new, pos)]
    scale = HEAD_DIM ** -0.5

    def block(qq, fq, qp):
        fq_t = jnp.swapaxes(fq, 1, 2)[..., None]
        logits = []
        for k_, _, fk, kp in segs:
            s = (jnp.einsum('bqhd,bkhd->bhqk', qq, k_, preferred_element_type=F32) * scale
                 + (fq_t - jnp.swapaxes(fk, 1, 2)[:, :, None, :]))
            logits.append(jnp.where(kp[None, :] <= qp[:, None], s, NEG))
        return _attend(logits, [sg[1] for sg in segs], 'bhqk,bkhd->bqhd')
    o = _map_query_blocks(block, (q, F_new), pos)
    return o.reshape(B, T, MIX_WIDTH), k, v, logf


def _nsa(q, kc, vc, ks, vs, kw, vw, g_logit, pos, past, l, w):
    B, T, _ = q.shape
    q = q.reshape(B, T, NSA_HEADS, HEAD_DIM)
    g = jax.nn.sigmoid(g_logit).reshape(B, T, 3, NSA_HEADS)
    rows = jnp.stack([kc, vc, ks, vs], axis=2)
    win_rows = jnp.stack([kw, vw], axis=2)
    if past is None:
        kc_all, vc_all, ks_all, vs_all = kc, vc, ks, vs
        win_ctx = win_rows
        keep = min(NSA_WINDOW, T)
    else:
        pr = past['nsa']
        kc_all = jnp.concatenate([pr[:, :, 0], kc], axis=1)
        vc_all = jnp.concatenate([pr[:, :, 1], vc], axis=1)
        ks_all = jnp.concatenate([pr[:, :, 2], ks], axis=1)
        vs_all = jnp.concatenate([pr[:, :, 3], vs], axis=1)
        win_ctx = jnp.concatenate([past['win'], win_rows], axis=1)
        keep = past['win'].shape[1]
    n_keys = kc_all.shape[1]
    win_base = n_keys - win_ctx.shape[1]
    n_cb = -(-n_keys // NSA_CMP_BLOCK)
    n_sb = -(-n_keys // NSA_SEL_BLOCK)
    ratio = NSA_SEL_BLOCK // NSA_CMP_BLOCK
    k_sel = min(NSA_TOPK, n_sb)

    def pad_rows(a, n):
        return jnp.pad(a, ((0, 0), (0, n - a.shape[1]), (0, 0)))
    kcb = jnp.einsum('bnid,i->bnd', pad_rows(kc_all, n_cb * NSA_CMP_BLOCK).reshape(B, n_cb, NSA_CMP_BLOCK, HEAD_DIM),
                     w['nsa_w_cmp_k'][l])
    vcb = jnp.einsum('bnid,i->bnd', pad_rows(vc_all, n_cb * NSA_CMP_BLOCK).reshape(B, n_cb, NSA_CMP_BLOCK, HEAD_DIM),
                     w['nsa_w_cmp_v'][l])
    cb_end = jnp.arange(n_cb, dtype=jnp.int32) * NSA_CMP_BLOCK + (NSA_CMP_BLOCK - 1)
    sb_start = jnp.arange(n_sb, dtype=jnp.int32) * NSA_SEL_BLOCK
    ks_pad = pad_rows(ks_all, n_sb * NSA_SEL_BLOCK)
    vs_pad = pad_rows(vs_all, n_sb * NSA_SEL_BLOCK)
    win_pad = jnp.pad(win_ctx, ((0, 0), (NSA_WINDOW, 0), (0, 0), (0, 0)))
    rel_bias = w['rel_bias']
    scale = HEAD_DIM ** -0.5

    def block(qq, gb, qp):
        nq = qp.shape[0]
        lc = (jnp.einsum('bqhd,bnd->bhqn', qq, kcb, preferred_element_type=F32) * scale
              + _t5_bias(rel_bias, qp[:, None] - cb_end[None, :]))
        pc = _masked_softmax(lc, cb_end[None, :] <= qp[:, None])
        o_c = jnp.einsum('bhqn,bnd->bqhd', pc, vcb)
        imp = jnp.pad(pc.sum(axis=1), ((0, 0), (0, 0), (0, n_sb * ratio - n_cb)))
        imp = imp.reshape(B, nq, n_sb, ratio).sum(-1)
        cur = (qp[:, None] // NSA_SEL_BLOCK) == jnp.arange(n_sb, dtype=jnp.int32)[None, :]
        score = jnp.where(cur, FORCE, jnp.where(sb_start[None, :] <= qp[:, None], imp, NEG))
        top_v, top_i = lax.top_k(score, k_sel)
        idx = (top_i[..., None] * NSA_SEL_BLOCK + jnp.arange(NSA_SEL_BLOCK, dtype=jnp.int32)).reshape(
            B, nq, k_sel * NSA_SEL_BLOCK)
        valid = jnp.repeat(top_v > NEG / 2, NSA_SEL_BLOCK, axis=-1) & (idx <= qp[None, :, None])
        ks_g = jax.vmap(lambda a, i: a[i])(ks_pad, idx)
        vs_g = jax.vmap(lambda a, i: a[i])(vs_pad, idx)
        ls = (jnp.einsum('bqhd,bqkd->bhqk', qq, ks_g, preferred_element_type=F32) * scale
              + _t5_bias(rel_bias, qp[None, :, None] - idx))
        ps = _masked_softmax(ls, valid[:, None])
        o_s = jnp.einsum('bhqk,bqkd->bqhd', ps, vs_g)
        start = qp[0] - win_base
        wk = lax.dynamic_slice_in_dim(win_pad, start, NSA_WINDOW + nq, axis=1)
        wp = qp[0] - NSA_WINDOW + jnp.arange(NSA_WINDOW + nq, dtype=jnp.int32)
        mask_w = ((wp[None, :] >= 0) & (wp[None, :] <= qp[:, None])
                  & (qp[:, None] - wp[None, :] <= NSA_WINDOW))
        lw = (jnp.einsum('bqhd,bkd->bhqk', qq, wk[:, :, 0], preferred_element_type=F32) * scale
              + _t5_bias(rel_bias, qp[:, None] - wp[None, :]))
        pw = jax.nn.softmax(jnp.where(mask_w, lw, NEG), axis=-1)
        o_w = jnp.einsum('bhqk,bkd->bqhd', pw, wk[:, :, 1])
        return (gb[:, :, 0, :, None] * o_c + gb[:, :, 1, :, None] * o_s
                + gb[:, :, 2, :, None] * o_w)
    o = _map_query_blocks(block, (q, g), pos)
    return o.reshape(B, T, MIX_WIDTH), rows, win_ctx[:, win_ctx.shape[1] - keep:]


def _zcol(z, B, T, start, width):
    return z[:, start:start + width].reshape(B, T, width)


def _mixers_prompt(z, B, T, l, w, wc):
    col = functools.partial(_zcol, z, B, T)
    qcat, kf, kb = _mla_prep(z, wc['rope_c'], wc['rope_s'], wc['mla_gq'][l], wc['mla_gkv'][l],
                             wc['mla_w1'][l], wc['mla_wuk'][l], T, 512)
    y_a = _mla_attn(qcat, kb, wc['mla_wuv'][l], B, T)
    mla_rows = kf[:, :MLA_KV_LORA + MLA_ROPE].reshape(B, T, MLA_KV_LORA + MLA_ROPE)

    logf = jax.nn.log_sigmoid(col(Z_NG + 3 * NSA_HEADS, FOX_HEADS) + w['fox_b_f'][l])
    fsum = jnp.cumsum(logf, axis=1)
    fq = jnp.pad(fsum.reshape(B * T, FOX_HEADS), ((0, 0), (0, LANE - FOX_HEADS)))
    fk = jnp.pad(jnp.swapaxes(fsum.reshape(B, T // TQ, TQ, FOX_HEADS), 2, 3),
                 ((0, 0), (0, 0), (0, 8 - FOX_HEADS), (0, 0)))
    y_c = _fox_attn(z, fq, fk, B, T)
    fox_k = col(Z_FK, MIX_WIDTH).reshape(B, T, FOX_HEADS, HEAD_DIM)
    fox_v = col(Z_FV, MIX_WIDTH).reshape(B, T, FOX_HEADS, HEAD_DIM)

    y_d = _nsa_attn(z, wc['nsa_wkv'][l], wc['nsa_tb'], wc['nsa_bc'], B, T)
    nsa_rows = col(Z_KCVC, 2 * LANE).reshape(B, T, 4, HEAD_DIM)
    keep = min(NSA_WINDOW, T)
    win_state = col(Z_KWVW, LANE)[:, T - keep:].reshape(B, keep, 2, HEAD_DIM)

    y_b, s5_state = _s5_prompt(col(Z_S5U, MIX_WIDTH), w, l)
    return y_a, y_b, y_c, y_d, (mla_rows, fox_k, fox_v, logf, nsa_rows, win_state, s5_state)


def _head_rows(a):
    return jnp.pad(a, ((0, 0), (0, SUB - a.shape[1])) + ((0, 0),) * (a.ndim - 2))


def _mixers_decode(z, l, w, wc, caches):
    (cache_mla, cache_fox_k, cache_fox_v, cache_fox_logf, cache_nsa, state_nsa_win, state_s5, page_table) = caches
    nb, n_pages = page_table.shape
    pool = cache_mla.shape[1]
    assert cache_mla.shape[2] == PAGE_ROWS and n_pages % PAGES_PER_STEP == 0
    dec = wc['dec']
    col = lambda start, width: z[:, start:start + width]

    qcat, kf, _ = _mla_prep(z, dec['rope_c'], dec['rope_s'], wc['mla_gq'][l], wc['mla_gkv'][l],
                            wc['mla_w1'][l], wc['mla_wuk'][l], nb, nb)
    def feat_major(c):
        c = jnp.moveaxis(c, 2, -1)
        return c.reshape(c.shape[0], c.shape[1], -1, c.shape[-1])
    o_lat = _mla_dec(l, page_table, feat_major(cache_mla), _head_rows(qcat.reshape(nb, N_HEADS, MLA_QW)),
                     kf.reshape(nb, 1, MLA_QW))
    y_a = _rowmm(o_lat[:, :N_HEADS].reshape(nb, N_HEADS * MLA_KV_LORA),
                 wc['mla_wuv'][l].reshape(N_HEADS * MLA_KV_LORA, MIX_WIDTH))
    mla_rows = kf[:, :MLA_KV_LORA + MLA_ROPE].reshape(nb, 1, MLA_KV_LORA + MLA_ROPE)

    head_of_lane = jnp.arange(MIX_WIDTH, dtype=jnp.int32) // HEAD_DIM
    fq = col(Z_FQ, MIX_WIDTH) * SCALE_64
    fq = jnp.where(head_of_lane[None, None, :] == jnp.arange(N_HEADS, dtype=jnp.int32)[None, :, None],
                   fq[:, None, :], 0.0)
    logf = jax.nn.log_sigmoid(col(Z_NG + 3 * NSA_HEADS, FOX_HEADS) + w['fox_b_f'][l])
    fnew = jnp.broadcast_to(_head_rows(logf[:, :, None]), (nb, SUB, LANE))
    y_c = _fox_dec(l, page_table, feat_major(cache_fox_k), feat_major(cache_fox_v), feat_major(cache_fox_logf),
                   _head_rows(fq).astype(BF16), col(Z_FK, MIX_WIDTH).reshape(nb, 1, MIX_WIDTH),
                   col(Z_FV, MIX_WIDTH).reshape(nb, 1, MIX_WIDTH), fnew, dec['fox_dt'])
    fox_k = col(Z_FK, MIX_WIDTH).reshape(nb, 1, FOX_HEADS, HEAD_DIM)
    fox_v = col(Z_FV, MIX_WIDTH).reshape(nb, 1, FOX_HEADS, HEAD_DIM)

    nq = (col(Z_NQ, MIX_WIDTH) * SCALE_64).reshape(nb, N_HEADS, HEAD_DIM)
    nq = _head_rows(jnp.pad(nq, ((0, 0), (0, 0), (0, LANE - HEAD_DIM)))).astype(BF16)
    nsa_t = feat_major(cache_nsa)
    key = jnp.arange(PAGE_ROWS, dtype=jnp.int32)
    in_block = key[None, :] // NSA_CMP_BLOCK == jnp.arange(SUB, dtype=jnp.int32)[:, None]
    wt = jnp.stack([jnp.where(in_block, w[name][l][key % NSA_CMP_BLOCK][None, :], 0.0)
                    for name in ('nsa_w_cmp_k', 'nsa_w_cmp_v')])
    o_c, idx = _nsa_cmp_dec(l, page_table, nsa_t, nq, wt, dec['nsa_bc'], dec['nsa_pair'])
    sel_id = idx[:, :N_SEL_PAST, 0]
    sel_page = jnp.take_along_axis(page_table, sel_id // (PAGE_ROWS // NSA_SEL_BLOCK), axis=1)
    gates = jax.nn.sigmoid(col(Z_NG, 3 * NSA_HEADS)).reshape(nb, 3, NSA_HEADS, 1)
    gates = jnp.broadcast_to(jnp.pad(gates, ((0, 0), (0, 0), (0, SUB - NSA_HEADS), (0, 0))), (nb, 3, SUB, LANE))
    wnew = col(Z_KWVW, LANE)
    win_past = feat_major(state_nsa_win)
    y_d = _nsa_sel_dec(l, sel_page, sel_id, nsa_t, nq, o_c, col(Z_KSVS, LANE).reshape(nb, 1, LANE), wnew.reshape(nb, 1, LANE), gates, win_past,
                       dec['nsa_bsel'], dec['nsa_bwin'], dec['nsa_b0'])
    nsa_rows = col(Z_KCVC, 2 * LANE).reshape(nb, 1, 4, HEAD_DIM)
    win_state = jnp.concatenate([state_nsa_win[l][:, 1:], wnew.reshape(nb, 1, 2, HEAD_DIM)], axis=1)

    y_b, s5_state = _s5(col(Z_S5U, MIX_WIDTH).reshape(nb, 1, MIX_WIDTH), {'s5': state_s5[l]}, l, w)
    return y_a, y_b, y_c, y_d, (mla_rows, fox_k, fox_v, logf.reshape(nb, 1, FOX_HEADS), nsa_rows, win_state,
                                s5_state)


def _decode_tables(rel_bias, past_len, nb):
    half = MLA_ROPE // 2
    freq = ROPE_THETA ** (-jnp.arange(half, dtype=F32) / half)
    ang = jnp.full((nb, 1), past_len, jnp.int32).astype(F32) * freq[None, :]
    cos, sin = jnp.cos(ang), jnp.sin(ang)
    lane_pad = ((0, 0), (0, LANE - MLA_ROPE))

    def tab(dist):
        oh = jax.nn.one_hot(_rel_bucket(dist), REL_BUCKETS, dtype=F32)
        return _head_rows(jnp.einsum('rcb,bh->hrc', oh, rel_bias.astype(F32), precision=HIGHEST)[None])[0]
    n_cb = past_len // NSA_CMP_BLOCK
    n_sb = past_len // NSA_SEL_BLOCK
    cb_end = jnp.arange(n_cb, dtype=jnp.int32) * NSA_CMP_BLOCK + (NSA_CMP_BLOCK - 1)
    key_pos = (jnp.arange(n_sb, dtype=jnp.int32)[:, None] * NSA_SEL_BLOCK
               + jnp.arange(NSA_SEL_BLOCK, dtype=jnp.int32)[None, :])
    win_pos = past_len - NSA_WINDOW + jnp.arange(NSA_WINDOW, dtype=jnp.int32)
    j = np.arange(PAGE_ROWS)[:, None]
    k = np.arange(PAGE_ROWS)[None, :]
    r = np.arange(n_cb)[:, None] // (NSA_SEL_BLOCK // NSA_CMP_BLOCK)
    c = np.arange(n_sb)[None, :]
    return {'rope_c': jnp.pad(jnp.concatenate([cos, cos], axis=-1), lane_pad),
            'rope_s': jnp.pad(jnp.concatenate([-sin, sin], axis=-1), lane_pad),
            'fox_dt': jnp.asarray((j <= k).astype(np.float32)),
            'nsa_pair': jnp.asarray((r == c).astype(np.float32)),
            'nsa_bc': tab((past_len - cb_end)[None, :])[:, 0, :],
            'nsa_bsel': jnp.tile(jnp.moveaxis(tab(past_len - key_pos), 0, 1), (1, 1, PAGE_ROWS // NSA_SEL_BLOCK)),
            'nsa_bwin': tab((past_len - win_pos)[None, :])[:, 0, :],
            'nsa_b0': jnp.broadcast_to(tab(jnp.zeros((1, 1), jnp.int32))[:, 0, :], (SUB, LANE))}


def _prep_weights(w_in, mla_g_q, mla_g_kv, mla_w_uq, mla_w_uk, mla_w_uv, nsa_w_cmp_k, nsa_w_cmp_v, rel_bias, T):
    def cols(a, b):
        return w_in[:, :, a:b]

    def zeros(n):
        return jnp.zeros((DEPTH, D_MODEL, n), w_in.dtype)
    o = [0] + [int(v) for v in np.cumsum(IN_SPLITS)]
    (o_ql, o_kv, o_kr, o_s5, o_fq, o_fk, o_fv, o_ff, o_nq, o_kc, o_vc, o_ks, o_vs, o_kw, o_vw, o_ng, o_gate,
     o_end) = o
    half = MLA_ROPE // 2
    w_in_r = jnp.concatenate([
        cols(o_gate, o_end), cols(o_fq, o_fk), cols(o_fk, o_fv), cols(o_fv, o_ff), cols(o_nq, o_kc),
        cols(o_s5, o_fq), cols(o_ql, o_kv), zeros(2 * LANE - MLA_Q_LORA), cols(o_kv, o_kr),
        cols(o_kr, o_s5), zeros(LANE - MLA_ROPE),
        cols(o_kr + half, o_s5), cols(o_kr, o_kr + half), zeros(LANE - MLA_ROPE),
        cols(o_kc, o_ks), cols(o_ks, o_kw), cols(o_kw, o_ng), cols(o_ng, o_gate), cols(o_ff, o_nq),
        zeros(LANE - 3 * NSA_HEADS - FOX_HEADS)], axis=-1)
    assert w_in_r.shape[-1] == Z_COLS

    def rope_cols(r):
        return jnp.pad(r, ((0, 0), (0, 0), (0, 0), (0, LANE - MLA_ROPE))).reshape(DEPTH, MLA_Q_LORA, N_HEADS * LANE)
    nope = mla_w_uq[..., :MLA_NOPE].reshape(DEPTH, MLA_Q_LORA, MIX_WIDTH)
    rope = mla_w_uq[..., MLA_NOPE:]
    rope_sw = jnp.concatenate([rope[..., half:], rope[..., :half]], axis=-1)
    w1 = jnp.concatenate([nope, rope_cols(rope), rope_cols(rope_sw)], axis=-1)
    w1 = jnp.pad(w1, ((0, 0), (0, 2 * LANE - MLA_Q_LORA), (0, 0)))
    eye = jnp.eye(N_HEADS, dtype=mla_w_uk.dtype)
    wuk = jnp.einsum('dlhn,hg->dhngl', mla_w_uk, eye).reshape(DEPTH, MIX_WIDTH, N_HEADS * MLA_KV_LORA)
    wuv = jnp.einsum('dlhv,hg->dhlgv', mla_w_uv, eye).reshape(DEPTH, N_HEADS, MLA_KV_LORA, MIX_WIDTH)

    freq = ROPE_THETA ** (-jnp.arange(half, dtype=F32) / half)
    ang = jnp.arange(T, dtype=jnp.int32).astype(F32)[:, None] * freq[None, :]
    cos, sin = jnp.cos(ang), jnp.sin(ang)
    lane_pad = ((0, 0), (0, LANE - MLA_ROPE))
    wkv = jnp.concatenate([jnp.broadcast_to(nsa_w_cmp_k[:, :, None], (DEPTH, NSA_CMP_BLOCK, HEAD_DIM)),
                           jnp.broadcast_to(nsa_w_cmp_v[:, :, None], (DEPTH, NSA_CMP_BLOCK, HEAD_DIM))], axis=-1)
    tb, bc = _nsa_tables(rel_bias, T)
    return {'w_in': w_in_r.astype(BF16),
            'mla_w1': w1.astype(BF16), 'mla_wuk': wuk.astype(BF16), 'mla_wuv': wuv.astype(BF16),
            'mla_gq': jnp.pad(mla_g_q, ((0, 0), (0, 2 * LANE - MLA_Q_LORA))).reshape(DEPTH, 1, 2 * LANE),
            'mla_gkv': mla_g_kv.reshape(DEPTH, 1, MLA_KV_LORA),
            'rope_c': jnp.pad(jnp.concatenate([cos, cos], axis=-1), lane_pad),
            'rope_s': jnp.pad(jnp.concatenate([-sin, sin], axis=-1), lane_pad),
            'nsa_wkv': wkv.astype(F32), 'nsa_tb': tb, 'nsa_bc': bc}


def _forward(x, mods, pos, w, wc, caches, tm):
    B, T, _ = x.shape
    n = B * T
    xf = x.reshape(n, D_MODEL)
    per_row = T == 1
    outs = [[] for _ in range(7)]
    for l in range(DEPTH):
        m6 = mods[l].reshape(B, 6, D_MODEL)
        if per_row:
            sh1, sc1, g1, sh2, sc2, g2 = (m6[:, i].reshape(1, n, D_MODEL) for i in range(6))
        else:
            sh1, sc1, g1, sh2, sc2, g2 = (m6[:, i].reshape(B, 1, D_MODEL) for i in range(6))
        gains = w['norm_gains'][l]
        z = _inproj(xf, gains[0], sc1, sh1, wc['w_in'][l], tm)
        if caches is None:
            y_a, y_b, y_c, y_d, states = _mixers_prompt(z, B, T, l, w, wc)
        else:
            y_a, y_b, y_c, y_d, states = _mixers_decode(z, l, w, wc, caches)
        mla_rows, fox_k, fox_v, fox_logf, nsa_rows, win_state, s5_state = states
        ys = tuple(y.reshape(n, MIX_WIDTH) for y in (y_a, y_b, y_c, y_d))
        xf = _merge(ys, z, xf, wc['w_branch'][l], wc['w_out'][l], gains[1], g1, tm)
        xf = _ffn(xf, gains[2], sc2, sh2, wc['w_ffn_gate'][l], wc['w_ffn_up'][l], wc['w_ffn_down'][l],
                  gains[3], g2, tm)
        for o, s in zip(outs, (mla_rows, fox_k, fox_v, fox_logf, nsa_rows, win_state, s5_state)):
            o.append(s)
    return xf.reshape(B, T, D_MODEL), tuple(jnp.stack(o) for o in outs)


def kernel(x_prompt, x_sample, c_prompt, c_sample, cache_mla, cache_fox_k, cache_fox_v, cache_fox_logf, cache_nsa, state_nsa_win, state_s5, page_table, w_ada, b_ada, norm_gains, w_in, mla_g_q, mla_g_kv, mla_w_uq, mla_w_uk, mla_w_uv, s5_lambda_re, s5_lambda_im, s5_log_dt, s5_b_re, s5_b_im, s5_c_re, s5_c_im, s5_d, s5_w_glu, fox_b_f, nsa_w_cmp_k, nsa_w_cmp_v, rel_bias, w_branch, w_out, w_ffn_gate, w_ffn_up, w_ffn_down):
    w = {'norm_gains': norm_gains,
         'mla_g_q': mla_g_q, 'mla_g_kv': mla_g_kv, 'mla_w_uq': mla_w_uq, 'mla_w_uk': mla_w_uk,
         'mla_w_uv': mla_w_uv, 's5_lambda_re': s5_lambda_re, 's5_lambda_im': s5_lambda_im,
         's5_log_dt': s5_log_dt, 's5_b_re': s5_b_re, 's5_b_im': s5_b_im, 's5_c_re': s5_c_re,
         's5_c_im': s5_c_im, 's5_d': s5_d, 's5_w_glu': s5_w_glu, 'fox_b_f': fox_b_f,
         'nsa_w_cmp_k': nsa_w_cmp_k, 'nsa_w_cmp_v': nsa_w_cmp_v, 'rel_bias': rel_bias}
    wc = _prep_weights(w_in, mla_g_q, mla_g_kv, mla_w_uq, mla_w_uk, mla_w_uv, nsa_w_cmp_k, nsa_w_cmp_v, rel_bias,
                       x_prompt.shape[1])
    wc.update({'w_branch': w_branch.astype(BF16), 'w_out': w_out.astype(BF16),
               'w_ffn_gate': w_ffn_gate.astype(BF16), 'w_ffn_up': w_ffn_up.astype(BF16),
               'w_ffn_down': w_ffn_down.astype(BF16)})
    wc['dec'] = _decode_tables(rel_bias, page_table.shape[1] * cache_mla.shape[2], x_sample.shape[0])
    nb_p = c_prompt.shape[0]
    mods = _ada(jnp.concatenate([c_prompt, c_sample], axis=0), w_ada, b_ada)
    past_len = page_table.shape[1] * cache_mla.shape[2]
    pos_p = jnp.arange(x_prompt.shape[1], dtype=jnp.int32)
    pos_s = past_len + jnp.arange(x_sample.shape[1], dtype=jnp.int32)
    caches = (cache_mla, cache_fox_k, cache_fox_v, cache_fox_logf, cache_nsa,
              state_nsa_win, state_s5, page_table)
    y_prompt, st_p = _forward(x_prompt, mods[:, :nb_p], pos_p, w, wc, None, 512)
    y_sample, st_s = _forward(x_sample, mods[:, nb_p:], pos_s, w, wc, caches, x_sample.shape[0])
    mla_p, fox_k_p, fox_v_p, fox_logf_p, nsa_p, win_p, s5_p = st_p
    mla_s, fox_k_s, fox_v_s, fox_logf_s, nsa_s, win_s, s5_s = st_s
    return (y_prompt, y_sample, mla_p, mla_s, fox_k_p, fox_k_s, fox_v_p, fox_v_s,
            fox_logf_p, fox_logf_s, nsa_p, nsa_s, win_p, win_s, s5_p, s5_s)
```

```python
import functools
import math

import numpy as np
import jax
import jax.numpy as jnp
from jax import lax
from jax.experimental import pallas as pl
from jax.experimental.pallas import tpu as pltpu

F32 = jnp.float32
BF16 = jnp.bfloat16

D_MODEL = 1024
DEPTH = 2
HEAD_DIM = 64
MIX_WIDTH = D_MODEL // 4
N_BRANCH = 4
Q_BLOCK = 128
PAGE_ROWS = 128
EPS = 1e-6
NEG = -1e30
FORCE = 1e30

MLA_HEADS = MIX_WIDTH // HEAD_DIM
MLA_Q_LORA = 3 * D_MODEL // 16
MLA_KV_LORA = D_MODEL // 8
MLA_NOPE = HEAD_DIM
MLA_ROPE = HEAD_DIM // 2
MLA_V = HEAD_DIM
ROPE_THETA = 10000.0

S5_GROUP = 16
S5_GROUPS = MIX_WIDTH // S5_GROUP
S5_STATE = 64

FOX_HEADS = MIX_WIDTH // HEAD_DIM

NSA_HEADS = MIX_WIDTH // HEAD_DIM
NSA_CMP_BLOCK = 32
NSA_SEL_BLOCK = 64
NSA_TOPK = 16
NSA_WINDOW = 512

REL_BUCKETS = 32
REL_MAX_DIST = 128

FFN_HIDDEN = (-(-8 * D_MODEL // 3) + 255) // 256 * 256

IN_SPLITS = (MLA_Q_LORA, MLA_KV_LORA, MLA_ROPE,
             MIX_WIDTH,
             MIX_WIDTH, MIX_WIDTH, MIX_WIDTH, FOX_HEADS,
             MIX_WIDTH, HEAD_DIM, HEAD_DIM, HEAD_DIM, HEAD_DIM, HEAD_DIM, HEAD_DIM, 3 * NSA_HEADS,
             N_BRANCH * D_MODEL)
IN_COLS = sum(IN_SPLITS)
GATE_COLS = N_BRANCH * D_MODEL
MIX_COLS = IN_COLS - GATE_COLS
MIX_SPLIT_POINTS = tuple(int(v) for v in np.cumsum(IN_SPLITS[:-1])[:-1])

LANE = 128
VMEM_LIMIT = 48 * 1024 * 1024

Z_GATE = 0
Z_FQ = Z_GATE + GATE_COLS
Z_FK = Z_FQ + MIX_WIDTH
Z_FV = Z_FK + MIX_WIDTH
Z_NQ = Z_FV + MIX_WIDTH
Z_S5U = Z_NQ + MIX_WIDTH
Z_QLAT = Z_S5U + MIX_WIDTH
Z_KVLAT = Z_QLAT + 2 * LANE
Z_KROPE = Z_KVLAT + LANE
Z_KROPE_SW = Z_KROPE + LANE
Z_KCVC = Z_KROPE_SW + LANE
Z_KSVS = Z_KCVC + LANE
Z_KWVW = Z_KSVS + LANE
Z_NG = Z_KWVW + LANE
Z_COLS = Z_NG + LANE
Z_TILE = Z_COLS // 3


def _rms(x, g):
    return x * lax.rsqrt(jnp.mean(x * x, axis=-1, keepdims=True) + EPS) * g


def _ada_kernel(c_ref, w_ref, b_ref, o_ref):
    c = c_ref[...]
    a = (c * jax.nn.sigmoid(c)).astype(BF16)
    o_ref[0] = jnp.dot(a, w_ref[0], preferred_element_type=F32) + b_ref[0]


def _ada(c_all, w_ada, b_ada):
    rows = c_all.shape[0]
    tn = 1536
    return pl.pallas_call(
        _ada_kernel,
        grid=(DEPTH, 6 * D_MODEL // tn),
        in_specs=[pl.BlockSpec((rows, D_MODEL), lambda l, j: (0, 0)),
                  pl.BlockSpec((1, D_MODEL, tn), lambda l, j: (l, 0, j)),
                  pl.BlockSpec((1, 1, tn), lambda l, j: (l, 0, j))],
        out_specs=pl.BlockSpec((1, rows, tn), lambda l, j: (l, 0, j)),
        out_shape=jax.ShapeDtypeStruct((DEPTH, rows, 6 * D_MODEL), F32),
        compiler_params=pltpu.CompilerParams(dimension_semantics=("arbitrary", "arbitrary"),
                                             vmem_limit_bytes=VMEM_LIMIT),
        name="ada_mod",
    )(c_all, w_ada.astype(BF16), b_ada.reshape(DEPTH, 1, 6 * D_MODEL))


def _inproj_kernel(x_ref, gain_ref, sc_ref, sh_ref, w_ref, o_ref, h_ref):
    @pl.when(pl.program_id(1) == 0)
    def _():
        h = _rms(x_ref[...], gain_ref[...]) * (1.0 + sc_ref[0]) + sh_ref[0]
        h_ref[...] = h.astype(BF16)
    o_ref[...] = jnp.dot(h_ref[...], w_ref[...], preferred_element_type=F32)


def _inproj(x, gain, sc, sh, w, tm):
    n = x.shape[0]
    tiles_per_group = n // tm // sc.shape[0]
    tn = Z_TILE
    mod_spec = pl.BlockSpec((1, sc.shape[1], D_MODEL), lambda i, j: (i // tiles_per_group, 0, 0))
    return pl.pallas_call(
        _inproj_kernel,
        grid=(n // tm, Z_COLS // tn),
        in_specs=[pl.BlockSpec((tm, D_MODEL), lambda i, j: (i, 0)),
                  pl.BlockSpec((1, D_MODEL), lambda i, j: (0, 0)),
                  mod_spec, mod_spec,
                  pl.BlockSpec((D_MODEL, tn), lambda i, j: (0, j))],
        out_specs=pl.BlockSpec((tm, tn), lambda i, j: (i, j)),
        out_shape=jax.ShapeDtypeStruct((n, Z_COLS), F32),
        scratch_shapes=[pltpu.VMEM((tm, D_MODEL), BF16)],
        compiler_params=pltpu.CompilerParams(dimension_semantics=("parallel", "arbitrary"),
                                             vmem_limit_bytes=VMEM_LIMIT),
        name="in_proj",
    )(x, gain.reshape(1, D_MODEL), sc, sh, w)


def _merge_kernel(ya_ref, yb_ref, yc_ref, yd_ref, gl_ref, x_ref, wb_ref, wo_ref, gain_ref, g1_ref, o_ref):
    acc = None
    for n, y_ref in enumerate((ya_ref, yb_ref, yc_ref, yd_ref)):
        p = jnp.dot(y_ref[...].astype(BF16), wb_ref[n], preferred_element_type=F32)
        t = jax.nn.sigmoid(gl_ref[:, n * D_MODEL:(n + 1) * D_MODEL]) * p
        acc = t if acc is None else acc + t
    m = jnp.dot(acc.astype(BF16), wo_ref[...], preferred_element_type=F32)
    o_ref[...] = x_ref[...] + g1_ref[0] * _rms(m, gain_ref[...])


def _merge(ys, z, x, wb, wo, gain, g1, tm):
    n = x.shape[0]
    tiles_per_group = n // tm // g1.shape[0]
    y_spec = pl.BlockSpec((tm, MIX_WIDTH), lambda i: (i, 0))
    return pl.pallas_call(
        _merge_kernel,
        grid=(n // tm,),
        in_specs=[y_spec, y_spec, y_spec, y_spec,
                  pl.BlockSpec((tm, GATE_COLS), lambda i: (i, 0)),
                  pl.BlockSpec((tm, D_MODEL), lambda i: (i, 0)),
                  pl.BlockSpec((N_BRANCH, MIX_WIDTH, D_MODEL), lambda i: (0, 0, 0)),
                  pl.BlockSpec((D_MODEL, D_MODEL), lambda i: (0, 0)),
                  pl.BlockSpec((1, D_MODEL), lambda i: (0, 0)),
                  pl.BlockSpec((1, g1.shape[1], D_MODEL), lambda i: (i // tiles_per_group, 0, 0))],
        out_specs=pl.BlockSpec((tm, D_MODEL), lambda i: (i, 0)),
        out_shape=jax.ShapeDtypeStruct((n, D_MODEL), F32),
        compiler_params=pltpu.CompilerParams(dimension_semantics=("parallel",),
                                             vmem_limit_bytes=VMEM_LIMIT),
        name="merge",
    )(*ys, z, x, wb, wo, gain.reshape(1, D_MODEL), g1)


def _ffn_kernel(x_ref, gain_h_ref, sc_ref, sh_ref, wg_ref, wu_ref, wd_ref, gain_o_ref, g2_ref, o_ref,
                h_ref, acc_ref):
    j = pl.program_id(1)

    @pl.when(j == 0)
    def _():
        h = _rms(x_ref[...], gain_h_ref[...]) * (1.0 + sc_ref[0]) + sh_ref[0]
        h_ref[...] = h.astype(BF16)
        acc_ref[...] = jnp.zeros_like(acc_ref)

    h = h_ref[...]
    a = jnp.dot(h, wg_ref[...], preferred_element_type=F32)
    b = jnp.dot(h, wu_ref[...], preferred_element_type=F32)
    t = (a * jax.nn.sigmoid(a)) * b
    acc_ref[...] += jnp.dot(t.astype(BF16), wd_ref[...], preferred_element_type=F32)

    @pl.when(j == pl.num_programs(1) - 1)
    def _():
        o_ref[...] = x_ref[...] + g2_ref[0] * _rms(acc_ref[...], gain_o_ref[...])


def _ffn(x, gain_h, sc, sh, wg, wu, wd, gain_o, g2, tm):
    n = x.shape[0]
    tiles_per_group = n // tm // sc.shape[0]
    th = FFN_HIDDEN // 2
    mod_spec = pl.BlockSpec((1, sc.shape[1], D_MODEL), lambda i, j: (i // tiles_per_group, 0, 0))
    vec_spec = pl.BlockSpec((1, D_MODEL), lambda i, j: (0, 0))
    return pl.pallas_call(
        _ffn_kernel,
        grid=(n // tm, FFN_HIDDEN // th),
        in_specs=[pl.BlockSpec((tm, D_MODEL), lambda i, j: (i, 0)),
                  vec_spec, mod_spec, mod_spec,
                  pl.BlockSpec((D_MODEL, th), lambda i, j: (0, j)),
                  pl.BlockSpec((D_MODEL, th), lambda i, j: (0, j)),
                  pl.BlockSpec((th, D_MODEL), lambda i, j: (j, 0)),
                  vec_spec, mod_spec],
        out_specs=pl.BlockSpec((tm, D_MODEL), lambda i, j: (i, 0)),
        out_shape=jax.ShapeDtypeStruct((n, D_MODEL), F32),
        scratch_shapes=[pltpu.VMEM((tm, D_MODEL), BF16), pltpu.VMEM((tm, D_MODEL), F32)],
        compiler_params=pltpu.CompilerParams(dimension_semantics=("parallel", "arbitrary"),
                                             vmem_limit_bytes=VMEM_LIMIT),
        name="ffn",
    )(x, gain_h.reshape(1, D_MODEL), sc, sh, wg, wu, wd, gain_o.reshape(1, D_MODEL), g2)


TQ = 256
N_HEADS = 4
SCALE_64 = HEAD_DIM ** -0.5


def _dot_nt(a, b):
    return lax.dot_general(a, b, (((1,), (1,)), ((), ())), preferred_element_type=F32)


def _softmax_step(s, v, m_ref, l_ref, acc_ref):
    m_old = m_ref[...]
    m_new = jnp.maximum(m_old, jnp.max(s, axis=-1, keepdims=True))
    alpha = jnp.exp(m_old - m_new)
    p = jnp.exp(s - m_new)
    l_ref[...] = alpha * l_ref[...] + jnp.sum(p, axis=-1, keepdims=True)
    acc_ref[...] = alpha * acc_ref[...] + jnp.dot(p.astype(BF16), v, preferred_element_type=F32)
    m_ref[...] = m_new


def _softmax_reset(m_ref, l_ref, acc_ref):
    m_ref[...] = jnp.full(m_ref.shape, NEG, F32)
    l_ref[...] = jnp.zeros(l_ref.shape, F32)
    acc_ref[...] = jnp.zeros(acc_ref.shape, F32)


def _lanes(x, width):
    return x if width == LANE else jnp.concatenate([x] * (width // LANE), axis=1)


def _flash_step(s, v, m_ref, acc_ref):
    m_old = m_ref[...]
    m_new = jnp.maximum(m_old, jnp.max(s, axis=-1, keepdims=True))
    alpha = jnp.exp(m_old - m_new)
    p = jnp.exp(s - _lanes(m_new, s.shape[1])).astype(BF16)
    v1 = jnp.concatenate([v, jnp.ones((v.shape[0], LANE), BF16)], axis=1)
    acc_ref[...] = _lanes(alpha, acc_ref.shape[1]) * acc_ref[...] + jnp.dot(p, v1, preferred_element_type=F32)
    m_ref[...] = m_new


def _flash_reset(m_ref, acc_ref):
    m_ref[...] = jnp.full(m_ref.shape, NEG, F32)
    acc_ref[...] = jnp.zeros(acc_ref.shape, F32)


def _flash_out(acc_ref):
    acc = acc_ref[...]
    dv = acc.shape[1] - LANE
    return acc[:, :dv] / _lanes(acc[:, dv:], dv)


def _causal_neg(rows):
    i = lax.broadcasted_iota(jnp.int32, (TQ, TQ), 0)
    j = lax.broadcasted_iota(jnp.int32, (TQ, TQ), 1)
    neg = jnp.where(j <= i, 0.0, NEG).astype(F32)
    return jnp.concatenate([neg] * rows, axis=0)


MLA_QW = 2 * LANE


def _mla_prep_kernel(ql_ref, kv_ref, kr_ref, krs_ref, cq_ref, sq_ref, gq_ref, gkv_ref, w1_ref, wuk_ref,
                     qcat_ref, kf_ref, kb_ref):
    ql = ql_ref[...]
    qn = ql * lax.rsqrt(jnp.sum(ql * ql, axis=-1, keepdims=True) * (1.0 / MLA_Q_LORA) + EPS) * gq_ref[...]
    q1 = jnp.dot(qn.astype(BF16), w1_ref[...], preferred_element_type=F32)
    q_abs = jnp.dot(q1[:, :MIX_WIDTH].astype(BF16), wuk_ref[...], preferred_element_type=F32)
    cc, ss = cq_ref[...], sq_ref[...]
    parts = []
    for h in range(N_HEADS):
        r = q1[:, MIX_WIDTH + h * LANE:MIX_WIDTH + (h + 1) * LANE]
        rs = q1[:, MIX_WIDTH + (N_HEADS + h) * LANE:MIX_WIDTH + (N_HEADS + h + 1) * LANE]
        parts += [q_abs[:, h * LANE:(h + 1) * LANE], r * cc + rs * ss]
    qcat_ref[...] = jnp.concatenate(parts, axis=1).astype(BF16)
    c_new = _rms(kv_ref[...], gkv_ref[...])
    kr = kr_ref[...] * cc + krs_ref[...] * ss
    k = jnp.concatenate([c_new, kr], axis=1)
    kf_ref[...] = k
    kb_ref[...] = k.astype(BF16)


def _mla_prep(z, cq, sq, gq, gkv, w1, wuk, T, tm):
    n = z.shape[0]
    tiles_per_seq = T // tm
    row = lambda c: pl.BlockSpec((tm, LANE), lambda i: (i, c))
    tab = pl.BlockSpec((tm, LANE), lambda i: (i % tiles_per_seq, 0))
    full = lambda a: pl.BlockSpec(a.shape, lambda i: (0,) * a.ndim)
    return pl.pallas_call(
        _mla_prep_kernel,
        grid=(n // tm,),
        in_specs=[pl.BlockSpec((tm, 2 * LANE), lambda i: (i, Z_QLAT // (2 * LANE))),
                  row(Z_KVLAT // LANE), row(Z_KROPE // LANE), row(Z_KROPE_SW // LANE),
                  tab, tab, full(gq), full(gkv), full(w1), full(wuk)],
        out_specs=[pl.BlockSpec((tm, N_HEADS * MLA_QW), lambda i: (i, 0)),
                   pl.BlockSpec((tm, MLA_QW), lambda i: (i, 0)),
                   pl.BlockSpec((tm, MLA_QW), lambda i: (i, 0))],
        out_shape=[jax.ShapeDtypeStruct((n, N_HEADS * MLA_QW), BF16),
                   jax.ShapeDtypeStruct((n, MLA_QW), F32),
                   jax.ShapeDtypeStruct((n, MLA_QW), BF16)],
        compiler_params=pltpu.CompilerParams(dimension_semantics=("parallel",), vmem_limit_bytes=VMEM_LIMIT),
        name="mla_prep",
    )(z, z, z, z, cq, sq, gq, gkv, w1, wuk)


def _mla_attn_kernel(q_ref, k_ref, wuv_ref, o_ref, m_ref, acc_ref):
    qi = pl.program_id(1)
    q = q_ref[...]
    qs = jnp.concatenate([q[:, h * MLA_QW:(h + 1) * MLA_QW] for h in range(N_HEADS)], axis=0)
    scale = (MLA_NOPE + MLA_ROPE) ** -0.5
    _flash_reset(m_ref, acc_ref)

    def step(kt, neg):
        k = k_ref[pl.ds(pl.multiple_of(kt * TQ, TQ), TQ), :]
        s = _dot_nt(qs, k) * scale
        if neg is not None:
            s = s + neg
        _flash_step(s, k[:, :MLA_KV_LORA], m_ref, acc_ref)

    lax.fori_loop(0, qi, lambda kt, c: (step(kt, None), c)[1], 0)
    step(qi, _causal_neg(N_HEADS))
    o_lat = _flash_out(acc_ref).astype(BF16)
    out = None
    for h in range(N_HEADS):
        t = jnp.dot(o_lat[h * TQ:(h + 1) * TQ], wuv_ref[h], preferred_element_type=F32)
        out = t if out is None else out + t
    o_ref[...] = out


def _mla_attn(qcat, kb, wuv, B, T):
    return pl.pallas_call(
        _mla_attn_kernel,
        grid=(B, T // TQ),
        in_specs=[pl.BlockSpec((TQ, N_HEADS * MLA_QW), lambda b, i: (b * (T // TQ) + i, 0)),
                  pl.BlockSpec((T, MLA_QW), lambda b, i: (b, 0)),
                  pl.BlockSpec(wuv.shape, lambda b, i: (0, 0, 0))],
        out_specs=pl.BlockSpec((TQ, MIX_WIDTH), lambda b, i: (b * (T // TQ) + i, 0)),
        out_shape=jax.ShapeDtypeStruct((B * T, MIX_WIDTH), F32),
        scratch_shapes=[pltpu.VMEM((N_HEADS * TQ, LANE), F32),
                        pltpu.VMEM((N_HEADS * TQ, MLA_KV_LORA + LANE), F32)],
        compiler_params=pltpu.CompilerParams(dimension_semantics=("parallel", "arbitrary"),
                                             vmem_limit_bytes=VMEM_LIMIT),
        name="mla_attn",
    )(qcat, kb, wuv)


def _fox_attn_kernel(q_ref, k_ref, v_ref, fq_ref, fk_ref, o_ref, fqb_ref, m_ref, acc_ref):
    qi = pl.program_id(1)
    q = q_ref[...] * SCALE_64
    head = lax.broadcasted_iota(jnp.int32, (TQ, MIX_WIDTH), 1) // HEAD_DIM
    qs = jnp.concatenate([jnp.where(head == h, q, 0.0) for h in range(N_HEADS)], axis=0).astype(BF16)
    fq = fq_ref[...]
    for h in range(N_HEADS):
        fqb_ref[h * TQ:(h + 1) * TQ, :] = jnp.broadcast_to(fq[:, h:h + 1], (TQ, TQ))
    _flash_reset(m_ref, acc_ref)

    def step(kt, neg):
        off = pl.multiple_of(kt * TQ, TQ)
        k = k_ref[pl.ds(off, TQ), :].astype(BF16)
        v = v_ref[pl.ds(off, TQ), :].astype(BF16)
        fk = fk_ref[kt]
        fkb = jnp.concatenate([jnp.broadcast_to(fk[h:h + 1, :], (TQ, TQ)) for h in range(N_HEADS)], axis=0)
        s = _dot_nt(qs, k) + (fqb_ref[...] - fkb)
        if neg is not None:
            s = s + neg
        _flash_step(s, v, m_ref, acc_ref)

    lax.fori_loop(0, qi, lambda kt, c: (step(kt, None), c)[1], 0)
    step(qi, _causal_neg(N_HEADS))
    o = _flash_out(acc_ref)
    out = jnp.zeros((TQ, MIX_WIDTH), F32)
    for h in range(N_HEADS):
        out = jnp.where(head == h, o[h * TQ:(h + 1) * TQ], out)
    o_ref[...] = out


def _fox_attn(z, fq, fk, B, T):
    nq = T // TQ
    col = lambda c: pl.BlockSpec((T, MIX_WIDTH), lambda b, i: (b, c))
    return pl.pallas_call(
        _fox_attn_kernel,
        grid=(B, nq),
        in_specs=[pl.BlockSpec((TQ, MIX_WIDTH), lambda b, i: (b * nq + i, Z_FQ // MIX_WIDTH)),
                  col(Z_FK // MIX_WIDTH), col(Z_FV // MIX_WIDTH),
                  pl.BlockSpec((TQ, LANE), lambda b, i: (b * nq + i, 0)),
                  pl.BlockSpec((None, nq, 8, TQ), lambda b, i: (b, 0, 0, 0))],
        out_specs=pl.BlockSpec((TQ, MIX_WIDTH), lambda b, i: (b * nq + i, 0)),
        out_shape=jax.ShapeDtypeStruct((B * T, MIX_WIDTH), F32),
        scratch_shapes=[pltpu.VMEM((N_HEADS * TQ, TQ), F32),
                        pltpu.VMEM((N_HEADS * TQ, LANE), F32),
                        pltpu.VMEM((N_HEADS * TQ, MIX_WIDTH + LANE), F32)],
        compiler_params=pltpu.CompilerParams(dimension_semantics=("parallel", "arbitrary"),
                                             vmem_limit_bytes=VMEM_LIMIT),
        name="fox_attn",
    )(z, z, z, fq, fk)


def _nsa_tables(rel_bias, T):
    def tab(dist):
        oh = jax.nn.one_hot(_rel_bucket(dist), REL_BUCKETS, dtype=F32)
        b = jnp.einsum('rcb,bh->hrc', oh, rel_bias.astype(F32), precision=lax.Precision.HIGHEST)
        return b.reshape(N_HEADS * dist.shape[0], dist.shape[1])
    i = jnp.arange(TQ, dtype=jnp.int32)[:, None]
    j = jnp.arange(TQ, dtype=jnp.int32)[None, :]
    tile4 = lambda m: jnp.concatenate([m] * N_HEADS, axis=0)
    far = tab(jnp.full((TQ, TQ), REL_MAX_DIST, jnp.int32))
    prev = tab(TQ + i - j)
    diag = jnp.where(tile4(j <= i), tab(i - j), NEG)
    edge = jnp.where(tile4(j >= i), far, NEG)
    tb = jnp.stack([far, prev, diag, edge])
    n_cb = T // NSA_CMP_BLOCK
    pos = jnp.arange(T, dtype=jnp.int32)[:, None]
    cb_end = jnp.arange(n_cb, dtype=jnp.int32)[None, :] * NSA_CMP_BLOCK + (NSA_CMP_BLOCK - 1)
    bc = tab(pos - cb_end).reshape(N_HEADS, T // TQ, TQ, n_cb)
    bc = jnp.moveaxis(bc, 1, 0).reshape(T // TQ, N_HEADS * TQ, n_cb)
    return tb, bc


def _nsa_attn_kernel(q_ref, kcvc_ref, ksvs_ref, kwvw_ref, ng_ref, wkv_ref, tb_ref, bc_ref, o_ref,
                     cb_ref, expand_ref, negsel_ref, m_ref, acc_ref):
    qi = pl.program_id(1)
    T = ksvs_ref.shape[0]
    n_cb = T // NSA_CMP_BLOCK
    n_kt = T // TQ

    @pl.when(qi == 0)
    def _():
        x = kcvc_ref[...].reshape(n_cb, NSA_CMP_BLOCK, LANE) * wkv_ref[...][None]
        cb_ref[...] = jnp.sum(x, axis=1)
        r = lax.broadcasted_iota(jnp.int32, (n_cb, T), 0)
        key = lax.broadcasted_iota(jnp.int32, (n_cb, T), 1)
        expand_ref[...] = jnp.where(r == 2 * (key // NSA_SEL_BLOCK), 1.0, 0.0).astype(BF16)

    lane = lax.broadcasted_iota(jnp.int32, (TQ, LANE), 1)
    lo = lane < HEAD_DIM
    q = q_ref[...] * SCALE_64
    t0, t1 = q[:, :LANE], q[:, LANE:]
    qs = jnp.concatenate([jnp.where(lo, t0, 0.0), jnp.where(lo, pltpu.roll(t0, HEAD_DIM, 1), 0.0),
                          jnp.where(lo, t1, 0.0), jnp.where(lo, pltpu.roll(t1, HEAD_DIM, 1), 0.0)],
                         axis=0).astype(BF16)

    cb = cb_ref[...].astype(BF16)
    qpos = qi * TQ + lax.broadcasted_iota(jnp.int32, (TQ, n_cb), 0)
    cb_end = lax.broadcasted_iota(jnp.int32, (TQ, n_cb), 1) * NSA_CMP_BLOCK + (NSA_CMP_BLOCK - 1)
    valid = jnp.concatenate([cb_end <= qpos] * N_HEADS, axis=0)
    lc = jnp.where(valid, _dot_nt(qs, cb) + bc_ref[0], NEG)
    e = jnp.where(valid, jnp.exp(lc - jnp.max(lc, axis=-1, keepdims=True)), 0.0)
    pc = e / jnp.maximum(jnp.sum(e, axis=-1, keepdims=True), 1e-30)
    o_c = jnp.dot(pc.astype(BF16), cb, preferred_element_type=F32)

    psum = pc[0:TQ] + pc[TQ:2 * TQ] + pc[2 * TQ:3 * TQ] + pc[3 * TQ:4 * TQ]
    imp = (psum + pltpu.roll(psum, n_cb - 1, 1)).T
    row = lax.broadcasted_iota(jnp.int32, (n_cb, TQ), 0)
    qpos_t = qi * TQ + lax.broadcasted_iota(jnp.int32, (n_cb, TQ), 1)
    blk = row // 2
    score = jnp.where(qpos_t // NSA_SEL_BLOCK == blk, FORCE,
                      jnp.where(blk * NSA_SEL_BLOCK <= qpos_t, imp, NEG))
    score = jnp.where(row % 2 == 0, score, 2 * NEG)
    cnt = jnp.zeros((n_cb, TQ), F32)
    for i in range(0, n_cb, 2):
        si = score[i:i + 1, :]
        tie = jnp.where(row > i, 1.0, 0.0)
        cnt = cnt + jnp.where(si > score, 1.0, jnp.where(si == score, tie, 0.0))
    sel = jnp.where(cnt < NSA_TOPK, jnp.where(score > NEG / 2, 1.0, 0.0), 0.0).T
    keymask = jnp.dot(sel.astype(BF16), expand_ref[...], preferred_element_type=F32)
    for kt in range(n_kt):
        negsel_ref[kt] = (keymask[:, kt * TQ:(kt + 1) * TQ] - 1.0) * (-NEG)

    def step(kt, tab, kv_ref, use_sel):
        kv = kv_ref[pl.ds(pl.multiple_of(kt * TQ, TQ), TQ), :].astype(BF16)
        s = _dot_nt(qs, kv) + tb_ref[tab]
        if use_sel:
            ns = negsel_ref[kt]
            s = s + jnp.concatenate([ns] * N_HEADS, axis=0)
        _flash_step(s, kv, m_ref, acc_ref)

    _flash_reset(m_ref, acc_ref)
    lax.fori_loop(0, qi + 1,
                  lambda kt, c: (step(kt, jnp.maximum(kt - qi + 2, 0), ksvs_ref, True), c)[1], 0)
    o_s = _flash_out(acc_ref)
    _flash_reset(m_ref, acc_ref)
    lax.fori_loop(jnp.maximum(qi - 2, 0), qi + 1,
                  lambda kt, c: (step(kt, jnp.where(kt == qi - 2, 3, kt - qi + 2), kwvw_ref, False), c)[1], 0)
    o_w = _flash_out(acc_ref)

    g = jax.nn.sigmoid(ng_ref[...])
    mixed = []
    for h in range(N_HEADS):
        rows = slice(h * TQ, (h + 1) * TQ)
        mixed.append(g[:, h:h + 1] * o_c[rows] + g[:, N_HEADS + h:N_HEADS + h + 1] * o_s[rows]
                     + g[:, 2 * N_HEADS + h:2 * N_HEADS + h + 1] * o_w[rows])
    o_ref[...] = jnp.concatenate([jnp.where(lo, pltpu.roll(mixed[0], HEAD_DIM, 1), mixed[1]),
                                  jnp.where(lo, pltpu.roll(mixed[2], HEAD_DIM, 1), mixed[3])], axis=1)


def _nsa_attn(z, wkv, tb, bc, B, T):
    nq = T // TQ
    n_cb = T // NSA_CMP_BLOCK
    kv = lambda c: pl.BlockSpec((T, LANE), lambda b, i: (b, c))
    return pl.pallas_call(
        _nsa_attn_kernel,
        grid=(B, nq),
        in_specs=[pl.BlockSpec((TQ, MIX_WIDTH), lambda b, i: (b * nq + i, Z_NQ // MIX_WIDTH)),
                  kv(Z_KCVC // LANE), kv(Z_KSVS // LANE), kv(Z_KWVW // LANE),
                  pl.BlockSpec((TQ, LANE), lambda b, i: (b * nq + i, Z_NG // LANE)),
                  pl.BlockSpec(wkv.shape, lambda b, i: (0, 0)),
                  pl.BlockSpec(tb.shape, lambda b, i: (0, 0, 0)),
                  pl.BlockSpec((1, N_HEADS * TQ, n_cb), lambda b, i: (i, 0, 0))],
        out_specs=pl.BlockSpec((TQ, MIX_WIDTH), lambda b, i: (b * nq + i, 0)),
        out_shape=jax.ShapeDtypeStruct((B * T, MIX_WIDTH), F32),
        scratch_shapes=[pltpu.VMEM((n_cb, LANE), F32), pltpu.VMEM((n_cb, T), BF16),
                        pltpu.VMEM((nq, TQ, TQ), F32),
                        pltpu.VMEM((N_HEADS * TQ, LANE), F32),
                        pltpu.VMEM((N_HEADS * TQ, 2 * LANE), F32)],
        compiler_params=pltpu.CompilerParams(dimension_semantics=("parallel", "arbitrary"),
                                             vmem_limit_bytes=VMEM_LIMIT),
        name="nsa_attn",
    )(z, z, z, z, z, wkv, tb, bc)


PAGES_PER_STEP = 16
SUB = 8
HIGHEST = lax.Precision.HIGHEST


def _page_specs(layer, rows, row_block=0):
    return [pl.BlockSpec((None, None, rows, PAGE_ROWS),
                         lambda b, j, pt, k=k: (layer, pt[b, j * PAGES_PER_STEP + k], row_block, 0))
            for k in range(PAGES_PER_STEP)]


def _softmax_step_t(s, vt, m_ref, l_ref, acc_ref):
    m_old = m_ref[...]
    m_new = jnp.maximum(m_old, jnp.max(s, axis=-1, keepdims=True))
    alpha = jnp.exp(m_old - m_new)
    p = jnp.exp(s - m_new)
    l_ref[...] = alpha * l_ref[...] + jnp.sum(p, axis=-1, keepdims=True)
    acc_ref[...] = alpha * acc_ref[...] + _dot_nt(p.astype(BF16), vt)
    m_ref[...] = m_new


def _seq_spec(rows, width):
    return pl.BlockSpec((None, rows, width), lambda b, j, pt: (b, 0, 0))


def _const_spec(a):
    return pl.BlockSpec(a.shape, lambda b, j, pt: (0,) * a.ndim)


def _merge_new_key(s_new, v_new, m_ref, l_ref, acc_ref):
    m_old = m_ref[...]
    m_new = jnp.maximum(m_old, s_new)
    alpha = jnp.exp(m_old - m_new)
    p = jnp.exp(s_new - m_new)
    return (alpha * acc_ref[...] + p * v_new) / (alpha * l_ref[...] + p)


def _bf16_round(x):
    return x.astype(BF16).astype(F32)


def _paged_call(kernel_fn, name, page_table, n_steps, in_specs, out_specs, out_shape, scratch_shapes, args):
    nb = page_table.shape[0]
    return pl.pallas_call(
        kernel_fn,
        grid_spec=pltpu.PrefetchScalarGridSpec(num_scalar_prefetch=1, grid=(nb, n_steps), in_specs=in_specs,
                                               out_specs=out_specs, scratch_shapes=scratch_shapes),
        out_shape=out_shape,
        compiler_params=pltpu.CompilerParams(dimension_semantics=("parallel", "arbitrary"),
                                             vmem_limit_bytes=VMEM_LIMIT),
        name=name,
    )(page_table, *args)


def _mla_dec_kernel(pt_ref, q_ref, knew_ref, *refs):
    pages = refs[:PAGES_PER_STEP]
    o_ref, m_ref, l_ref, acc_ref = refs[PAGES_PER_STEP:]
    j = pl.program_id(1)
    scale = (MLA_NOPE + MLA_ROPE) ** -0.5

    @pl.when(j == 0)
    def _():
        _softmax_reset(m_ref, l_ref, acc_ref)

    qs = q_ref[...]
    kt = jnp.concatenate([p[...] for p in pages], axis=1).astype(BF16)
    s = jnp.dot(qs[:, :MLA_KV_LORA + MLA_ROPE], kt, preferred_element_type=F32) * scale
    _softmax_step_t(s, kt[:MLA_KV_LORA], m_ref, l_ref, acc_ref)

    @pl.when(j == pl.num_programs(1) - 1)
    def _():
        kn = _bf16_round(knew_ref[...])
        s_new = jnp.sum(qs.astype(F32) * kn, axis=-1, keepdims=True) * scale
        o_ref[...] = _merge_new_key(s_new, kn[:, :MLA_KV_LORA], m_ref, l_ref, acc_ref)


def _mla_dec(layer, page_table, cache_mla, q, knew):
    nb, n_pages = page_table.shape
    return _paged_call(
        _mla_dec_kernel, "mla_decode", page_table, n_pages // PAGES_PER_STEP,
        [_seq_spec(SUB, MLA_QW), _seq_spec(1, MLA_QW)] + _page_specs(layer, cache_mla.shape[2]),
        _seq_spec(SUB, MLA_KV_LORA), jax.ShapeDtypeStruct((nb, SUB, MLA_KV_LORA), F32),
        [pltpu.VMEM((SUB, 1), F32), pltpu.VMEM((SUB, 1), F32), pltpu.VMEM((SUB, MLA_KV_LORA), F32)],
        [q, knew] + [cache_mla] * PAGES_PER_STEP)


def _fox_dec_kernel(pt_ref, q_ref, knew_ref, vnew_ref, fnew_ref, dt_ref, *refs):
    P = PAGES_PER_STEP
    k_pages, v_pages, f_pages = refs[:P], refs[P:2 * P], refs[2 * P:3 * P]
    o_ref, fsum_ref, m_ref, l_ref, acc_ref = refs[3 * P:]
    j = pl.program_id(1)

    @pl.when(j == 0)
    def _():
        _softmax_reset(m_ref, l_ref, acc_ref)
        fsum_ref[...] = jnp.zeros(fsum_ref.shape, F32)

    qs = q_ref[...]
    carry = fsum_ref[...]
    pad = jnp.zeros((SUB - FOX_HEADS, PAGE_ROWS), F32)
    fks = []
    for p in f_pages:
        x = jnp.concatenate([p[...], pad], axis=0)
        fk = jnp.dot(x, dt_ref[...], preferred_element_type=F32, precision=HIGHEST) + carry
        carry = fk[:, PAGE_ROWS - 1:PAGE_ROWS]
        fks.append(fk)
    fsum_ref[...] = carry
    kt = jnp.concatenate([p[...] for p in k_pages], axis=1).astype(BF16)
    vt = jnp.concatenate([p[...] for p in v_pages], axis=1).astype(BF16)
    s = jnp.dot(qs, kt, preferred_element_type=F32) - jnp.concatenate(fks, axis=1)
    _softmax_step_t(s, vt, m_ref, l_ref, acc_ref)

    @pl.when(j == pl.num_programs(1) - 1)
    def _():
        m_ref[...] = m_ref[...] + (carry + fnew_ref[...][:, 0:1])
        s_new = jnp.sum(qs.astype(F32) * _bf16_round(knew_ref[...]), axis=-1, keepdims=True)
        o = _merge_new_key(s_new, _bf16_round(vnew_ref[...]), m_ref, l_ref, acc_ref)
        head = lax.broadcasted_iota(jnp.int32, (SUB, MIX_WIDTH), 1) // HEAD_DIM
        hrow = lax.broadcasted_iota(jnp.int32, (SUB, MIX_WIDTH), 0)
        o_ref[...] = jnp.sum(jnp.where(head == hrow, o, 0.0), axis=0, keepdims=True)


def _fox_dec(layer, page_table, cache_k, cache_v, cache_f, q, knew, vnew, fnew, dt):
    nb, n_pages = page_table.shape
    return _paged_call(
        _fox_dec_kernel, "fox_decode", page_table, n_pages // PAGES_PER_STEP,
        [_seq_spec(SUB, MIX_WIDTH), _seq_spec(1, MIX_WIDTH), _seq_spec(1, MIX_WIDTH), _seq_spec(SUB, LANE),
         _const_spec(dt)]
        + _page_specs(layer, MIX_WIDTH) + _page_specs(layer, MIX_WIDTH) + _page_specs(layer, FOX_HEADS),
        _seq_spec(1, MIX_WIDTH), jax.ShapeDtypeStruct((nb, 1, MIX_WIDTH), F32),
        [pltpu.VMEM((SUB, 1), F32), pltpu.VMEM((SUB, 1), F32), pltpu.VMEM((SUB, 1), F32),
         pltpu.VMEM((SUB, MIX_WIDTH), F32)],
        [q, knew, vnew, fnew, dt] + [cache_k] * PAGES_PER_STEP + [cache_v] * PAGES_PER_STEP
        + [cache_f] * PAGES_PER_STEP)


def _nsa_cmp_dec_kernel(pt_ref, q_ref, wt_ref, bias_ref, pair_ref, *refs):
    P = PAGES_PER_STEP
    pages = refs[:P]
    oc_ref, idx_ref, cb_ref = refs[P:]
    j = pl.program_id(1)
    per_page = PAGE_ROWS // NSA_CMP_BLOCK
    wk, wv = wt_ref[0].astype(BF16), wt_ref[1].astype(BF16)
    blocks = []
    for p in pages:
        x = p[...].astype(BF16)
        blocks.append(jnp.concatenate([_dot_nt(wk, x[:HEAD_DIM])[:per_page],
                                       _dot_nt(wv, x[HEAD_DIM:])[:per_page]], axis=1))
    rows = P * per_page
    cb_ref[pl.ds(pl.multiple_of(j * rows, rows), rows), :] = jnp.concatenate(blocks, axis=0)

    @pl.when(j == pl.num_programs(1) - 1)
    def _():
        qs = q_ref[...]
        cb = cb_ref[...].astype(BF16)
        lc = _dot_nt(qs, cb) + bias_ref[...]
        e = jnp.exp(lc - jnp.max(lc, axis=-1, keepdims=True))
        pc = e / jnp.maximum(jnp.sum(e, axis=-1, keepdims=True), 1e-30)
        oc_ref[...] = jnp.dot(pc.astype(BF16), cb, preferred_element_type=F32)
        psum = jnp.sum(pc[0:N_HEADS], axis=0, keepdims=True)
        imp = jnp.dot(jnp.broadcast_to(psum, (SUB, psum.shape[1])), pair_ref[...],
                      preferred_element_type=F32, precision=HIGHEST)
        n_sel = imp.shape[1]
        col = jnp.broadcast_to(imp.T[:, 0:1], (n_sel, n_sel))
        rowv = jnp.broadcast_to(imp[0:1, :], (n_sel, n_sel))
        i_idx = lax.broadcasted_iota(jnp.int32, (n_sel, n_sel), 0)
        j_idx = lax.broadcasted_iota(jnp.int32, (n_sel, n_sel), 1)
        ahead = jnp.where(col > rowv, 1.0, jnp.where(col == rowv, jnp.where(i_idx < j_idx, 1.0, 0.0), 0.0))
        rank = jnp.sum(ahead, axis=0, keepdims=True)
        r = lax.broadcasted_iota(jnp.int32, (2 * SUB, n_sel), 0).astype(F32)
        blk = lax.broadcasted_iota(jnp.int32, (2 * SUB, n_sel), 1).astype(F32)
        pick = jnp.sum(jnp.where(rank == r, blk, 0.0), axis=-1, keepdims=True)
        idx_ref[...] = jnp.broadcast_to(pick, (2 * SUB, LANE)).astype(jnp.int32)


def _nsa_cmp_dec(layer, page_table, cache_nsa, q, wt, bias, pair):
    nb, n_pages = page_table.shape
    n_cb = n_pages * PAGE_ROWS // NSA_CMP_BLOCK
    return _paged_call(
        _nsa_cmp_dec_kernel, "nsa_cmp_decode", page_table, n_pages // PAGES_PER_STEP,
        [_seq_spec(SUB, LANE), _const_spec(wt), _const_spec(bias), _const_spec(pair)]
        + _page_specs(layer, 2 * HEAD_DIM, 0),
        [_seq_spec(SUB, LANE), _seq_spec(2 * SUB, LANE)],
        [jax.ShapeDtypeStruct((nb, SUB, LANE), F32), jax.ShapeDtypeStruct((nb, 2 * SUB, LANE), jnp.int32)],
        [pltpu.VMEM((n_cb, LANE), F32)],
        [q, wt, bias, pair] + [cache_nsa] * PAGES_PER_STEP)


N_SEL_PAST = NSA_TOPK - 1


def _nsa_sel_dec_kernel(sel_ref, id_ref, q_ref, oc_ref, snew_ref, wnew_ref, gate_ref, win_ref, bsel_ref, bwin_ref,
                        b0_ref, *refs):
    blocks = refs[:N_SEL_PAST]
    o_ref = refs[N_SEL_PAST]
    b = pl.program_id(0)
    qs = q_ref[...][:, :HEAD_DIM]
    qf = qs.astype(F32)
    b0 = b0_ref[...]
    half_of_lane = lax.broadcasted_iota(jnp.int32, (SUB, PAGE_ROWS), 1) // NSA_SEL_BLOCK

    def attend(vts, s_list, new_ref):
        new = _bf16_round(new_ref[...])
        s_new = jnp.sum(qf * new[:, :HEAD_DIM], axis=-1, keepdims=True) + b0[:, 0:1]
        m = s_new
        for s in s_list:
            m = jnp.maximum(m, jnp.max(s, axis=-1, keepdims=True))
        p_new = jnp.exp(s_new - m)
        l, acc = p_new, p_new * new[:, HEAD_DIM:]
        for s, vt in zip(s_list, vts):
            p = jnp.exp(s - m)
            l = l + jnp.sum(p, axis=-1, keepdims=True)
            acc = acc + _dot_nt(p.astype(BF16), vt)
        return acc / l

    sel_s, sel_vt = [], []
    for r, ref in enumerate(blocks):
        x = ref[...].astype(BF16)
        blk = id_ref[b, r]
        s = jnp.dot(qs, x[:HEAD_DIM], preferred_element_type=F32) + bsel_ref[blk]
        sel_s.append(jnp.where(half_of_lane == blk % (PAGE_ROWS // NSA_SEL_BLOCK), s, NEG))
        sel_vt.append(x[HEAD_DIM:])
    o_s = attend(sel_vt, sel_s, snew_ref)
    win = win_ref[...].astype(BF16)
    o_w = attend([win[HEAD_DIM:]], [jnp.dot(qs, win[:HEAD_DIM], preferred_element_type=F32) + bwin_ref[...]],
                 wnew_ref)
    g = gate_ref[...]
    mixed = (g[0][:, :HEAD_DIM] * oc_ref[...][:, HEAD_DIM:] + g[1][:, :HEAD_DIM] * o_s
             + g[2][:, :HEAD_DIM] * o_w)
    o_ref[...] = jnp.concatenate([mixed[h:h + 1] for h in range(N_HEADS)], axis=1)


def _nsa_sel_dec(layer, sel_page, sel_id, cache_nsa, q, oc, snew, wnew, gates, win, bsel, bwin, b0):
    nb = q.shape[0]
    seq = lambda rows, width: pl.BlockSpec((None, rows, width), lambda b, sp, si: (b, 0, 0))
    const = lambda a: pl.BlockSpec(a.shape, lambda b, sp, si: (0,) * a.ndim)
    blk_specs = [pl.BlockSpec((None, None, 2 * HEAD_DIM, PAGE_ROWS), lambda b, sp, si, r=r: (layer, sp[b, r], 1, 0))
                 for r in range(N_SEL_PAST)]
    return pl.pallas_call(
        _nsa_sel_dec_kernel,
        grid_spec=pltpu.PrefetchScalarGridSpec(
            num_scalar_prefetch=2, grid=(nb,),
            in_specs=[seq(SUB, LANE), seq(SUB, LANE), seq(1, LANE), seq(1, LANE),
                      pl.BlockSpec((None, 3, SUB, LANE), lambda b, sp, si: (b, 0, 0, 0)),
                      pl.BlockSpec((None, None, 2 * HEAD_DIM, win.shape[3]), lambda b, sp, si: (layer, b, 0, 0)),
                      const(bsel), const(bwin), const(b0)] + blk_specs,
            out_specs=seq(1, MIX_WIDTH)),
        out_shape=jax.ShapeDtypeStruct((nb, 1, MIX_WIDTH), F32),
        compiler_params=pltpu.CompilerParams(dimension_semantics=("parallel",), vmem_limit_bytes=VMEM_LIMIT),
        name="nsa_sel_decode",
    )(sel_page, sel_id, q, oc, snew, wnew, gates, win, bsel, bwin, b0, *([cache_nsa] * N_SEL_PAST))


S5_CHUNK = 256
S5_WIDTH = S5_GROUPS * S5_STATE


def _s5_scan_kernel(u_ref, bbr_ref, bbi_ref, a_ref, cr_ref, ci_ref, d_ref, wg_ref, y_ref, st_ref,
                    xr_ref, xi_ref, sr_ref, si_ref):
    j = pl.program_id(0)

    @pl.when(j == 0)
    def _():
        sr_ref[...] = jnp.zeros(sr_ref.shape, F32)
        si_ref[...] = jnp.zeros(si_ref.shape, F32)

    u = u_ref[...]
    ub = u.astype(BF16)
    xr_ref[...] = jnp.dot(ub, bbr_ref[...], preferred_element_type=F32)
    xi_ref[...] = jnp.dot(ub, bbi_ref[...], preferred_element_type=F32)
    ar = jnp.broadcast_to(a_ref[0:1, :], (SUB, S5_WIDTH))
    ai = jnp.broadcast_to(a_ref[1:2, :], (SUB, S5_WIDTH))

    def step(t, carry):
        xr, xi = carry
        rows = pl.ds(pl.multiple_of(t * SUB, SUB), SUB)
        nr = ar * xr - ai * xi + xr_ref[rows, :]
        ni = ar * xi + ai * xr + xi_ref[rows, :]
        xr_ref[rows, :] = nr
        xi_ref[rows, :] = ni
        return nr, ni

    xr, xi = lax.fori_loop(0, S5_CHUNK, step, (sr_ref[...], si_ref[...]), unroll=4)
    sr_ref[...] = xr
    si_ref[...] = xi
    st_ref[0] = xr
    st_ref[1] = xi
    y = (jnp.dot(xr_ref[...].astype(BF16), cr_ref[...], preferred_element_type=F32)
         - jnp.dot(xi_ref[...].astype(BF16), ci_ref[...], preferred_element_type=F32)) + d_ref[...] * u
    y = jax.nn.gelu(y)
    y_ref[...] = y * jax.nn.sigmoid(jnp.dot(y.astype(BF16), wg_ref[...], preferred_element_type=F32))


def _s5_scan(u_tb, bbr, bbi, a, cr, ci, dvec, wg):
    n = u_tb.shape[0]
    rows = S5_CHUNK * SUB
    full = lambda x: pl.BlockSpec(x.shape, lambda j: (0,) * x.ndim)
    return pl.pallas_call(
        _s5_scan_kernel,
        grid=(n // rows,),
        in_specs=[pl.BlockSpec((rows, MIX_WIDTH), lambda j: (j, 0)), full(bbr), full(bbi), full(a), full(cr),
                  full(ci), full(dvec), full(wg)],
        out_specs=[pl.BlockSpec((rows, MIX_WIDTH), lambda j: (j, 0)),
                   pl.BlockSpec((2, SUB, S5_WIDTH), lambda j: (0, 0, 0))],
        out_shape=[jax.ShapeDtypeStruct((n, MIX_WIDTH), F32), jax.ShapeDtypeStruct((2, SUB, S5_WIDTH), F32)],
        scratch_shapes=[pltpu.VMEM((rows, S5_WIDTH), F32), pltpu.VMEM((rows, S5_WIDTH), F32),
                        pltpu.VMEM((SUB, S5_WIDTH), F32), pltpu.VMEM((SUB, S5_WIDTH), F32)],
        compiler_params=pltpu.CompilerParams(dimension_semantics=("arbitrary",), vmem_limit_bytes=VMEM_LIMIT),
        name="s5_scan",
    )(u_tb, bbr, bbi, a, cr, ci, dvec, wg)


def _s5_weights(w, l):
    lam_re, lam_im = w['s5_lambda_re'][l], w['s5_lambda_im'][l]
    dt = jnp.exp(w['s5_log_dt'][l])[:, None]
    mag = jnp.exp(lam_re * dt)
    a_re, a_im = mag * jnp.cos(lam_im * dt), mag * jnp.sin(lam_im * dt)
    den = lam_re * lam_re + lam_im * lam_im
    coef_re = ((a_re - 1.0) * lam_re + a_im * lam_im) / den
    coef_im = (a_im * lam_re - (a_re - 1.0) * lam_im) / den
    b_re, b_im = w['s5_b_re'][l], w['s5_b_im'][l]
    bb_re = coef_re[..., None] * b_re - coef_im[..., None] * b_im
    bb_im = coef_re[..., None] * b_im + coef_im[..., None] * b_re
    eye = jnp.eye(S5_GROUPS, dtype=F32)
    bd_in = lambda bb: jnp.einsum('gpi,gh->gihp', bb, eye).reshape(MIX_WIDTH, S5_WIDTH)
    bd_out = lambda c: jnp.einsum('gip,gh->gphi', c, eye).reshape(S5_WIDTH, MIX_WIDTH)
    a = jnp.stack([a_re.reshape(S5_WIDTH), a_im.reshape(S5_WIDTH)])
    return (bd_in(bb_re).astype(BF16), bd_in(bb_im).astype(BF16), a,
            bd_out(w['s5_c_re'][l]).astype(BF16), bd_out(w['s5_c_im'][l]).astype(BF16),
            w['s5_d'][l].reshape(1, MIX_WIDTH), w['s5_w_glu'][l].astype(BF16))


def _s5_prompt(u, w, l):
    B, T, _ = u.shape
    u_tb = jnp.pad(jnp.swapaxes(u, 0, 1), ((0, 0), (0, SUB - B), (0, 0))).reshape(T * SUB, MIX_WIDTH)
    y, st = _s5_scan(u_tb, *_s5_weights(w, l))
    y = jnp.swapaxes(y.reshape(T, SUB, MIX_WIDTH)[:, :B], 0, 1)
    return y, jnp.swapaxes(st[:, :B], 0, 1).reshape(B, 2, S5_GROUPS, S5_STATE)


def _rowmm_kernel(x_ref, w_ref, o_ref):
    o_ref[...] = jnp.dot(x_ref[...].astype(BF16), w_ref[...], preferred_element_type=F32)


def _rowmm(x, w):
    return pl.pallas_call(_rowmm_kernel, out_shape=jax.ShapeDtypeStruct((x.shape[0], w.shape[1]), F32),
                          name="row_matmul")(x, w)


def _rms_norm(x, g):
    xf = x.astype(F32)
    y = xf * lax.rsqrt(jnp.mean(xf * xf, axis=-1, keepdims=True) + EPS)
    return (y * g.astype(F32)).astype(x.dtype)


def _rope(x, pos):
    half = x.shape[-1] // 2
    freq = ROPE_THETA ** (-jnp.arange(half, dtype=F32) / half)
    ang = pos.astype(F32)[:, None] * freq[None, :]
    ang = ang.reshape(ang.shape[0], *([1] * (x.ndim - 3)), half)
    cos, sin = jnp.cos(ang), jnp.sin(ang)
    x1, x2 = x[..., :half], x[..., half:]
    return jnp.concatenate([x1 * cos - x2 * sin, x1 * sin + x2 * cos], axis=-1)


def _rel_bucket(dist):
    n = jnp.maximum(dist, 0)
    exact = REL_BUCKETS // 2
    large = exact + (jnp.log(jnp.maximum(n, 1).astype(F32) / exact)
                     / math.log(REL_MAX_DIST / exact) * (REL_BUCKETS - exact)).astype(jnp.int32)
    large = jnp.minimum(large, REL_BUCKETS - 1)
    return jnp.where(n < exact, n, large)


def _t5_bias(rel_bias, dist):
    b = rel_bias[_rel_bucket(dist)].astype(F32)
    return jnp.moveaxis(b, -1, -3)


def _masked_softmax(logits, mask):
    lg = jnp.where(mask, logits, NEG)
    m = jnp.max(lg, axis=-1, keepdims=True)
    e = jnp.where(mask, jnp.exp(lg - m), 0.0)
    return e / jnp.maximum(jnp.sum(e, axis=-1, keepdims=True), 1e-30)


def _attend(logits, values, spec):
    lg = logits[0] if len(logits) == 1 else jnp.concatenate(logits, axis=-1)
    p = jax.nn.softmax(lg, axis=-1)
    out, start = None, 0
    for l_, v_ in zip(logits, values):
        n = l_.shape[-1]
        o = jnp.einsum(spec, p[..., start:start + n], v_)
        out = o if out is None else out + o
        start += n
    return out


def _map_query_blocks(fn, arrays, qpos):
    T = qpos.shape[0]
    qb = Q_BLOCK if T % Q_BLOCK == 0 else T
    nb = T // qb

    def split(a):
        return jnp.moveaxis(a.reshape(a.shape[0], nb, qb, *a.shape[2:]), 1, 0)
    xs = tuple(split(a) for a in arrays) + (qpos.reshape(nb, qb),)
    out = lax.map(lambda args: fn(*args), xs)
    out = jnp.moveaxis(out, 0, 1)
    return out.reshape(out.shape[0], T, *out.shape[3:])


def _complex_affine_combine(e1, e2):
    a1r, a1i, b1r, b1i = e1
    a2r, a2i, b2r, b2i = e2
    return (a2r * a1r - a2i * a1i, a2r * a1i + a2i * a1r,
            a2r * b1r - a2i * b1i + b2r, a2r * b1i + a2i * b1r + b2i)


def _gather_past(l, cache_mla, cache_fox_k, cache_fox_v, cache_fox_logf, cache_nsa,
                 state_nsa_win, state_s5, page_table):
    nb, n_pages = page_table.shape

    def pages(cache):
        g = cache[l, page_table]
        return g.reshape(nb, n_pages * g.shape[2], *g.shape[3:])
    return {'mla': pages(cache_mla), 'fox_k': pages(cache_fox_k), 'fox_v': pages(cache_fox_v),
            'fox_logf': pages(cache_fox_logf), 'nsa': pages(cache_nsa),
            'win': state_nsa_win[l], 's5': state_s5[l]}


def _mla(q_lat, kv_lat, k_rope, pos, past, l, w):
    B, T, _ = q_lat.shape
    qn = _rms_norm(q_lat, w['mla_g_q'][l])
    q = jnp.einsum('btr,rhe->bthe', qn, w['mla_w_uq'][l])
    q_nope = q[..., :MLA_NOPE]
    q_rope = _rope(q[..., MLA_NOPE:], pos)
    c_new = _rms_norm(kv_lat, w['mla_g_kv'][l])
    kr_new = _rope(k_rope, pos)
    q_abs = jnp.einsum('bthn,lhn->bthl', q_nope, w['mla_w_uk'][l])
    segs = [(c_new, kr_new, pos)]
    if past is not None:
        n_past = past['mla'].shape[1]
        segs = [(past['mla'][..., :MLA_KV_LORA], past['mla'][..., MLA_KV_LORA:],
                 jnp.arange(n_past, dtype=jnp.int32))] + segs
    scale = (MLA_NOPE + MLA_ROPE) ** -0.5

    def block(qa, qr, qp):
        logits = []
        for c_, kr_, kp in segs:
            s = (jnp.einsum('bqhl,bkl->bhqk', qa, c_, preferred_element_type=F32)
                 + jnp.einsum('bqhr,bkr->bhqk', qr, kr_, preferred_element_type=F32)) * scale
            logits.append(jnp.where(kp[None, :] <= qp[:, None], s, NEG))
        return _attend(logits, [sg[0] for sg in segs], 'bhqk,bkl->bqhl')
    o_lat = _map_query_blocks(block, (q_abs, q_rope), pos)
    o = jnp.einsum('bthl,lhv->bthv', o_lat, w['mla_w_uv'][l]).reshape(B, T, MLA_HEADS * MLA_V)
    return o, jnp.concatenate([c_new, kr_new], axis=-1)


def _s5(u, past, l, w):
    B, T, _ = u.shape
    lam_re = w['s5_lambda_re'][l]
    lam_im = w['s5_lambda_im'][l]
    dt = jnp.exp(w['s5_log_dt'][l])[:, None]
    mag = jnp.exp(lam_re * dt)
    a_re, a_im = mag * jnp.cos(lam_im * dt), mag * jnp.sin(lam_im * dt)
    den = lam_re * lam_re + lam_im * lam_im
    coef_re = ((a_re - 1.0) * lam_re + a_im * lam_im) / den
    coef_im = (a_im * lam_re - (a_re - 1.0) * lam_im) / den
    b_re, b_im = w['s5_b_re'][l], w['s5_b_im'][l]
    bb_re = coef_re[..., None] * b_re - coef_im[..., None] * b_im
    bb_im = coef_re[..., None] * b_im + coef_im[..., None] * b_re
    ug = u.reshape(B, T, S5_GROUPS, S5_GROUP)
    bu_re = jnp.einsum('btgi,gpi->btgp', ug, bb_re)
    bu_im = jnp.einsum('btgi,gpi->btgp', ug, bb_im)
    if past is not None:
        x0_re, x0_im = past['s5'][:, 0], past['s5'][:, 1]
        bu_re = bu_re.at[:, 0].add(a_re * x0_re - a_im * x0_im)
        bu_im = bu_im.at[:, 0].add(a_re * x0_im + a_im * x0_re)
    A_re = jnp.broadcast_to(a_re, bu_re.shape)
    A_im = jnp.broadcast_to(a_im, bu_im.shape)
    _, _, x_re, x_im = lax.associative_scan(_complex_affine_combine, (A_re, A_im, bu_re, bu_im), axis=1)
    y = (jnp.einsum('btgp,gip->btgi', x_re, w['s5_c_re'][l])
         - jnp.einsum('btgp,gip->btgi', x_im, w['s5_c_im'][l]))
    y = y.reshape(B, T, MIX_WIDTH) + w['s5_d'][l] * u
    y = jax.nn.gelu(y)
    y = y * jax.nn.sigmoid(jnp.einsum('btw,wv->btv', y, w['s5_w_glu'][l]))
    state = jnp.stack([x_re[:, -1], x_im[:, -1]], axis=1)
    return y, state


def _fox(q, k, v, f_logit, pos, past, l, w):
    B, T, _ = q.shape
    q = q.reshape(B, T, FOX_HEADS, HEAD_DIM)
    k = k.reshape(B, T, FOX_HEADS, HEAD_DIM)
    v = v.reshape(B, T, FOX_HEADS, HEAD_DIM)
    logf = jax.nn.log_sigmoid(f_logit + w['fox_b_f'][l])
    if past is None:
        F_new = jnp.cumsum(logf, axis=1)
        segs = [(k, v, F_new, pos)]
    else:
        F_past = jnp.cumsum(past['fox_logf'], axis=1)
        F_new = F_past[:, -1:] + jnp.cumsum(logf, axis=1)
        segs = [(past['fox_k'], past['fox_v'], F_past, jnp.arange(F_past.shape[1], dtype=jnp.int32)),
                (k, v, F_new, pos)]
    scale = HEAD_DIM ** -0.5

    def block(qq, fq, qp):
        fq_t = jnp.swapaxes(fq, 1, 2)[..., None]
        logits = []
        for k_, _, fk, kp in segs:
            s = (jnp.einsum('bqhd,bkhd->bhqk', qq, k_, preferred_element_type=F32) * scale
                 + (fq_t - jnp.swapaxes(fk, 1, 2)[:, :, None, :]))
            logits.append(jnp.where(kp[None, :] <= qp[:, None], s, NEG))
        return _attend(logits, [sg[1] for sg in segs], 'bhqk,bkhd->bqhd')
    o = _map_query_blocks(block, (q, F_new), pos)
    return o.reshape(B, T, MIX_WIDTH), k, v, logf


def _nsa(q, kc, vc, ks, vs, kw, vw, g_logit, pos, past, l, w):
    B, T, _ = q.shape
    q = q.reshape(B, T, NSA_HEADS, HEAD_DIM)
    g = jax.nn.sigmoid(g_logit).reshape(B, T, 3, NSA_HEADS)
    rows = jnp.stack([kc, vc, ks, vs], axis=2)
    win_rows = jnp.stack([kw, vw], axis=2)
    if past is None:
        kc_all, vc_all, ks_all, vs_all = kc, vc, ks, vs
        win_ctx = win_rows
        keep = min(NSA_WINDOW, T)
    else:
        pr = past['nsa']
        kc_all = jnp.concatenate([pr[:, :, 0], kc], axis=1)
        vc_all = jnp.concatenate([pr[:, :, 1], vc], axis=1)
        ks_all = jnp.concatenate([pr[:, :, 2], ks], axis=1)
        vs_all = jnp.concatenate([pr[:, :, 3], vs], axis=1)
        win_ctx = jnp.concatenate([past['win'], win_rows], axis=1)
        keep = past['win'].shape[1]
    n_keys = kc_all.shape[1]
    win_base = n_keys - win_ctx.shape[1]
    n_cb = -(-n_keys // NSA_CMP_BLOCK)
    n_sb = -(-n_keys // NSA_SEL_BLOCK)
    ratio = NSA_SEL_BLOCK // NSA_CMP_BLOCK
    k_sel = min(NSA_TOPK, n_sb)

    def pad_rows(a, n):
        return jnp.pad(a, ((0, 0), (0, n - a.shape[1]), (0, 0)))
    kcb = jnp.einsum('bnid,i->bnd', pad_rows(kc_all, n_cb * NSA_CMP_BLOCK).reshape(B, n_cb, NSA_CMP_BLOCK, HEAD_DIM),
                     w['nsa_w_cmp_k'][l])
    vcb = jnp.einsum('bnid,i->bnd', pad_rows(vc_all, n_cb * NSA_CMP_BLOCK).reshape(B, n_cb, NSA_CMP_BLOCK, HEAD_DIM),
                     w['nsa_w_cmp_v'][l])
    cb_end = jnp.arange(n_cb, dtype=jnp.int32) * NSA_CMP_BLOCK + (NSA_CMP_BLOCK - 1)
    sb_start = jnp.arange(n_sb, dtype=jnp.int32) * NSA_SEL_BLOCK
    ks_pad = pad_rows(ks_all, n_sb * NSA_SEL_BLOCK)
    vs_pad = pad_rows(vs_all, n_sb * NSA_SEL_BLOCK)
    win_pad = jnp.pad(win_ctx, ((0, 0), (NSA_WINDOW, 0), (0, 0), (0, 0)))
    rel_bias = w['rel_bias']
    scale = HEAD_DIM ** -0.5

    def block(qq, gb, qp):
        nq = qp.shape[0]
        lc = (jnp.einsum('bqhd,bnd->bhqn', qq, kcb, preferred_element_type=F32) * scale
              + _t5_bias(rel_bias, qp[:, None] - cb_end[None, :]))
        pc = _masked_softmax(lc, cb_end[None, :] <= qp[:, None])
        o_c = jnp.einsum('bhqn,bnd->bqhd', pc, vcb)
        imp = jnp.pad(pc.sum(axis=1), ((0, 0), (0, 0), (0, n_sb * ratio - n_cb)))
        imp = imp.reshape(B, nq, n_sb, ratio).sum(-1)
        cur = (qp[:, None] // NSA_SEL_BLOCK) == jnp.arange(n_sb, dtype=jnp.int32)[None, :]
        score = jnp.where(cur, FORCE, jnp.where(sb_start[None, :] <= qp[:, None], imp, NEG))
        top_v, top_i = lax.top_k(score, k_sel)
        idx = (top_i[..., None] * NSA_SEL_BLOCK + jnp.arange(NSA_SEL_BLOCK, dtype=jnp.int32)).reshape(
            B, nq, k_sel * NSA_SEL_BLOCK)
        valid = jnp.repeat(top_v > NEG / 2, NSA_SEL_BLOCK, axis=-1) & (idx <= qp[None, :, None])
        ks_g = jax.vmap(lambda a, i: a[i])(ks_pad, idx)
        vs_g = jax.vmap(lambda a, i: a[i])(vs_pad, idx)
        ls = (jnp.einsum('bqhd,bqkd->bhqk', qq, ks_g, preferred_element_type=F32) * scale
              + _t5_bias(rel_bias, qp[None, :, None] - idx))
        ps = _masked_softmax(ls, valid[:, None])
        o_s = jnp.einsum('bhqk,bqkd->bqhd', ps, vs_g)
        start = qp[0] - win_base
        wk = lax.dynamic_slice_in_dim(win_pad, start, NSA_WINDOW + nq, axis=1)
        wp = qp[0] - NSA_WINDOW + jnp.arange(NSA_WINDOW + nq, dtype=jnp.int32)
        mask_w = ((wp[None, :] >= 0) & (wp[None, :] <= qp[:, None])
                  & (qp[:, None] - wp[None, :] <= NSA_WINDOW))
        lw = (jnp.einsum('bqhd,bkd->bhqk', qq, wk[:, :, 0], preferred_element_type=F32) * scale
              + _t5_bias(rel_bias, qp[:, None] - wp[None, :]))
        pw = jax.nn.softmax(jnp.where(mask_w, lw, NEG), axis=-1)
        o_w = jnp.einsum('bhqk,bkd->bqhd', pw, wk[:, :, 1])
        return (gb[:, :, 0, :, None] * o_c + gb[:, :, 1, :, None] * o_s
                + gb[:, :, 2, :, None] * o_w)
    o = _map_query_blocks(block, (q, g), pos)
    return o.reshape(B, T, MIX_WIDTH), rows, win_ctx[:, win_ctx.shape[1] - keep:]


def _zcol(z, B, T, start, width):
    return z[:, start:start + width].reshape(B, T, width)


def _mixers_prompt(z, B, T, l, w, wc):
    col = functools.partial(_zcol, z, B, T)
    qcat, kf, kb = _mla_prep(z, wc['rope_c'], wc['rope_s'], wc['mla_gq'][l], wc['mla_gkv'][l],
                             wc['mla_w1'][l], wc['mla_wuk'][l], T, 512)
    y_a = _mla_attn(qcat, kb, wc['mla_wuv'][l], B, T)
    mla_rows = kf[:, :MLA_KV_LORA + MLA_ROPE].reshape(B, T, MLA_KV_LORA + MLA_ROPE)

    logf = jax.nn.log_sigmoid(col(Z_NG + 3 * NSA_HEADS, FOX_HEADS) + w['fox_b_f'][l])
    fsum = jnp.cumsum(logf, axis=1)
    fq = jnp.pad(fsum.reshape(B * T, FOX_HEADS), ((0, 0), (0, LANE - FOX_HEADS)))
    fk = jnp.pad(jnp.swapaxes(fsum.reshape(B, T // TQ, TQ, FOX_HEADS), 2, 3),
                 ((0, 0), (0, 0), (0, 8 - FOX_HEADS), (0, 0)))
    y_c = _fox_attn(z, fq, fk, B, T)
    fox_k = col(Z_FK, MIX_WIDTH).reshape(B, T, FOX_HEADS, HEAD_DIM)
    fox_v = col(Z_FV, MIX_WIDTH).reshape(B, T, FOX_HEADS, HEAD_DIM)

    y_d = _nsa_attn(z, wc['nsa_wkv'][l], wc['nsa_tb'], wc['nsa_bc'], B, T)
    nsa_rows = col(Z_KCVC, 2 * LANE).reshape(B, T, 4, HEAD_DIM)
    keep = min(NSA_WINDOW, T)
    win_state = col(Z_KWVW, LANE)[:, T - keep:].reshape(B, keep, 2, HEAD_DIM)

    y_b, s5_state = _s5_prompt(col(Z_S5U, MIX_WIDTH), w, l)
    return y_a, y_b, y_c, y_d, (mla_rows, fox_k, fox_v, logf, nsa_rows, win_state, s5_state)


def _head_rows(a):
    return jnp.pad(a, ((0, 0), (0, SUB - a.shape[1])) + ((0, 0),) * (a.ndim - 2))


def _mixers_decode(z, l, w, wc, caches):
    (cache_mla, cache_fox_k, cache_fox_v, cache_fox_logf, cache_nsa, state_nsa_win, state_s5, page_table) = caches
    nb, n_pages = page_table.shape
    pool = cache_mla.shape[1]
    assert cache_mla.shape[2] == PAGE_ROWS and n_pages % PAGES_PER_STEP == 0
    dec = wc['dec']
    col = lambda start, width: z[:, start:start + width]

    qcat, kf, _ = _mla_prep(z, dec['rope_c'], dec['rope_s'], wc['mla_gq'][l], wc['mla_gkv'][l],
                            wc['mla_w1'][l], wc['mla_wuk'][l], nb, nb)
    def feat_major(c):
        c = jnp.moveaxis(c, 2, -1)
        return c.reshape(c.shape[0], c.shape[1], -1, c.shape[-1])
    o_lat = _mla_dec(l, page_table, feat_major(cache_mla), _head_rows(qcat.reshape(nb, N_HEADS, MLA_QW)),
                     kf.reshape(nb, 1, MLA_QW))
    y_a = _rowmm(o_lat[:, :N_HEADS].reshape(nb, N_HEADS * MLA_KV_LORA),
                 wc['mla_wuv'][l].reshape(N_HEADS * MLA_KV_LORA, MIX_WIDTH))
    mla_rows = kf[:, :MLA_KV_LORA + MLA_ROPE].reshape(nb, 1, MLA_KV_LORA + MLA_ROPE)

    head_of_lane = jnp.arange(MIX_WIDTH, dtype=jnp.int32) // HEAD_DIM
    fq = col(Z_FQ, MIX_WIDTH) * SCALE_64
    fq = jnp.where(head_of_lane[None, None, :] == jnp.arange(N_HEADS, dtype=jnp.int32)[None, :, None],
                   fq[:, None, :], 0.0)
    logf = jax.nn.log_sigmoid(col(Z_NG + 3 * NSA_HEADS, FOX_HEADS) + w['fox_b_f'][l])
    fnew = jnp.broadcast_to(_head_rows(logf[:, :, None]), (nb, SUB, LANE))
    y_c = _fox_dec(l, page_table, feat_major(cache_fox_k), feat_major(cache_fox_v), feat_major(cache_fox_logf),
                   _head_rows(fq).astype(BF16), col(Z_FK, MIX_WIDTH).reshape(nb, 1, MIX_WIDTH),
                   col(Z_FV, MIX_WIDTH).reshape(nb, 1, MIX_WIDTH), fnew, dec['fox_dt'])
    fox_k = col(Z_FK, MIX_WIDTH).reshape(nb, 1, FOX_HEADS, HEAD_DIM)
    fox_v = col(Z_FV, MIX_WIDTH).reshape(nb, 1, FOX_HEADS, HEAD_DIM)

    nq = (col(Z_NQ, MIX_WIDTH) * SCALE_64).reshape(nb, N_HEADS, HEAD_DIM)
    nq = _head_rows(jnp.pad(nq, ((0, 0), (0, 0), (0, LANE - HEAD_DIM)))).astype(BF16)
    nsa_t = feat_major(cache_nsa)
    key = jnp.arange(PAGE_ROWS, dtype=jnp.int32)
    in_block = key[None, :] // NSA_CMP_BLOCK == jnp.arange(SUB, dtype=jnp.int32)[:, None]
    wt = jnp.stack([jnp.where(in_block, w[name][l][key % NSA_CMP_BLOCK][None, :], 0.0)
                    for name in ('nsa_w_cmp_k', 'nsa_w_cmp_v')])
    o_c, idx = _nsa_cmp_dec(l, page_table, nsa_t, nq, wt, dec['nsa_bc'], dec['nsa_pair'])
    sel_id = idx[:, :N_SEL_PAST, 0]
    sel_page = jnp.take_along_axis(page_table, sel_id // (PAGE_ROWS // NSA_SEL_BLOCK), axis=1)
    gates = jax.nn.sigmoid(col(Z_NG, 3 * NSA_HEADS)).reshape(nb, 3, NSA_HEADS, 1)
    gates = jnp.broadcast_to(jnp.pad(gates, ((0, 0), (0, 0), (0, SUB - NSA_HEADS), (0, 0))), (nb, 3, SUB, LANE))
    wnew = col(Z_KWVW, LANE)
    win_past = feat_major(state_nsa_win)
    y_d = _nsa_sel_dec(l, sel_page, sel_id, nsa_t, nq, o_c, col(Z_KSVS, LANE).reshape(nb, 1, LANE), wnew.reshape(nb, 1, LANE), gates, win_past,
                       dec['nsa_bsel'], dec['nsa_bwin'], dec['nsa_b0'])
    nsa_rows = col(Z_KCVC, 2 * LANE).reshape(nb, 1, 4, HEAD_DIM)
    win_state = jnp.concatenate([state_nsa_win[l][:, 1:], wnew.reshape(nb, 1, 2, HEAD_DIM)], axis=1)

    y_b, s5_state = _s5(col(Z_S5U, MIX_WIDTH).reshape(nb, 1, MIX_WIDTH), {'s5': state_s5[l]}, l, w)
    return y_a, y_b, y_c, y_d, (mla_rows, fox_k, fox_v, logf.reshape(nb, 1, FOX_HEADS), nsa_rows, win_state,
                                s5_state)


def _decode_tables(rel_bias, past_len, nb):
    half = MLA_ROPE // 2
    freq = ROPE_THETA ** (-jnp.arange(half, dtype=F32) / half)
    ang = jnp.full((nb, 1), past_len, jnp.int32).astype(F32) * freq[None, :]
    cos, sin = jnp.cos(ang), jnp.sin(ang)
    lane_pad = ((0, 0), (0, LANE - MLA_ROPE))

    def tab(dist):
        oh = jax.nn.one_hot(_rel_bucket(dist), REL_BUCKETS, dtype=F32)
        return _head_rows(jnp.einsum('rcb,bh->hrc', oh, rel_bias.astype(F32), precision=HIGHEST)[None])[0]
    n_cb = past_len // NSA_CMP_BLOCK
    n_sb = past_len // NSA_SEL_BLOCK
    cb_end = jnp.arange(n_cb, dtype=jnp.int32) * NSA_CMP_BLOCK + (NSA_CMP_BLOCK - 1)
    key_pos = (jnp.arange(n_sb, dtype=jnp.int32)[:, None] * NSA_SEL_BLOCK
               + jnp.arange(NSA_SEL_BLOCK, dtype=jnp.int32)[None, :])
    win_pos = past_len - NSA_WINDOW + jnp.arange(NSA_WINDOW, dtype=jnp.int32)
    j = np.arange(PAGE_ROWS)[:, None]
    k = np.arange(PAGE_ROWS)[None, :]
    r = np.arange(n_cb)[:, None] // (NSA_SEL_BLOCK // NSA_CMP_BLOCK)
    c = np.arange(n_sb)[None, :]
    return {'rope_c': jnp.pad(jnp.concatenate([cos, cos], axis=-1), lane_pad),
            'rope_s': jnp.pad(jnp.concatenate([-sin, sin], axis=-1), lane_pad),
            'fox_dt': jnp.asarray((j <= k).astype(np.float32)),
            'nsa_pair': jnp.asarray((r == c).astype(np.float32)),
            'nsa_bc': tab((past_len - cb_end)[None, :])[:, 0, :],
            'nsa_bsel': jnp.tile(jnp.moveaxis(tab(past_len - key_pos), 0, 1), (1, 1, PAGE_ROWS // NSA_SEL_BLOCK)),
            'nsa_bwin': tab((past_len - win_pos)[None, :])[:, 0, :],
            'nsa_b0': jnp.broadcast_to(tab(jnp.zeros((1, 1), jnp.int32))[:, 0, :], (SUB, LANE))}


def _prep_weights(w_in, mla_g_q, mla_g_kv, mla_w_uq, mla_w_uk, mla_w_uv, nsa_w_cmp_k, nsa_w_cmp_v, rel_bias, T):
    def cols(a, b):
        return w_in[:, :, a:b]

    def zeros(n):
        return jnp.zeros((DEPTH, D_MODEL, n), w_in.dtype)
    o = [0] + [int(v) for v in np.cumsum(IN_SPLITS)]
    (o_ql, o_kv, o_kr, o_s5, o_fq, o_fk, o_fv, o_ff, o_nq, o_kc, o_vc, o_ks, o_vs, o_kw, o_vw, o_ng, o_gate,
     o_end) = o
    half = MLA_ROPE // 2
    w_in_r = jnp.concatenate([
        cols(o_gate, o_end), cols(o_fq, o_fk), cols(o_fk, o_fv), cols(o_fv, o_ff), cols(o_nq, o_kc),
        cols(o_s5, o_fq), cols(o_ql, o_kv), zeros(2 * LANE - MLA_Q_LORA), cols(o_kv, o_kr),
        cols(o_kr, o_s5), zeros(LANE - MLA_ROPE),
        cols(o_kr + half, o_s5), cols(o_kr, o_kr + half), zeros(LANE - MLA_ROPE),
        cols(o_kc, o_ks), cols(o_ks, o_kw), cols(o_kw, o_ng), cols(o_ng, o_gate), cols(o_ff, o_nq),
        zeros(LANE - 3 * NSA_HEADS - FOX_HEADS)], axis=-1)
    assert w_in_r.shape[-1] == Z_COLS

    def rope_cols(r):
        return jnp.pad(r, ((0, 0), (0, 0), (0, 0), (0, LANE - MLA_ROPE))).reshape(DEPTH, MLA_Q_LORA, N_HEADS * LANE)
    nope = mla_w_uq[..., :MLA_NOPE].reshape(DEPTH, MLA_Q_LORA, MIX_WIDTH)
    rope = mla_w_uq[..., MLA_NOPE:]
    rope_sw = jnp.concatenate([rope[..., half:], rope[..., :half]], axis=-1)
    w1 = jnp.concatenate([nope, rope_cols(rope), rope_cols(rope_sw)], axis=-1)
    w1 = jnp.pad(w1, ((0, 0), (0, 2 * LANE - MLA_Q_LORA), (0, 0)))
    eye = jnp.eye(N_HEADS, dtype=mla_w_uk.dtype)
    wuk = jnp.einsum('dlhn,hg->dhngl', mla_w_uk, eye).reshape(DEPTH, MIX_WIDTH, N_HEADS * MLA_KV_LORA)
    wuv = jnp.einsum('dlhv,hg->dhlgv', mla_w_uv, eye).reshape(DEPTH, N_HEADS, MLA_KV_LORA, MIX_WIDTH)

    freq = ROPE_THETA ** (-jnp.arange(half, dtype=F32) / half)
    ang = jnp.arange(T, dtype=jnp.int32).astype(F32)[:, None] * freq[None, :]
    cos, sin = jnp.cos(ang), jnp.sin(ang)
    lane_pad = ((0, 0), (0, LANE - MLA_ROPE))
    wkv = jnp.concatenate([jnp.broadcast_to(nsa_w_cmp_k[:, :, None], (DEPTH, NSA_CMP_BLOCK, HEAD_DIM)),
                           jnp.broadcast_to(nsa_w_cmp_v[:, :, None], (DEPTH, NSA_CMP_BLOCK, HEAD_DIM))], axis=-1)
    tb, bc = _nsa_tables(rel_bias, T)
    return {'w_in': w_in_r.astype(BF16),
            'mla_w1': w1.astype(BF16), 'mla_wuk': wuk.astype(BF16), 'mla_wuv': wuv.astype(BF16),
            'mla_gq': jnp.pad(mla_g_q, ((0, 0), (0, 2 * LANE - MLA_Q_LORA))).reshape(DEPTH, 1, 2 * LANE),
            'mla_gkv': mla_g_kv.reshape(DEPTH, 1, MLA_KV_LORA),
            'rope_c': jnp.pad(jnp.concatenate([cos, cos], axis=-1), lane_pad),
            'rope_s': jnp.pad(jnp.concatenate([-sin, sin], axis=-1), lane_pad),
            'nsa_wkv': wkv.astype(F32), 'nsa_tb': tb, 'nsa_bc': bc}


def _forward(x, mods, pos, w, wc, caches, tm):
    B, T, _ = x.shape
    n = B * T
    xf = x.reshape(n, D_MODEL)
    per_row = T == 1
    outs = [[] for _ in range(7)]
    for l in range(DEPTH):
        m6 = mods[l].reshape(B, 6, D_MODEL)
        if per_row:
            sh1, sc1, g1, sh2, sc2, g2 = (m6[:, i].reshape(1, n, D_MODEL) for i in range(6))
        else:
            sh1, sc1, g1, sh2, sc2, g2 = (m6[:, i].reshape(B, 1, D_MODEL) for i in range(6))
        gains = w['norm_gains'][l]
        z = _inproj(xf, gains[0], sc1, sh1, wc['w_in'][l], tm)
        if caches is None:
            y_a, y_b, y_c, y_d, states = _mixers_prompt(z, B, T, l, w, wc)
        else:
            y_a, y_b, y_c, y_d, states = _mixers_decode(z, l, w, wc, caches)
        mla_rows, fox_k, fox_v, fox_logf, nsa_rows, win_state, s5_state = states
        ys = tuple(y.reshape(n, MIX_WIDTH) for y in (y_a, y_b, y_c, y_d))
        xf = _merge(ys, z, xf, wc['w_branch'][l], wc['w_out'][l], gains[1], g1, tm)
        xf = _ffn(xf, gains[2], sc2, sh2, wc['w_ffn_gate'][l], wc['w_ffn_up'][l], wc['w_ffn_down'][l],
                  gains[3], g2, tm)
        for o, s in zip(outs, (mla_rows, fox_k, fox_v, fox_logf, nsa_rows, win_state, s5_state)):
            o.append(s)
    return xf.reshape(B, T, D_MODEL), tuple(jnp.stack(o) for o in outs)


def kernel(x_prompt, x_sample, c_prompt, c_sample, cache_mla, cache_fox_k, cache_fox_v, cache_fox_logf, cache_nsa, state_nsa_win, state_s5, page_table, w_ada, b_ada, norm_gains, w_in, mla_g_q, mla_g_kv, mla_w_uq, mla_w_uk, mla_w_uv, s5_lambda_re, s5_lambda_im, s5_log_dt, s5_b_re, s5_b_im, s5_c_re, s5_c_im, s5_d, s5_w_glu, fox_b_f, nsa_w_cmp_k, nsa_w_cmp_v, rel_bias, w_branch, w_out, w_ffn_gate, w_ffn_up, w_ffn_down):
    w = {'norm_gains': norm_gains,
         'mla_g_q': mla_g_q, 'mla_g_kv': mla_g_kv, 'mla_w_uq': mla_w_uq, 'mla_w_uk': mla_w_uk,
         'mla_w_uv': mla_w_uv, 's5_lambda_re': s5_lambda_re, 's5_lambda_im': s5_lambda_im,
         's5_log_dt': s5_log_dt, 's5_b_re': s5_b_re, 's5_b_im': s5_b_im, 's5_c_re': s5_c_re,
         's5_c_im': s5_c_im, 's5_d': s5_d, 's5_w_glu': s5_w_glu, 'fox_b_f': fox_b_f,
         'nsa_w_cmp_k': nsa_w_cmp_k, 'nsa_w_cmp_v': nsa_w_cmp_v, 'rel_bias': rel_bias}
    wc = _prep_weights(w_in, mla_g_q, mla_g_kv, mla_w_uq, mla_w_uk, mla_w_uv, nsa_w_cmp_k, nsa_w_cmp_v, rel_bias,
                       x_prompt.shape[1])
    wc.update({'w_branch': w_branch.astype(BF16), 'w_out': w_out.astype(BF16),
               'w_ffn_gate': w_ffn_gate.astype(BF16), 'w_ffn_up': w_ffn_up.astype(BF16),
               'w_ffn_down': w_ffn_down.astype(BF16)})
    wc['dec'] = _decode_tables(rel_bias, page_table.shape[1] * cache_mla.shape[2], x_sample.shape[0])
    nb_p = c_prompt.shape[0]
    mods = _ada(jnp.concatenate([c_prompt, c_sample], axis=0), w_ada, b_ada)
    past_len = page_table.shape[1] * cache_mla.shape[2]
    pos_p = jnp.arange(x_prompt.shape[1], dtype=jnp.int32)
    pos_s = past_len + jnp.arange(x_sample.shape[1], dtype=jnp.int32)
    caches = (cache_mla, cache_fox_k, cache_fox_v, cache_fox_logf, cache_nsa,
              state_nsa_win, state_s5, page_table)
    y_prompt, st_p = _forward(x_prompt, mods[:, :nb_p], pos_p, w, wc, None, 512)
    y_sample, st_s = _forward(x_sample, mods[:, nb_p:], pos_s, w, wc, caches, x_sample.shape[0])
    mla_p, fox_k_p, fox_v_p, fox_logf_p, nsa_p, win_p, s5_p = st_p
    mla_s, fox_k_s, fox_v_s, fox_logf_s, nsa_s, win_s, s5_s = st_s
    return (y_prompt, y_sample, mla_p, mla_s, fox_k_p, fox_k_s, fox_v_p, fox_v_s,
            fox_logf_p, fox_logf_s, nsa_p, nsa_s, win_p, win_s, s5_p, s5_s)
```

```python
import functools
import math

import numpy as np
import jax
import jax.numpy as jnp
from jax import lax
from jax.experimental import pallas as pl
from jax.experimental.pallas import tpu as pltpu

F32 = jnp.float32
BF16 = jnp.bfloat16

D_MODEL = 1024
DEPTH = 2
HEAD_DIM = 64
MIX_WIDTH = D_MODEL // 4
N_BRANCH = 4
Q_BLOCK = 128
PAGE_ROWS = 128
EPS = 1e-6
NEG = -1e30
FORCE = 1e30

MLA_HEADS = MIX_WIDTH // HEAD_DIM
MLA_Q_LORA = 3 * D_MODEL // 16
MLA_KV_LORA = D_MODEL // 8
MLA_NOPE = HEAD_DIM
MLA_ROPE = HEAD_DIM // 2
MLA_V = HEAD_DIM
ROPE_THETA = 10000.0

S5_GROUP = 16
S5_GROUPS = MIX_WIDTH // S5_GROUP
S5_STATE = 64

FOX_HEADS = MIX_WIDTH // HEAD_DIM

NSA_HEADS = MIX_WIDTH // HEAD_DIM
NSA_CMP_BLOCK = 32
NSA_SEL_BLOCK = 64
NSA_TOPK = 16
NSA_WINDOW = 512

REL_BUCKETS = 32
REL_MAX_DIST = 128

FFN_HIDDEN = (-(-8 * D_MODEL // 3) + 255) // 256 * 256

IN_SPLITS = (MLA_Q_LORA, MLA_KV_LORA, MLA_ROPE,
             MIX_WIDTH,
             MIX_WIDTH, MIX_WIDTH, MIX_WIDTH, FOX_HEADS,
             MIX_WIDTH, HEAD_DIM, HEAD_DIM, HEAD_DIM, HEAD_DIM, HEAD_DIM, HEAD_DIM, 3 * NSA_HEADS,
             N_BRANCH * D_MODEL)
IN_COLS = sum(IN_SPLITS)
GATE_COLS = N_BRANCH * D_MODEL
MIX_COLS = IN_COLS - GATE_COLS
MIX_SPLIT_POINTS = tuple(int(v) for v in np.cumsum(IN_SPLITS[:-1])[:-1])

LANE = 128
VMEM_LIMIT = 48 * 1024 * 1024

Z_GATE = 0
Z_FQ = Z_GATE + GATE_COLS
Z_FK = Z_FQ + MIX_WIDTH
Z_FV = Z_FK + MIX_WIDTH
Z_NQ = Z_FV + MIX_WIDTH
Z_S5U = Z_NQ + MIX_WIDTH
Z_QLAT = Z_S5U + MIX_WIDTH
Z_KVLAT = Z_QLAT + 2 * LANE
Z_KROPE = Z_KVLAT + LANE
Z_KROPE_SW = Z_KROPE + LANE
Z_KCVC = Z_KROPE_SW + LANE
Z_KSVS = Z_KCVC + LANE
Z_KWVW = Z_KSVS + LANE
Z_NG = Z_KWVW + LANE
Z_COLS = Z_NG + LANE
Z_TILE = Z_COLS // 3


def _rms(x, g):
    return x * lax.rsqrt(jnp.mean(x * x, axis=-1, keepdims=True) + EPS) * g


def _ada_kernel(c_ref, w_ref, b_ref, o_ref):
    c = c_ref[...]
    a = (c * jax.nn.sigmoid(c)).astype(BF16)
    o_ref[0] = jnp.dot(a, w_ref[0], preferred_element_type=F32) + b_ref[0]


def _ada(c_all, w_ada, b_ada):
    rows = c_all.shape[0]
    tn = 1536
    return pl.pallas_call(
        _ada_kernel,
        grid=(DEPTH, 6 * D_MODEL // tn),
        in_specs=[pl.BlockSpec((rows, D_MODEL), lambda l, j: (0, 0)),
                  pl.BlockSpec((1, D_MODEL, tn), lambda l, j: (l, 0, j)),
                  pl.BlockSpec((1, 1, tn), lambda l, j: (l, 0, j))],
        out_specs=pl.BlockSpec((1, rows, tn), lambda l, j: (l, 0, j)),
        out_shape=jax.ShapeDtypeStruct((DEPTH, rows, 6 * D_MODEL), F32),
        compiler_params=pltpu.CompilerParams(dimension_semantics=("arbitrary", "arbitrary"),
                                             vmem_limit_bytes=VMEM_LIMIT),
        name="ada_mod",
    )(c_all, w_ada.astype(BF16), b_ada.reshape(DEPTH, 1, 6 * D_MODEL))


def _inproj_kernel(x_ref, gain_ref, sc_ref, sh_ref, w_ref, o_ref, h_ref):
    @pl.when(pl.program_id(1) == 0)
    def _():
        h = _rms(x_ref[...], gain_ref[...]) * (1.0 + sc_ref[0]) + sh_ref[0]
        h_ref[...] = h.astype(BF16)
    o_ref[...] = jnp.dot(h_ref[...], w_ref[...], preferred_element_type=F32)


def _inproj(x, gain, sc, sh, w, tm):
    n = x.shape[0]
    tiles_per_group = n // tm // sc.shape[0]
    tn = Z_TILE
    mod_spec = pl.BlockSpec((1, sc.shape[1], D_MODEL), lambda i, j: (i // tiles_per_group, 0, 0))
    return pl.pallas_call(
        _inproj_kernel,
        grid=(n // tm, Z_COLS // tn),
        in_specs=[pl.BlockSpec((tm, D_MODEL), lambda i, j: (i, 0)),
                  pl.BlockSpec((1, D_MODEL), lambda i, j: (0, 0)),
                  mod_spec, mod_spec,
                  pl.BlockSpec((D_MODEL, tn), lambda i, j: (0, j))],
        out_specs=pl.BlockSpec((tm, tn), lambda i, j: (i, j)),
        out_shape=jax.ShapeDtypeStruct((n, Z_COLS), F32),
        scratch_shapes=[pltpu.VMEM((tm, D_MODEL), BF16)],
        compiler_params=pltpu.CompilerParams(dimension_semantics=("parallel", "arbitrary"),
                                             vmem_limit_bytes=VMEM_LIMIT),
        name="in_proj",
    )(x, gain.reshape(1, D_MODEL), sc, sh, w)


def _merge_kernel(ya_ref, yb_ref, yc_ref, yd_ref, gl_ref, x_ref, wb_ref, wo_ref, gain_ref, g1_ref, o_ref):
    acc = None
    for n, y_ref in enumerate((ya_ref, yb_ref, yc_ref, yd_ref)):
        p = jnp.dot(y_ref[...].astype(BF16), wb_ref[n], preferred_element_type=F32)
        t = jax.nn.sigmoid(gl_ref[:, n * D_MODEL:(n + 1) * D_MODEL]) * p
        acc = t if acc is None else acc + t
    m = jnp.dot(acc.astype(BF16), wo_ref[...], preferred_element_type=F32)
    o_ref[...] = x_ref[...] + g1_ref[0] * _rms(m, gain_ref[...])


def _merge(ys, z, x, wb, wo, gain, g1, tm):
    n = x.shape[0]
    tiles_per_group = n // tm // g1.shape[0]
    y_spec = pl.BlockSpec((tm, MIX_WIDTH), lambda i: (i, 0))
    return pl.pallas_call(
        _merge_kernel,
        grid=(n // tm,),
        in_specs=[y_spec, y_spec, y_spec, y_spec,
                  pl.BlockSpec((tm, GATE_COLS), lambda i: (i, 0)),
                  pl.BlockSpec((tm, D_MODEL), lambda i: (i, 0)),
                  pl.BlockSpec((N_BRANCH, MIX_WIDTH, D_MODEL), lambda i: (0, 0, 0)),
                  pl.BlockSpec((D_MODEL, D_MODEL), lambda i: (0, 0)),
                  pl.BlockSpec((1, D_MODEL), lambda i: (0, 0)),
                  pl.BlockSpec((1, g1.shape[1], D_MODEL), lambda i: (i // tiles_per_group, 0, 0))],
        out_specs=pl.BlockSpec((tm, D_MODEL), lambda i: (i, 0)),
        out_shape=jax.ShapeDtypeStruct((n, D_MODEL), F32),
        compiler_params=pltpu.CompilerParams(dimension_semantics=("parallel",),
                                             vmem_limit_bytes=VMEM_LIMIT),
        name="merge",
    )(*ys, z, x, wb, wo, gain.reshape(1, D_MODEL), g1)


def _ffn_kernel(x_ref, gain_h_ref, sc_ref, sh_ref, wg_ref, wu_ref, wd_ref, gain_o_ref, g2_ref, o_ref,
                h_ref, acc_ref):
    j = pl.program_id(1)

    @pl.when(j == 0)
    def _():
        h = _rms(x_ref[...], gain_h_ref[...]) * (1.0 + sc_ref[0]) + sh_ref[0]
        h_ref[...] = h.astype(BF16)
        acc_ref[...] = jnp.zeros_like(acc_ref)

    h = h_ref[...]
    a = jnp.dot(h, wg_ref[...], preferred_element_type=F32)
    b = jnp.dot(h, wu_ref[...], preferred_element_type=F32)
    t = (a * jax.nn.sigmoid(a)) * b
    acc_ref[...] += jnp.dot(t.astype(BF16), wd_ref[...], preferred_element_type=F32)

    @pl.when(j == pl.num_programs(1) - 1)
    def _():
        o_ref[...] = x_ref[...] + g2_ref[0] * _rms(acc_ref[...], gain_o_ref[...])


def _ffn(x, gain_h, sc, sh, wg, wu, wd, gain_o, g2, tm):
    n = x.shape[0]
    tiles_per_group = n // tm // sc.shape[0]
    th = FFN_HIDDEN // 2
    mod_spec = pl.BlockSpec((1, sc.shape[1], D_MODEL), lambda i, j: (i // tiles_per_group, 0, 0))
    vec_spec = pl.BlockSpec((1, D_MODEL), lambda i, j: (0, 0))
    return pl.pallas_call(
        _ffn_kernel,
        grid=(n // tm, FFN_HIDDEN // th),
        in_specs=[pl.BlockSpec((tm, D_MODEL), lambda i, j: (i, 0)),
                  vec_spec, mod_spec, mod_spec,
                  pl.BlockSpec((D_MODEL, th), lambda i, j: (0, j)),
                  pl.BlockSpec((D_MODEL, th), lambda i, j: (0, j)),
                  pl.BlockSpec((th, D_MODEL), lambda i, j: (j, 0)),
                  vec_spec, mod_spec],
        out_specs=pl.BlockSpec((tm, D_MODEL), lambda i, j: (i, 0)),
        out_shape=jax.ShapeDtypeStruct((n, D_MODEL), F32),
        scratch_shapes=[pltpu.VMEM((tm, D_MODEL), BF16), pltpu.VMEM((tm, D_MODEL), F32)],
        compiler_params=pltpu.CompilerParams(dimension_semantics=("parallel", "arbitrary"),
                                             vmem_limit_bytes=VMEM_LIMIT),
        name="ffn",
    )(x, gain_h.reshape(1, D_MODEL), sc, sh, wg, wu, wd, gain_o.reshape(1, D_MODEL), g2)


TQ = 256
N_HEADS = 4
SCALE_64 = HEAD_DIM ** -0.5


def _dot_nt(a, b):
    return lax.dot_general(a, b, (((1,), (1,)), ((), ())), preferred_element_type=F32)


def _softmax_step(s, v, m_ref, l_ref, acc_ref):
    m_old = m_ref[...]
    m_new = jnp.maximum(m_old, jnp.max(s, axis=-1, keepdims=True))
    alpha = jnp.exp(m_old - m_new)
    p = jnp.exp(s - m_new)
    l_ref[...] = alpha * l_ref[...] + jnp.sum(p, axis=-1, keepdims=True)
    acc_ref[...] = alpha * acc_ref[...] + jnp.dot(p.astype(BF16), v, preferred_element_type=F32)
    m_ref[...] = m_new


def _softmax_reset(m_ref, l_ref, acc_ref):
    m_ref[...] = jnp.full(m_ref.shape, NEG, F32)
    l_ref[...] = jnp.zeros(l_ref.shape, F32)
    acc_ref[...] = jnp.zeros(acc_ref.shape, F32)


def _lanes(x, width):
    return x if width == LANE else jnp.concatenate([x] * (width // LANE), axis=1)


def _flash_step(s, v, m_ref, acc_ref):
    m_old = m_ref[...]
    m_new = jnp.maximum(m_old, jnp.max(s, axis=-1, keepdims=True))
    alpha = jnp.exp(m_old - m_new)
    p = jnp.exp(s - _lanes(m_new, s.shape[1])).astype(BF16)
    v1 = jnp.concatenate([v, jnp.ones((v.shape[0], LANE), BF16)], axis=1)
    acc_ref[...] = _lanes(alpha, acc_ref.shape[1]) * acc_ref[...] + jnp.dot(p, v1, preferred_element_type=F32)
    m_ref[...] = m_new


def _flash_reset(m_ref, acc_ref):
    m_ref[...] = jnp.full(m_ref.shape, NEG, F32)
    acc_ref[...] = jnp.zeros(acc_ref.shape, F32)


def _flash_out(acc_ref):
    acc = acc_ref[...]
    dv = acc.shape[1] - LANE
    return acc[:, :dv] / _lanes(acc[:, dv:], dv)


def _causal_neg(rows):
    i = lax.broadcasted_iota(jnp.int32, (TQ, TQ), 0)
    j = lax.broadcasted_iota(jnp.int32, (TQ, TQ), 1)
    neg = jnp.where(j <= i, 0.0, NEG).astype(F32)
    return jnp.concatenate([neg] * rows, axis=0)


MLA_QW = 2 * LANE


def _mla_prep_kernel(ql_ref, kv_ref, kr_ref, krs_ref, cq_ref, sq_ref, gq_ref, gkv_ref, w1_ref, wuk_ref,
                     qcat_ref, kf_ref, kb_ref):
    ql = ql_ref[...]
    qn = ql * lax.rsqrt(jnp.sum(ql * ql, axis=-1, keepdims=True) * (1.0 / MLA_Q_LORA) + EPS) * gq_ref[...]
    q1 = jnp.dot(qn.astype(BF16), w1_ref[...], preferred_element_type=F32)
    q_abs = jnp.dot(q1[:, :MIX_WIDTH].astype(BF16), wuk_ref[...], preferred_element_type=F32)
    cc, ss = cq_ref[...], sq_ref[...]
    parts = []
    for h in range(N_HEADS):
        r = q1[:, MIX_WIDTH + h * LANE:MIX_WIDTH + (h + 1) * LANE]
        rs = q1[:, MIX_WIDTH + (N_HEADS + h) * LANE:MIX_WIDTH + (N_HEADS + h + 1) * LANE]
        parts += [q_abs[:, h * LANE:(h + 1) * LANE], r * cc + rs * ss]
    qcat_ref[...] = jnp.concatenate(parts, axis=1).astype(BF16)
    c_new = _rms(kv_ref[...], gkv_ref[...])
    kr = kr_ref[...] * cc + krs_ref[...] * ss
    k = jnp.concatenate([c_new, kr], axis=1)
    kf_ref[...] = k
    kb_ref[...] = k.astype(BF16)


def _mla_prep(z, cq, sq, gq, gkv, w1, wuk, T, tm):
    n = z.shape[0]
    tiles_per_seq = T // tm
    row = lambda c: pl.BlockSpec((tm, LANE), lambda i: (i, c))
    tab = pl.BlockSpec((tm, LANE), lambda i: (i % tiles_per_seq, 0))
    full = lambda a: pl.BlockSpec(a.shape, lambda i: (0,) * a.ndim)
    return pl.pallas_call(
        _mla_prep_kernel,
        grid=(n // tm,),
        in_specs=[pl.BlockSpec((tm, 2 * LANE), lambda i: (i, Z_QLAT // (2 * LANE))),
                  row(Z_KVLAT // LANE), row(Z_KROPE // LANE), row(Z_KROPE_SW // LANE),
                  tab, tab, full(gq), full(gkv), full(w1), full(wuk)],
        out_specs=[pl.BlockSpec((tm, N_HEADS * MLA_QW), lambda i: (i, 0)),
                   pl.BlockSpec((tm, MLA_QW), lambda i: (i, 0)),
                   pl.BlockSpec((tm, MLA_QW), lambda i: (i, 0))],
        out_shape=[jax.ShapeDtypeStruct((n, N_HEADS * MLA_QW), BF16),
                   jax.ShapeDtypeStruct((n, MLA_QW), F32),
                   jax.ShapeDtypeStruct((n, MLA_QW), BF16)],
        compiler_params=pltpu.CompilerParams(dimension_semantics=("parallel",), vmem_limit_bytes=VMEM_LIMIT),
        name="mla_prep",
    )(z, z, z, z, cq, sq, gq, gkv, w1, wuk)


def _mla_attn_kernel(q_ref, k_ref, wuv_ref, o_ref, m_ref, acc_ref):
    qi = pl.program_id(1)
    q = q_ref[...]
    qs = jnp.concatenate([q[:, h * MLA_QW:(h + 1) * MLA_QW] for h in range(N_HEADS)], axis=0)
    scale = (MLA_NOPE + MLA_ROPE) ** -0.5
    _flash_reset(m_ref, acc_ref)

    def step(kt, neg):
        k = k_ref[pl.ds(pl.multiple_of(kt * TQ, TQ), TQ), :]
        s = _dot_nt(qs, k) * scale
        if neg is not None:
            s = s + neg
        _flash_step(s, k[:, :MLA_KV_LORA], m_ref, acc_ref)

    lax.fori_loop(0, qi, lambda kt, c: (step(kt, None), c)[1], 0)
    step(qi, _causal_neg(N_HEADS))
    o_lat = _flash_out(acc_ref).astype(BF16)
    out = None
    for h in range(N_HEADS):
        t = jnp.dot(o_lat[h * TQ:(h + 1) * TQ], wuv_ref[h], preferred_element_type=F32)
        out = t if out is None else out + t
    o_ref[...] = out


def _mla_attn(qcat, kb, wuv, B, T):
    return pl.pallas_call(
        _mla_attn_kernel,
        grid=(B, T // TQ),
        in_specs=[pl.BlockSpec((TQ, N_HEADS * MLA_QW), lambda b, i: (b * (T // TQ) + i, 0)),
                  pl.BlockSpec((T, MLA_QW), lambda b, i: (b, 0)),
                  pl.BlockSpec(wuv.shape, lambda b, i: (0, 0, 0))],
        out_specs=pl.BlockSpec((TQ, MIX_WIDTH), lambda b, i: (b * (T // TQ) + i, 0)),
        out_shape=jax.ShapeDtypeStruct((B * T, MIX_WIDTH), F32),
        scratch_shapes=[pltpu.VMEM((N_HEADS * TQ, LANE), F32),
                        pltpu.VMEM((N_HEADS * TQ, MLA_KV_LORA + LANE), F32)],
        compiler_params=pltpu.CompilerParams(dimension_semantics=("parallel", "arbitrary"),
                                             vmem_limit_bytes=VMEM_LIMIT),
        name="mla_attn",
    )(qcat, kb, wuv)


def _fox_attn_kernel(q_ref, k_ref, v_ref, fq_ref, fk_ref, o_ref, fqb_ref, m_ref, acc_ref):
    qi = pl.program_id(1)
    q = q_ref[...] * SCALE_64
    head = lax.broadcasted_iota(jnp.int32, (TQ, MIX_WIDTH), 1) // HEAD_DIM
    qs = jnp.concatenate([jnp.where(head == h, q, 0.0) for h in range(N_HEADS)], axis=0).astype(BF16)
    fq = fq_ref[...]
    for h in range(N_HEADS):
        fqb_ref[h * TQ:(h + 1) * TQ, :] = jnp.broadcast_to(fq[:, h:h + 1], (TQ, TQ))
    _flash_reset(m_ref, acc_ref)

    def step(kt, neg):
        off = pl.multiple_of(kt * TQ, TQ)
        k = k_ref[pl.ds(off, TQ), :].astype(BF16)
        v = v_ref[pl.ds(off, TQ), :].astype(BF16)
        fk = fk_ref[kt]
        fkb = jnp.concatenate([jnp.broadcast_to(fk[h:h + 1, :], (TQ, TQ)) for h in range(N_HEADS)], axis=0)
        s = _dot_nt(qs, k) + (fqb_ref[...] - fkb)
        if neg is not None:
            s = s + neg
        _flash_step(s, v, m_ref, acc_ref)

    lax.fori_loop(0, qi, lambda kt, c: (step(kt, None), c)[1], 0)
    step(qi, _causal_neg(N_HEADS))
    o = _flash_out(acc_ref)
    out = jnp.zeros((TQ, MIX_WIDTH), F32)
    for h in range(N_HEADS):
        out = jnp.where(head == h, o[h * TQ:(h + 1) * TQ], out)
    o_ref[...] = out


def _fox_attn(z, fq, fk, B, T):
    nq = T // TQ
    col = lambda c: pl.BlockSpec((T, MIX_WIDTH), lambda b, i: (b, c))
    return pl.pallas_call(
        _fox_attn_kernel,
        grid=(B, nq),
        in_specs=[pl.BlockSpec((TQ, MIX_WIDTH), lambda b, i: (b * nq + i, Z_FQ // MIX_WIDTH)),
                  col(Z_FK // MIX_WIDTH), col(Z_FV // MIX_WIDTH),
                  pl.BlockSpec((TQ, LANE), lambda b, i: (b * nq + i, 0)),
                  pl.BlockSpec((None, nq, 8, TQ), lambda b, i: (b, 0, 0, 0))],
        out_specs=pl.BlockSpec((TQ, MIX_WIDTH), lambda b, i: (b * nq + i, 0)),
        out_shape=jax.ShapeDtypeStruct((B * T, MIX_WIDTH), F32),
        scratch_shapes=[pltpu.VMEM((N_HEADS * TQ, TQ), F32),
                        pltpu.VMEM((N_HEADS * TQ, LANE), F32),
                        pltpu.VMEM((N_HEADS * TQ, MIX_WIDTH + LANE), F32)],
        compiler_params=pltpu.CompilerParams(dimension_semantics=("parallel", "arbitrary"),
                                             vmem_limit_bytes=VMEM_LIMIT),
        name="fox_attn",
    )(z, z, z, fq, fk)


def _nsa_tables(rel_bias, T):
    def tab(dist):
        oh = jax.nn.one_hot(_rel_bucket(dist), REL_BUCKETS, dtype=F32)
        b = jnp.einsum('rcb,bh->hrc', oh, rel_bias.astype(F32), precision=lax.Precision.HIGHEST)
        return b.reshape(N_HEADS * dist.shape[0], dist.shape[1])
    i = jnp.arange(TQ, dtype=jnp.int32)[:, None]
    j = jnp.arange(TQ, dtype=jnp.int32)[None, :]
    tile4 = lambda m: jnp.concatenate([m] * N_HEADS, axis=0)
    far = tab(jnp.full((TQ, TQ), REL_MAX_DIST, jnp.int32))
    prev = tab(TQ + i - j)
    diag = jnp.where(tile4(j <= i), tab(i - j), NEG)
    edge = jnp.where(tile4(j >= i), far, NEG)
    tb = jnp.stack([far, prev, diag, edge])
    n_cb = T // NSA_CMP_BLOCK
    pos = jnp.arange(T, dtype=jnp.int32)[:, None]
    cb_end = jnp.arange(n_cb, dtype=jnp.int32)[None, :] * NSA_CMP_BLOCK + (NSA_CMP_BLOCK - 1)
    bc = tab(pos - cb_end).reshape(N_HEADS, T // TQ, TQ, n_cb)
    bc = jnp.moveaxis(bc, 1, 0).reshape(T // TQ, N_HEADS * TQ, n_cb)
    return tb, bc


def _nsa_attn_kernel(q_ref, kcvc_ref, ksvs_ref, kwvw_ref, ng_ref, wkv_ref, tb_ref, bc_ref, o_ref,
                     cb_ref, expand_ref, negsel_ref, m_ref, acc_ref):
    qi = pl.program_id(1)
    T = ksvs_ref.shape[0]
    n_cb = T // NSA_CMP_BLOCK
    n_kt = T // TQ

    @pl.when(qi == 0)
    def _():
        x = kcvc_ref[...].reshape(n_cb, NSA_CMP_BLOCK, LANE) * wkv_ref[...][None]
        cb_ref[...] = jnp.sum(x, axis=1)
        r = lax.broadcasted_iota(jnp.int32, (n_cb, T), 0)
        key = lax.broadcasted_iota(jnp.int32, (n_cb, T), 1)
        expand_ref[...] = jnp.where(r == 2 * (key // NSA_SEL_BLOCK), 1.0, 0.0).astype(BF16)

    lane = lax.broadcasted_iota(jnp.int32, (TQ, LANE), 1)
    lo = lane < HEAD_DIM
    q = q_ref[...] * SCALE_64
    t0, t1 = q[:, :LANE], q[:, LANE:]
    qs = jnp.concatenate([jnp.where(lo, t0, 0.0), jnp.where(lo, pltpu.roll(t0, HEAD_DIM, 1), 0.0),
                          jnp.where(lo, t1, 0.0), jnp.where(lo, pltpu.roll(t1, HEAD_DIM, 1), 0.0)],
                         axis=0).astype(BF16)

    cb = cb_ref[...].astype(BF16)
    qpos = qi * TQ + lax.broadcasted_iota(jnp.int32, (TQ, n_cb), 0)
    cb_end = lax.broadcasted_iota(jnp.int32, (TQ, n_cb), 1) * NSA_CMP_BLOCK + (NSA_CMP_BLOCK - 1)
    valid = jnp.concatenate([cb_end <= qpos] * N_HEADS, axis=0)
    lc = jnp.where(valid, _dot_nt(qs, cb) + bc_ref[0], NEG)
    e = jnp.where(valid, jnp.exp(lc - jnp.max(lc, axis=-1, keepdims=True)), 0.0)
    pc = e / jnp.maximum(jnp.sum(e, axis=-1, keepdims=True), 1e-30)
    o_c = jnp.dot(pc.astype(BF16), cb, preferred_element_type=F32)

    psum = pc[0:TQ] + pc[TQ:2 * TQ] + pc[2 * TQ:3 * TQ] + pc[3 * TQ:4 * TQ]
    imp = (psum + pltpu.roll(psum, n_cb - 1, 1)).T
    row = lax.broadcasted_iota(jnp.int32, (n_cb, TQ), 0)
    qpos_t = qi * TQ + lax.broadcasted_iota(jnp.int32, (n_cb, TQ), 1)
    blk = row // 2
    score = jnp.where(qpos_t // NSA_SEL_BLOCK == blk, FORCE,
                      jnp.where(blk * NSA_SEL_BLOCK <= qpos_t, imp, NEG))
    score = jnp.where(row % 2 == 0, score, 2 * NEG)
    cnt = jnp.zeros((n_cb, TQ), F32)
    for i in range(0, n_cb, 2):
        si = score[i:i + 1, :]
        tie = jnp.where(row > i, 1.0, 0.0)
        cnt = cnt + jnp.where(si > score, 1.0, jnp.where(si == score, tie, 0.0))
    sel = jnp.where(cnt < NSA_TOPK, jnp.where(score > NEG / 2, 1.0, 0.0), 0.0).T
    keymask = jnp.dot(sel.astype(BF16), expand_ref[...], preferred_element_type=F32)
    for kt in range(n_kt):
        negsel_ref[kt] = (keymask[:, kt * TQ:(kt + 1) * TQ] - 1.0) * (-NEG)

    def step(kt, tab, kv_ref, use_sel):
        kv = kv_ref[pl.ds(pl.multiple_of(kt * TQ, TQ), TQ), :].astype(BF16)
        s = _dot_nt(qs, kv) + tb_ref[tab]
        if use_sel:
            ns = negsel_ref[kt]
            s = s + jnp.concatenate([ns] * N_HEADS, axis=0)
        _flash_step(s, kv, m_ref, acc_ref)

    _flash_reset(m_ref, acc_ref)
    lax.fori_loop(0, qi + 1,
                  lambda kt, c: (step(kt, jnp.maximum(kt - qi + 2, 0), ksvs_ref, True), c)[1], 0)
    o_s = _flash_out(acc_ref)
    _flash_reset(m_ref, acc_ref)
    lax.fori_loop(jnp.maximum(qi - 2, 0), qi + 1,
                  lambda kt, c: (step(kt, jnp.where(kt == qi - 2, 3, kt - qi + 2), kwvw_ref, False), c)[1], 0)
    o_w = _flash_out(acc_ref)

    g = jax.nn.sigmoid(ng_ref[...])
    mixed = []
    for h in range(N_HEADS):
        rows = slice(h * TQ, (h + 1) * TQ)
        mixed.append(g[:, h:h + 1] * o_c[rows] + g[:, N_HEADS + h:N_HEADS + h + 1] * o_s[rows]
                     + g[:, 2 * N_HEADS + h:2 * N_HEADS + h + 1] * o_w[rows])
    o_ref[...] = jnp.concatenate([jnp.where(lo, pltpu.roll(mixed[0], HEAD_DIM, 1), mixed[1]),
                                  jnp.where(lo, pltpu.roll(mixed[2], HEAD_DIM, 1), mixed[3])], axis=1)


def _nsa_attn(z, wkv, tb, bc, B, T):
    nq = T // TQ
    n_cb = T // NSA_CMP_BLOCK
    kv = lambda c: pl.BlockSpec((T, LANE), lambda b, i: (b, c))
    return pl.pallas_call(
        _nsa_attn_kernel,
        grid=(B, nq),
        in_specs=[pl.BlockSpec((TQ, MIX_WIDTH), lambda b, i: (b * nq + i, Z_NQ // MIX_WIDTH)),
                  kv(Z_KCVC // LANE), kv(Z_KSVS // LANE), kv(Z_KWVW // LANE),
                  pl.BlockSpec((TQ, LANE), lambda b, i: (b * nq + i, Z_NG // LANE)),
                  pl.BlockSpec(wkv.shape, lambda b, i: (0, 0)),
                  pl.BlockSpec(tb.shape, lambda b, i: (0, 0, 0)),
                  pl.BlockSpec((1, N_HEADS * TQ, n_cb), lambda b, i: (i, 0, 0))],
        out_specs=pl.BlockSpec((TQ, MIX_WIDTH), lambda b, i: (b * nq + i, 0)),
        out_shape=jax.ShapeDtypeStruct((B * T, MIX_WIDTH), F32),
        scratch_shapes=[pltpu.VMEM((n_cb, LANE), F32), pltpu.VMEM((n_cb, T), BF16),
                        pltpu.VMEM((nq, TQ, TQ), F32),
                        pltpu.VMEM((N_HEADS * TQ, LANE), F32),
                        pltpu.VMEM((N_HEADS * TQ, 2 * LANE), F32)],
        compiler_params=pltpu.CompilerParams(dimension_semantics=("parallel", "arbitrary"),
                                             vmem_limit_bytes=VMEM_LIMIT),
        name="nsa_attn",
    )(z, z, z, z, z, wkv, tb, bc)


PAGES_PER_STEP = 32
SUB = 8
HIGHEST = lax.Precision.HIGHEST


def _page_specs(layer, rows, row_block=0):
    return [pl.BlockSpec((None, None, rows, PAGE_ROWS),
                         lambda b, j, pt, k=k: (layer, pt[b, j * PAGES_PER_STEP + k], row_block, 0))
            for k in range(PAGES_PER_STEP)]


def _softmax_step_t(s, vt, m_ref, l_ref, acc_ref):
    m_old = m_ref[...]
    m_new = jnp.maximum(m_old, jnp.max(s, axis=-1, keepdims=True))
    alpha = jnp.exp(m_old - m_new)
    p = jnp.exp(s - m_new)
    l_ref[...] = alpha * l_ref[...] + jnp.sum(p, axis=-1, keepdims=True)
    acc_ref[...] = alpha * acc_ref[...] + _dot_nt(p.astype(BF16), vt)
    m_ref[...] = m_new


def _seq_spec(rows, width):
    return pl.BlockSpec((None, rows, width), lambda b, j, pt: (b, 0, 0))


def _const_spec(a):
    return pl.BlockSpec(a.shape, lambda b, j, pt: (0,) * a.ndim)


def _merge_new_key(s_new, v_new, m_ref, l_ref, acc_ref):
    m_old = m_ref[...]
    m_new = jnp.maximum(m_old, s_new)
    alpha = jnp.exp(m_old - m_new)
    p = jnp.exp(s_new - m_new)
    return (alpha * acc_ref[...] + p * v_new) / (alpha * l_ref[...] + p)


def _bf16_round(x):
    return x.astype(BF16).astype(F32)


def _paged_call(kernel_fn, name, page_table, n_steps, in_specs, out_specs, out_shape, scratch_shapes, args):
    nb = page_table.shape[0]
    return pl.pallas_call(
        kernel_fn,
        grid_spec=pltpu.PrefetchScalarGridSpec(num_scalar_prefetch=1, grid=(nb, n_steps), in_specs=in_specs,
                                               out_specs=out_specs, scratch_shapes=scratch_shapes),
        out_shape=out_shape,
        compiler_params=pltpu.CompilerParams(dimension_semantics=("parallel", "arbitrary"),
                                             vmem_limit_bytes=VMEM_LIMIT),
        name=name,
    )(page_table, *args)


def _mla_dec_kernel(pt_ref, q_ref, knew_ref, *refs):
    pages = refs[:PAGES_PER_STEP]
    o_ref, m_ref, l_ref, acc_ref = refs[PAGES_PER_STEP:]
    j = pl.program_id(1)
    scale = (MLA_NOPE + MLA_ROPE) ** -0.5

    @pl.when(j == 0)
    def _():
        _softmax_reset(m_ref, l_ref, acc_ref)

    qs = q_ref[...]
    kt = jnp.concatenate([p[...] for p in pages], axis=1).astype(BF16)
    s = jnp.dot(qs[:, :MLA_KV_LORA + MLA_ROPE], kt, preferred_element_type=F32) * scale
    _softmax_step_t(s, kt[:MLA_KV_LORA], m_ref, l_ref, acc_ref)

    @pl.when(j == pl.num_programs(1) - 1)
    def _():
        kn = _bf16_round(knew_ref[...])
        s_new = jnp.sum(qs.astype(F32) * kn, axis=-1, keepdims=True) * scale
        o_ref[...] = _merge_new_key(s_new, kn[:, :MLA_KV_LORA], m_ref, l_ref, acc_ref)


def _mla_dec(layer, page_table, cache_mla, q, knew):
    nb, n_pages = page_table.shape
    return _paged_call(
        _mla_dec_kernel, "mla_decode", page_table, n_pages // PAGES_PER_STEP,
        [_seq_spec(SUB, MLA_QW), _seq_spec(1, MLA_QW)] + _page_specs(layer, cache_mla.shape[2]),
        _seq_spec(SUB, MLA_KV_LORA), jax.ShapeDtypeStruct((nb, SUB, MLA_KV_LORA), F32),
        [pltpu.VMEM((SUB, 1), F32), pltpu.VMEM((SUB, 1), F32), pltpu.VMEM((SUB, MLA_KV_LORA), F32)],
        [q, knew] + [cache_mla] * PAGES_PER_STEP)


def _fox_dec_kernel(pt_ref, q_ref, knew_ref, vnew_ref, fnew_ref, dt_ref, *refs):
    P = PAGES_PER_STEP
    k_pages, v_pages, f_pages = refs[:P], refs[P:2 * P], refs[2 * P:3 * P]
    o_ref, fsum_ref, m_ref, l_ref, acc_ref = refs[3 * P:]
    j = pl.program_id(1)

    @pl.when(j == 0)
    def _():
        _softmax_reset(m_ref, l_ref, acc_ref)
        fsum_ref[...] = jnp.zeros(fsum_ref.shape, F32)

    qs = q_ref[...]
    carry = fsum_ref[...]
    pad = jnp.zeros((SUB - FOX_HEADS, PAGE_ROWS), F32)
    fks = []
    for p in f_pages:
        x = jnp.concatenate([p[...], pad], axis=0)
        fk = jnp.dot(x, dt_ref[...], preferred_element_type=F32, precision=HIGHEST) + carry
        carry = fk[:, PAGE_ROWS - 1:PAGE_ROWS]
        fks.append(fk)
    fsum_ref[...] = carry
    kt = jnp.concatenate([p[...] for p in k_pages], axis=1).astype(BF16)
    vt = jnp.concatenate([p[...] for p in v_pages], axis=1).astype(BF16)
    s = jnp.dot(qs, kt, preferred_element_type=F32) - jnp.concatenate(fks, axis=1)
    _softmax_step_t(s, vt, m_ref, l_ref, acc_ref)

    @pl.when(j == pl.num_programs(1) - 1)
    def _():
        m_ref[...] = m_ref[...] + (carry + fnew_ref[...][:, 0:1])
        s_new = jnp.sum(qs.astype(F32) * _bf16_round(knew_ref[...]), axis=-1, keepdims=True)
        o = _merge_new_key(s_new, _bf16_round(vnew_ref[...]), m_ref, l_ref, acc_ref)
        head = lax.broadcasted_iota(jnp.int32, (SUB, MIX_WIDTH), 1) // HEAD_DIM
        hrow = lax.broadcasted_iota(jnp.int32, (SUB, MIX_WIDTH), 0)
        o_ref[...] = jnp.sum(jnp.where(head == hrow, o, 0.0), axis=0, keepdims=True)


def _fox_dec(layer, page_table, cache_k, cache_v, cache_f, q, knew, vnew, fnew, dt):
    nb, n_pages = page_table.shape
    return _paged_call(
        _fox_dec_kernel, "fox_decode", page_table, n_pages // PAGES_PER_STEP,
        [_seq_spec(SUB, MIX_WIDTH), _seq_spec(1, MIX_WIDTH), _seq_spec(1, MIX_WIDTH), _seq_spec(SUB, LANE),
         _const_spec(dt)]
        + _page_specs(layer, MIX_WIDTH) + _page_specs(layer, MIX_WIDTH) + _page_specs(layer, FOX_HEADS),
        _seq_spec(1, MIX_WIDTH), jax.ShapeDtypeStruct((nb, 1, MIX_WIDTH), F32),
        [pltpu.VMEM((SUB, 1), F32), pltpu.VMEM((SUB, 1), F32), pltpu.VMEM((SUB, 1), F32),
         pltpu.VMEM((SUB, MIX_WIDTH), F32)],
        [q, knew, vnew, fnew, dt] + [cache_k] * PAGES_PER_STEP + [cache_v] * PAGES_PER_STEP
        + [cache_f] * PAGES_PER_STEP)


def _nsa_cmp_dec_kernel(pt_ref, q_ref, wt_ref, bias_ref, pair_ref, *refs):
    P = PAGES_PER_STEP
    pages = refs[:P]
    oc_ref, idx_ref, cb_ref = refs[P:]
    j = pl.program_id(1)
    per_page = PAGE_ROWS // NSA_CMP_BLOCK
    wk, wv = wt_ref[0].astype(BF16), wt_ref[1].astype(BF16)
    blocks = []
    for p in pages:
        x = p[...].astype(BF16)
        blocks.append(jnp.concatenate([_dot_nt(wk, x[:HEAD_DIM])[:per_page],
                                       _dot_nt(wv, x[HEAD_DIM:])[:per_page]], axis=1))
    rows = P * per_page
    cb_ref[pl.ds(pl.multiple_of(j * rows, rows), rows), :] = jnp.concatenate(blocks, axis=0)

    @pl.when(j == pl.num_programs(1) - 1)
    def _():
        qs = q_ref[...]
        cb = cb_ref[...].astype(BF16)
        lc = _dot_nt(qs, cb) + bias_ref[...]
        e = jnp.exp(lc - jnp.max(lc, axis=-1, keepdims=True))
        pc = e / jnp.maximum(jnp.sum(e, axis=-1, keepdims=True), 1e-30)
        oc_ref[...] = jnp.dot(pc.astype(BF16), cb, preferred_element_type=F32)
        psum = jnp.sum(pc[0:N_HEADS], axis=0, keepdims=True)
        imp = jnp.dot(jnp.broadcast_to(psum, (SUB, psum.shape[1])), pair_ref[...],
                      preferred_element_type=F32, precision=HIGHEST)
        n_sel = imp.shape[1]
        col = jnp.broadcast_to(imp.T[:, 0:1], (n_sel, n_sel))
        rowv = jnp.broadcast_to(imp[0:1, :], (n_sel, n_sel))
        i_idx = lax.broadcasted_iota(jnp.int32, (n_sel, n_sel), 0)
        j_idx = lax.broadcasted_iota(jnp.int32, (n_sel, n_sel), 1)
        ahead = jnp.where(col > rowv, 1.0, jnp.where(col == rowv, jnp.where(i_idx < j_idx, 1.0, 0.0), 0.0))
        rank = jnp.sum(ahead, axis=0, keepdims=True)
        r = lax.broadcasted_iota(jnp.int32, (2 * SUB, n_sel), 0).astype(F32)
        blk = lax.broadcasted_iota(jnp.int32, (2 * SUB, n_sel), 1).astype(F32)
        pick = jnp.sum(jnp.where(rank == r, blk, 0.0), axis=-1, keepdims=True)
        idx_ref[...] = jnp.broadcast_to(pick, (2 * SUB, LANE)).astype(jnp.int32)


def _nsa_cmp_dec(layer, page_table, cache_nsa, q, wt, bias, pair):
    nb, n_pages = page_table.shape
    n_cb = n_pages * PAGE_ROWS // NSA_CMP_BLOCK
    return _paged_call(
        _nsa_cmp_dec_kernel, "nsa_cmp_decode", page_table, n_pages // PAGES_PER_STEP,
        [_seq_spec(SUB, LANE), _const_spec(wt), _const_spec(bias), _const_spec(pair)]
        + _page_specs(layer, 2 * HEAD_DIM, 0),
        [_seq_spec(SUB, LANE), _seq_spec(2 * SUB, LANE)],
        [jax.ShapeDtypeStruct((nb, SUB, LANE), F32), jax.ShapeDtypeStruct((nb, 2 * SUB, LANE), jnp.int32)],
        [pltpu.VMEM((n_cb, LANE), F32)],
        [q, wt, bias, pair] + [cache_nsa] * PAGES_PER_STEP)


N_SEL_PAST = NSA_TOPK - 1


def _nsa_sel_dec_kernel(sel_ref, id_ref, q_ref, oc_ref, snew_ref, wnew_ref, gate_ref, win_ref, bsel_ref, bwin_ref,
                        b0_ref, *refs):
    blocks = refs[:N_SEL_PAST]
    o_ref = refs[N_SEL_PAST]
    b = pl.program_id(0)
    qs = q_ref[...][:, :HEAD_DIM]
    qf = qs.astype(F32)
    b0 = b0_ref[...]
    half_of_lane = lax.broadcasted_iota(jnp.int32, (SUB, PAGE_ROWS), 1) // NSA_SEL_BLOCK

    def attend(vts, s_list, new_ref):
        new = _bf16_round(new_ref[...])
        s_new = jnp.sum(qf * new[:, :HEAD_DIM], axis=-1, keepdims=True) + b0[:, 0:1]
        m = s_new
        for s in s_list:
            m = jnp.maximum(m, jnp.max(s, axis=-1, keepdims=True))
        p_new = jnp.exp(s_new - m)
        l, acc = p_new, p_new * new[:, HEAD_DIM:]
        for s, vt in zip(s_list, vts):
            p = jnp.exp(s - m)
            l = l + jnp.sum(p, axis=-1, keepdims=True)
            acc = acc + _dot_nt(p.astype(BF16), vt)
        return acc / l

    sel_s, sel_vt = [], []
    for r, ref in enumerate(blocks):
        x = ref[...].astype(BF16)
        blk = id_ref[b, r]
        s = jnp.dot(qs, x[:HEAD_DIM], preferred_element_type=F32) + bsel_ref[blk]
        sel_s.append(jnp.where(half_of_lane == blk % (PAGE_ROWS // NSA_SEL_BLOCK), s, NEG))
        sel_vt.append(x[HEAD_DIM:])
    o_s = attend(sel_vt, sel_s, snew_ref)
    win = win_ref[...].astype(BF16)
    o_w = attend([win[HEAD_DIM:]], [jnp.dot(qs, win[:HEAD_DIM], preferred_element_type=F32) + bwin_ref[...]],
                 wnew_ref)
    g = gate_ref[...]
    mixed = (g[0][:, :HEAD_DIM] * oc_ref[...][:, HEAD_DIM:] + g[1][:, :HEAD_DIM] * o_s
             + g[2][:, :HEAD_DIM] * o_w)
    o_ref[...] = jnp.concatenate([mixed[h:h + 1] for h in range(N_HEADS)], axis=1)


def _nsa_sel_dec(layer, sel_page, sel_id, cache_nsa, q, oc, snew, wnew, gates, win, bsel, bwin, b0):
    nb = q.shape[0]
    seq = lambda rows, width: pl.BlockSpec((None, rows, width), lambda b, sp, si: (b, 0, 0))
    const = lambda a: pl.BlockSpec(a.shape, lambda b, sp, si: (0,) * a.ndim)
    blk_specs = [pl.BlockSpec((None, None, 2 * HEAD_DIM, PAGE_ROWS), lambda b, sp, si, r=r: (layer, sp[b, r], 1, 0))
                 for r in range(N_SEL_PAST)]
    return pl.pallas_call(
        _nsa_sel_dec_kernel,
        grid_spec=pltpu.PrefetchScalarGridSpec(
            num_scalar_prefetch=2, grid=(nb,),
            in_specs=[seq(SUB, LANE), seq(SUB, LANE), seq(1, LANE), seq(1, LANE),
                      pl.BlockSpec((None, 3, SUB, LANE), lambda b, sp, si: (b, 0, 0, 0)),
                      pl.BlockSpec((None, None, 2 * HEAD_DIM, win.shape[3]), lambda b, sp, si: (layer, b, 0, 0)),
                      const(bsel), const(bwin), const(b0)] + blk_specs,
            out_specs=seq(1, MIX_WIDTH)),
        out_shape=jax.ShapeDtypeStruct((nb, 1, MIX_WIDTH), F32),
        compiler_params=pltpu.CompilerParams(dimension_semantics=("parallel",), vmem_limit_bytes=VMEM_LIMIT),
        name="nsa_sel_decode",
    )(sel_page, sel_id, q, oc, snew, wnew, gates, win, bsel, bwin, b0, *([cache_nsa] * N_SEL_PAST))


S5_CHUNK = 256
S5_WIDTH = S5_GROUPS * S5_STATE


def _s5_scan_kernel(u_ref, bbr_ref, bbi_ref, a_ref, cr_ref, ci_ref, d_ref, wg_ref, y_ref, st_ref,
                    xr_ref, xi_ref, sr_ref, si_ref):
    j = pl.program_id(0)

    @pl.when(j == 0)
    def _():
        sr_ref[...] = jnp.zeros(sr_ref.shape, F32)
        si_ref[...] = jnp.zeros(si_ref.shape, F32)

    u = u_ref[...]
    ub = u.astype(BF16)
    xr_ref[...] = jnp.dot(ub, bbr_ref[...], preferred_element_type=F32)
    xi_ref[...] = jnp.dot(ub, bbi_ref[...], preferred_element_type=F32)
    ar = jnp.broadcast_to(a_ref[0:1, :], (SUB, S5_WIDTH))
    ai = jnp.broadcast_to(a_ref[1:2, :], (SUB, S5_WIDTH))

    def step(t, carry):
        xr, xi = carry
        rows = pl.ds(pl.multiple_of(t * SUB, SUB), SUB)
        nr = ar * xr - ai * xi + xr_ref[rows, :]
        ni = ar * xi + ai * xr + xi_ref[rows, :]
        xr_ref[rows, :] = nr
        xi_ref[rows, :] = ni
        return nr, ni

    xr, xi = lax.fori_loop(0, S5_CHUNK, step, (sr_ref[...], si_ref[...]), unroll=4)
    sr_ref[...] = xr
    si_ref[...] = xi
    st_ref[0] = xr
    st_ref[1] = xi
    y = (jnp.dot(xr_ref[...].astype(BF16), cr_ref[...], preferred_element_type=F32)
         - jnp.dot(xi_ref[...].astype(BF16), ci_ref[...], preferred_element_type=F32)) + d_ref[...] * u
    y = jax.nn.gelu(y)
    y_ref[...] = y * jax.nn.sigmoid(jnp.dot(y.astype(BF16), wg_ref[...], preferred_element_type=F32))


def _s5_scan(u_tb, bbr, bbi, a, cr, ci, dvec, wg):
    n = u_tb.shape[0]
    rows = S5_CHUNK * SUB
    full = lambda x: pl.BlockSpec(x.shape, lambda j: (0,) * x.ndim)
    return pl.pallas_call(
        _s5_scan_kernel,
        grid=(n // rows,),
        in_specs=[pl.BlockSpec((rows, MIX_WIDTH), lambda j: (j, 0)), full(bbr), full(bbi), full(a), full(cr),
                  full(ci), full(dvec), full(wg)],
        out_specs=[pl.BlockSpec((rows, MIX_WIDTH), lambda j: (j, 0)),
                   pl.BlockSpec((2, SUB, S5_WIDTH), lambda j: (0, 0, 0))],
        out_shape=[jax.ShapeDtypeStruct((n, MIX_WIDTH), F32), jax.ShapeDtypeStruct((2, SUB, S5_WIDTH), F32)],
        scratch_shapes=[pltpu.VMEM((rows, S5_WIDTH), F32), pltpu.VMEM((rows, S5_WIDTH), F32),
                        pltpu.VMEM((SUB, S5_WIDTH), F32), pltpu.VMEM((SUB, S5_WIDTH), F32)],
        compiler_params=pltpu.CompilerParams(dimension_semantics=("arbitrary",), vmem_limit_bytes=VMEM_LIMIT),
        name="s5_scan",
    )(u_tb, bbr, bbi, a, cr, ci, dvec, wg)


def _s5_weights(w, l):
    lam_re, lam_im = w['s5_lambda_re'][l], w['s5_lambda_im'][l]
    dt = jnp.exp(w['s5_log_dt'][l])[:, None]
    mag = jnp.exp(lam_re * dt)
    a_re, a_im = mag * jnp.cos(lam_im * dt), mag * jnp.sin(lam_im * dt)
    den = lam_re * lam_re + lam_im * lam_im
    coef_re = ((a_re - 1.0) * lam_re + a_im * lam_im) / den
    coef_im = (a_im * lam_re - (a_re - 1.0) * lam_im) / den
    b_re, b_im = w['s5_b_re'][l], w['s5_b_im'][l]
    bb_re = coef_re[..., None] * b_re - coef_im[..., None] * b_im
    bb_im = coef_re[..., None] * b_im + coef_im[..., None] * b_re
    eye = jnp.eye(S5_GROUPS, dtype=F32)
    bd_in = lambda bb: jnp.einsum('gpi,gh->gihp', bb, eye).reshape(MIX_WIDTH, S5_WIDTH)
    bd_out = lambda c: jnp.einsum('gip,gh->gphi', c, eye).reshape(S5_WIDTH, MIX_WIDTH)
    a = jnp.stack([a_re.reshape(S5_WIDTH), a_im.reshape(S5_WIDTH)])
    return (bd_in(bb_re).astype(BF16), bd_in(bb_im).astype(BF16), a,
            bd_out(w['s5_c_re'][l]).astype(BF16), bd_out(w['s5_c_im'][l]).astype(BF16),
            w['s5_d'][l].reshape(1, MIX_WIDTH), w['s5_w_glu'][l].astype(BF16))


def _s5_prompt(u, w, l):
    B, T, _ = u.shape
    u_tb = jnp.pad(jnp.swapaxes(u, 0, 1), ((0, 0), (0, SUB - B), (0, 0))).reshape(T * SUB, MIX_WIDTH)
    y, st = _s5_scan(u_tb, *_s5_weights(w, l))
    y = jnp.swapaxes(y.reshape(T, SUB, MIX_WIDTH)[:, :B], 0, 1)
    return y, jnp.swapaxes(st[:, :B], 0, 1).reshape(B, 2, S5_GROUPS, S5_STATE)


def _rowmm_kernel(x_ref, w_ref, o_ref):
    o_ref[...] = jnp.dot(x_ref[...].astype(BF16), w_ref[...], preferred_element_type=F32)


def _rowmm(x, w):
    return pl.pallas_call(_rowmm_kernel, out_shape=jax.ShapeDtypeStruct((x.shape[0], w.shape[1]), F32),
                          name="row_matmul")(x, w)


def _rms_norm(x, g):
    xf = x.astype(F32)
    y = xf * lax.rsqrt(jnp.mean(xf * xf, axis=-1, keepdims=True) + EPS)
    return (y * g.astype(F32)).astype(x.dtype)


def _rope(x, pos):
    half = x.shape[-1] // 2
    freq = ROPE_THETA ** (-jnp.arange(half, dtype=F32) / half)
    ang = pos.astype(F32)[:, None] * freq[None, :]
    ang = ang.reshape(ang.shape[0], *([1] * (x.ndim - 3)), half)
    cos, sin = jnp.cos(ang), jnp.sin(ang)
    x1, x2 = x[..., :half], x[..., half:]
    return jnp.concatenate([x1 * cos - x2 * sin, x1 * sin + x2 * cos], axis=-1)


def _rel_bucket(dist):
    n = jnp.maximum(dist, 0)
    exact = REL_BUCKETS // 2
    large = exact + (jnp.log(jnp.maximum(n, 1).astype(F32) / exact)
                     / math.log(REL_MAX_DIST / exact) * (REL_BUCKETS - exact)).astype(jnp.int32)
    large = jnp.minimum(large, REL_BUCKETS - 1)
    return jnp.where(n < exact, n, large)


def _t5_bias(rel_bias, dist):
    b = rel_bias[_rel_bucket(dist)].astype(F32)
    return jnp.moveaxis(b, -1, -3)


def _masked_softmax(logits, mask):
    lg = jnp.where(mask, logits, NEG)
    m = jnp.max(lg, axis=-1, keepdims=True)
    e = jnp.where(mask, jnp.exp(lg - m), 0.0)
    return e / jnp.maximum(jnp.sum(e, axis=-1, keepdims=True), 1e-30)


def _attend(logits, values, spec):
    lg = logits[0] if len(logits) == 1 else jnp.concatenate(logits, axis=-1)
    p = jax.nn.softmax(lg, axis=-1)
    out, start = None, 0
    for l_, v_ in zip(logits, values):
        n = l_.shape[-1]
        o = jnp.einsum(spec, p[..., start:start + n], v_)
        out = o if out is None else out + o
        start += n
    return out


def _map_query_blocks(fn, arrays, qpos):
    T = qpos.shape[0]
    qb = Q_BLOCK if T % Q_BLOCK == 0 else T
    nb = T // qb

    def split(a):
        return jnp.moveaxis(a.reshape(a.shape[0], nb, qb, *a.shape[2:]), 1, 0)
    xs = tuple(split(a) for a in arrays) + (qpos.reshape(nb, qb),)
    out = lax.map(lambda args: fn(*args), xs)
    out = jnp.moveaxis(out, 0, 1)
    return out.reshape(out.shape[0], T, *out.shape[3:])


def _complex_affine_combine(e1, e2):
    a1r, a1i, b1r, b1i = e1
    a2r, a2i, b2r, b2i = e2
    return (a2r * a1r - a2i * a1i, a2r * a1i + a2i * a1r,
            a2r * b1r - a2i * b1i + b2r, a2r * b1i + a2i * b1r + b2i)


def _gather_past(l, cache_mla, cache_fox_k, cache_fox_v, cache_fox_logf, cache_nsa,
                 state_nsa_win, state_s5, page_table):
    nb, n_pages = page_table.shape

    def pages(cache):
        g = cache[l, page_table]
        return g.reshape(nb, n_pages * g.shape[2], *g.shape[3:])
    return {'mla': pages(cache_mla), 'fox_k': pages(cache_fox_k), 'fox_v': pages(cache_fox_v),
            'fox_logf': pages(cache_fox_logf), 'nsa': pages(cache_nsa),
            'win': state_nsa_win[l], 's5': state_s5[l]}


def _mla(q_lat, kv_lat, k_rope, pos, past, l, w):
    B, T, _ = q_lat.shape
    qn = _rms_norm(q_lat, w['mla_g_q'][l])
    q = jnp.einsum('btr,rhe->bthe', qn, w['mla_w_uq'][l])
    q_nope = q[..., :MLA_NOPE]
    q_rope = _rope(q[..., MLA_NOPE:], pos)
    c_new = _rms_norm(kv_lat, w['mla_g_kv'][l])
    kr_new = _rope(k_rope, pos)
    q_abs = jnp.einsum('bthn,lhn->bthl', q_nope, w['mla_w_uk'][l])
    segs = [(c_new, kr_new, pos)]
    if past is not None:
        n_past = past['mla'].shape[1]
        segs = [(past['mla'][..., :MLA_KV_LORA], past['mla'][..., MLA_KV_LORA:],
                 jnp.arange(n_past, dtype=jnp.int32))] + segs
    scale = (MLA_NOPE + MLA_ROPE) ** -0.5

    def block(qa, qr, qp):
        logits = []
        for c_, kr_, kp in segs:
            s = (jnp.einsum('bqhl,bkl->bhqk', qa, c_, preferred_element_type=F32)
                 + jnp.einsum('bqhr,bkr->bhqk', qr, kr_, preferred_element_type=F32)) * scale
            logits.append(jnp.where(kp[None, :] <= qp[:, None], s, NEG))
        return _attend(logits, [sg[0] for sg in segs], 'bhqk,bkl->bqhl')
    o_lat = _map_query_blocks(block, (q_abs, q_rope), pos)
    o = jnp.einsum('bthl,lhv->bthv', o_lat, w['mla_w_uv'][l]).reshape(B, T, MLA_HEADS * MLA_V)
    return o, jnp.concatenate([c_new, kr_new], axis=-1)


def _s5(u, past, l, w):
    B, T, _ = u.shape
    lam_re = w['s5_lambda_re'][l]
    lam_im = w['s5_lambda_im'][l]
    dt = jnp.exp(w['s5_log_dt'][l])[:, None]
    mag = jnp.exp(lam_re * dt)
    a_re, a_im = mag * jnp.cos(lam_im * dt), mag * jnp.sin(lam_im * dt)
    den = lam_re * lam_re + lam_im * lam_im
    coef_re = ((a_re - 1.0) * lam_re + a_im * lam_im) / den
    coef_im = (a_im * lam_re - (a_re - 1.0) * lam_im) / den
    b_re, b_im = w['s5_b_re'][l], w['s5_b_im'][l]
    bb_re = coef_re[..., None] * b_re - coef_im[..., None] * b_im
    bb_im = coef_re[..., None] * b_im + coef_im[..., None] * b_re
    ug = u.reshape(B, T, S5_GROUPS, S5_GROUP)
    bu_re = jnp.einsum('btgi,gpi->btgp', ug, bb_re)
    bu_im = jnp.einsum('btgi,gpi->btgp', ug, bb_im)
    if past is not None:
        x0_re, x0_im = past['s5'][:, 0], past['s5'][:, 1]
        bu_re = bu_re.at[:, 0].add(a_re * x0_re - a_im * x0_im)
        bu_im = bu_im.at[:, 0].add(a_re * x0_im + a_im * x0_re)
    A_re = jnp.broadcast_to(a_re, bu_re.shape)
    A_im = jnp.broadcast_to(a_im, bu_im.shape)
    _, _, x_re, x_im = lax.associative_scan(_complex_affine_combine, (A_re, A_im, bu_re, bu_im), axis=1)
    y = (jnp.einsum('btgp,gip->btgi', x_re, w['s5_c_re'][l])
         - jnp.einsum('btgp,gip->btgi', x_im, w['s5_c_im'][l]))
    y = y.reshape(B, T, MIX_WIDTH) + w['s5_d'][l] * u
    y = jax.nn.gelu(y)
    y = y * jax.nn.sigmoid(jnp.einsum('btw,wv->btv', y, w['s5_w_glu'][l]))
    state = jnp.stack([x_re[:, -1], x_im[:, -1]], axis=1)
    return y, state


def _fox(q, k, v, f_logit, pos, past, l, w):
    B, T, _ = q.shape
    q = q.reshape(B, T, FOX_HEADS, HEAD_DIM)
    k = k.reshape(B, T, FOX_HEADS, HEAD_DIM)
    v = v.reshape(B, T, FOX_HEADS, HEAD_DIM)
    logf = jax.nn.log_sigmoid(f_logit + w['fox_b_f'][l])
    if past is None:
        F_new = jnp.cumsum(logf, axis=1)
        segs = [(k, v, F_new, pos)]
    else:
        F_past = jnp.cumsum(past['fox_logf'], axis=1)
        F_new = F_past[:, -1:] + jnp.cumsum(logf, axis=1)
        segs = [(past['fox_k'], past['fox_v'], F_past, jnp.arange(F_past.shape[1], dtype=jnp.int32)),
                (k, v, F_new, pos)]
    scale = HEAD_DIM ** -0.5

    def block(qq, fq, qp):
        fq_t = jnp.swapaxes(fq, 1, 2)[..., None]
        logits = []
        for k_, _, fk, kp in segs:
            s = (jnp.einsum('bqhd,bkhd->bhqk', qq, k_, preferred_element_type=F32) * scale
                 + (fq_t - jnp.swapaxes(fk, 1, 2)[:, :, None, :]))
            logits.append(jnp.where(kp[None, :] <= qp[:, None], s, NEG))
        return _attend(logits, [sg[1] for sg in segs], 'bhqk,bkhd->bqhd')
    o = _map_query_blocks(block, (q, F_new), pos)
    return o.reshape(B, T, MIX_WIDTH), k, v, logf


def _nsa(q, kc, vc, ks, vs, kw, vw, g_logit, pos, past, l, w):
    B, T, _ = q.shape
    q = q.reshape(B, T, NSA_HEADS, HEAD_DIM)
    g = jax.nn.sigmoid(g_logit).reshape(B, T, 3, NSA_HEADS)
    rows = jnp.stack([kc, vc, ks, vs], axis=2)
    win_rows = jnp.stack([kw, vw], axis=2)
    if past is None:
        kc_all, vc_all, ks_all, vs_all = kc, vc, ks, vs
        win_ctx = win_rows
        keep = min(NSA_WINDOW, T)
    else:
        pr = past['nsa']
        kc_all = jnp.concatenate([pr[:, :, 0], kc], axis=1)
        vc_all = jnp.concatenate([pr[:, :, 1], vc], axis=1)
        ks_all = jnp.concatenate([pr[:, :, 2], ks], axis=1)
        vs_all = jnp.concatenate([pr[:, :, 3], vs], axis=1)
        win_ctx = jnp.concatenate([past['win'], win_rows], axis=1)
        keep = past['win'].shape[1]
    n_keys = kc_all.shape[1]
    win_base = n_keys - win_ctx.shape[1]
    n_cb = -(-n_keys // NSA_CMP_BLOCK)
    n_sb = -(-n_keys // NSA_SEL_BLOCK)
    ratio = NSA_SEL_BLOCK // NSA_CMP_BLOCK
    k_sel = min(NSA_TOPK, n_sb)

    def pad_rows(a, n):
        return jnp.pad(a, ((0, 0), (0, n - a.shape[1]), (0, 0)))
    kcb = jnp.einsum('bnid,i->bnd', pad_rows(kc_all, n_cb * NSA_CMP_BLOCK).reshape(B, n_cb, NSA_CMP_BLOCK, HEAD_DIM),
                     w['nsa_w_cmp_k'][l])
    vcb = jnp.einsum('bnid,i->bnd', pad_rows(vc_all, n_cb * NSA_CMP_BLOCK).reshape(B, n_cb, NSA_CMP_BLOCK, HEAD_DIM),
                     w['nsa_w_cmp_v'][l])
    cb_end = jnp.arange(n_cb, dtype=jnp.int32) * NSA_CMP_BLOCK + (NSA_CMP_BLOCK - 1)
    sb_start = jnp.arange(n_sb, dtype=jnp.int32) * NSA_SEL_BLOCK
    ks_pad = pad_rows(ks_all, n_sb * NSA_SEL_BLOCK)
    vs_pad = pad_rows(vs_all, n_sb * NSA_SEL_BLOCK)
    win_pad = jnp.pad(win_ctx, ((0, 0), (NSA_WINDOW, 0), (0, 0), (0, 0)))
    rel_bias = w['rel_bias']
    scale = HEAD_DIM ** -0.5

    def block(qq, gb, qp):
        nq = qp.shape[0]
        lc = (jnp.einsum('bqhd,bnd->bhqn', qq, kcb, preferred_element_type=F32) * scale
              + _t5_bias(rel_bias, qp[:, None] - cb_end[None, :]))
        pc = _masked_softmax(lc, cb_end[None, :] <= qp[:, None])
        o_c = jnp.einsum('bhqn,bnd->bqhd', pc, vcb)
        imp = jnp.pad(pc.sum(axis=1), ((0, 0), (0, 0), (0, n_sb * ratio - n_cb)))
        imp = imp.reshape(B, nq, n_sb, ratio).sum(-1)
        cur = (qp[:, None] // NSA_SEL_BLOCK) == jnp.arange(n_sb, dtype=jnp.int32)[None, :]
        score = jnp.where(cur, FORCE, jnp.where(sb_start[None, :] <= qp[:, None], imp, NEG))
        top_v, top_i = lax.top_k(score, k_sel)
        idx = (top_i[..., None] * NSA_SEL_BLOCK + jnp.arange(NSA_SEL_BLOCK, dtype=jnp.int32)).reshape(
            B, nq, k_sel * NSA_SEL_BLOCK)
        valid = jnp.repeat(top_v > NEG / 2, NSA_SEL_BLOCK, axis=-1) & (idx <= qp[None, :, None])
        ks_g = jax.vmap(lambda a, i: a[i])(ks_pad, idx)
        vs_g = jax.vmap(lambda a, i: a[i])(vs_pad, idx)
        ls = (jnp.einsum('bqhd,bqkd->bhqk', qq, ks_g, preferred_element_type=F32) * scale
              + _t5_bias(rel_bias, qp[None, :, None] - idx))
        ps = _masked_softmax(ls, valid[:, None])
        o_s = jnp.einsum('bhqk,bqkd->bqhd', ps, vs_g)
        start = qp[0] - win_base
        wk = lax.dynamic_slice_in_dim(win_pad, start, NSA_WINDOW + nq, axis=1)
        wp = qp[0] - NSA_WINDOW + jnp.arange(NSA_WINDOW + nq, dtype=jnp.int32)
        mask_w = ((wp[None, :] >= 0) & (wp[None, :] <= qp[:, None])
                  & (qp[:, None] - wp[None, :] <= NSA_WINDOW))
        lw = (jnp.einsum('bqhd,bkd->bhqk', qq, wk[:, :, 0], preferred_element_type=F32) * scale
              + _t5_bias(rel_bias, qp[:, None] - wp[None, :]))
        pw = jax.nn.softmax(jnp.where(mask_w, lw, NEG), axis=-1)
        o_w = jnp.einsum('bhqk,bkd->bqhd', pw, wk[:, :, 1])
        return (gb[:, :, 0, :, None] * o_c + gb[:, :, 1, :, None] * o_s
                + gb[:, :, 2, :, None] * o_w)
    o = _map_query_blocks(block, (q, g), pos)
    return o.reshape(B, T, MIX_WIDTH), rows, win_ctx[:, win_ctx.shape[1] - keep:]


def _zcol(z, B, T, start, width):
    return z[:, start:start + width].reshape(B, T, width)


def _mixers_prompt(z, B, T, l, w, wc):
    col = functools.partial(_zcol, z, B, T)
    qcat, kf, kb = _mla_prep(z, wc['rope_c'], wc['rope_s'], wc['mla_gq'][l], wc['mla_gkv'][l],
                             wc['mla_w1'][l], wc['mla_wuk'][l], T, 512)
    y_a = _mla_attn(qcat, kb, wc['mla_wuv'][l], B, T)
    mla_rows = kf[:, :MLA_KV_LORA + MLA_ROPE].reshape(B, T, MLA_KV_LORA + MLA_ROPE)

    logf = jax.nn.log_sigmoid(col(Z_NG + 3 * NSA_HEADS, FOX_HEADS) + w['fox_b_f'][l])
    fsum = jnp.cumsum(logf, axis=1)
    fq = jnp.pad(fsum.reshape(B * T, FOX_HEADS), ((0, 0), (0, LANE - FOX_HEADS)))
    fk = jnp.pad(jnp.swapaxes(fsum.reshape(B, T // TQ, TQ, FOX_HEADS), 2, 3),
                 ((0, 0), (0, 0), (0, 8 - FOX_HEADS), (0, 0)))
    y_c = _fox_attn(z, fq, fk, B, T)
    fox_k = col(Z_FK, MIX_WIDTH).reshape(B, T, FOX_HEADS, HEAD_DIM)
    fox_v = col(Z_FV, MIX_WIDTH).reshape(B, T, FOX_HEADS, HEAD_DIM)

    y_d = _nsa_attn(z, wc['nsa_wkv'][l], wc['nsa_tb'], wc['nsa_bc'], B, T)
    nsa_rows = col(Z_KCVC, 2 * LANE).reshape(B, T, 4, HEAD_DIM)
    keep = min(NSA_WINDOW, T)
    win_state = col(Z_KWVW, LANE)[:, T - keep:].reshape(B, keep, 2, HEAD_DIM)

    y_b, s5_state = _s5_prompt(col(Z_S5U, MIX_WIDTH), w, l)
    return y_a, y_b, y_c, y_d, (mla_rows, fox_k, fox_v, logf, nsa_rows, win_state, s5_state)


def _head_rows(a):
    return jnp.pad(a, ((0, 0), (0, SUB - a.shape[1])) + ((0, 0),) * (a.ndim - 2))


def _mixers_decode(z, l, w, wc, caches):
    (cache_mla, cache_fox_k, cache_fox_v, cache_fox_logf, cache_nsa, state_nsa_win, state_s5, page_table) = caches
    nb, n_pages = page_table.shape
    pool = cache_mla.shape[1]
    assert cache_mla.shape[2] == PAGE_ROWS and n_pages % PAGES_PER_STEP == 0
    dec = wc['dec']
    col = lambda start, width: z[:, start:start + width]

    qcat, kf, _ = _mla_prep(z, dec['rope_c'], dec['rope_s'], wc['mla_gq'][l], wc['mla_gkv'][l],
                            wc['mla_w1'][l], wc['mla_wuk'][l], nb, nb)
    def feat_major(c):
        c = jnp.moveaxis(c, 2, -1)
        return c.reshape(c.shape[0], c.shape[1], -1, c.shape[-1])
    o_lat = _mla_dec(l, page_table, feat_major(cache_mla), _head_rows(qcat.reshape(nb, N_HEADS, MLA_QW)),
                     kf.reshape(nb, 1, MLA_QW))
    y_a = _rowmm(o_lat[:, :N_HEADS].reshape(nb, N_HEADS * MLA_KV_LORA),
                 wc['mla_wuv'][l].reshape(N_HEADS * MLA_KV_LORA, MIX_WIDTH))
    mla_rows = kf[:, :MLA_KV_LORA + MLA_ROPE].reshape(nb, 1, MLA_KV_LORA + MLA_ROPE)

    head_of_lane = jnp.arange(MIX_WIDTH, dtype=jnp.int32) // HEAD_DIM
    fq = col(Z_FQ, MIX_WIDTH) * SCALE_64
    fq = jnp.where(head_of_lane[None, None, :] == jnp.arange(N_HEADS, dtype=jnp.int32)[None, :, None],
                   fq[:, None, :], 0.0)
    logf = jax.nn.log_sigmoid(col(Z_NG + 3 * NSA_HEADS, FOX_HEADS) + w['fox_b_f'][l])
    fnew = jnp.broadcast_to(_head_rows(logf[:, :, None]), (nb, SUB, LANE))
    y_c = _fox_dec(l, page_table, feat_major(cache_fox_k), feat_major(cache_fox_v), feat_major(cache_fox_logf),
                   _head_rows(fq).astype(BF16), col(Z_FK, MIX_WIDTH).reshape(nb, 1, MIX_WIDTH),
                   col(Z_FV, MIX_WIDTH).reshape(nb, 1, MIX_WIDTH), fnew, dec['fox_dt'])
    fox_k = col(Z_FK, MIX_WIDTH).reshape(nb, 1, FOX_HEADS, HEAD_DIM)
    fox_v = col(Z_FV, MIX_WIDTH).reshape(nb, 1, FOX_HEADS, HEAD_DIM)

    nq = (col(Z_NQ, MIX_WIDTH) * SCALE_64).reshape(nb, N_HEADS, HEAD_DIM)
    nq = _head_rows(jnp.pad(nq, ((0, 0), (0, 0), (0, LANE - HEAD_DIM)))).astype(BF16)
    nsa_t = feat_major(cache_nsa)
    key = jnp.arange(PAGE_ROWS, dtype=jnp.int32)
    in_block = key[None, :] // NSA_CMP_BLOCK == jnp.arange(SUB, dtype=jnp.int32)[:, None]
    wt = jnp.stack([jnp.where(in_block, w[name][l][key % NSA_CMP_BLOCK][None, :], 0.0)
                    for name in ('nsa_w_cmp_k', 'nsa_w_cmp_v')])
    o_c, idx = _nsa_cmp_dec(l, page_table, nsa_t, nq, wt, dec['nsa_bc'], dec['nsa_pair'])
    sel_id = idx[:, :N_SEL_PAST, 0]
    sel_page = jnp.take_along_axis(page_table, sel_id // (PAGE_ROWS // NSA_SEL_BLOCK), axis=1)
    gates = jax.nn.sigmoid(col(Z_NG, 3 * NSA_HEADS)).reshape(nb, 3, NSA_HEADS, 1)
    gates = jnp.broadcast_to(jnp.pad(gates, ((0, 0), (0, 0), (0, SUB - NSA_HEADS), (0, 0))), (nb, 3, SUB, LANE))
    wnew = col(Z_KWVW, LANE)
    win_past = feat_major(state_nsa_win)
    y_d = _nsa_sel_dec(l, sel_page, sel_id, nsa_t, nq, o_c, col(Z_KSVS, LANE).reshape(nb, 1, LANE), wnew.reshape(nb, 1, LANE), gates, win_past,
                       dec['nsa_bsel'], dec['nsa_bwin'], dec['nsa_b0'])
    nsa_rows = col(Z_KCVC, 2 * LANE).reshape(nb, 1, 4, HEAD_DIM)
    win_state = jnp.concatenate([state_nsa_win[l][:, 1:], wnew.reshape(nb, 1, 2, HEAD_DIM)], axis=1)

    y_b, s5_state = _s5(col(Z_S5U, MIX_WIDTH).reshape(nb, 1, MIX_WIDTH), {'s5': state_s5[l]}, l, w)
    return y_a, y_b, y_c, y_d, (mla_rows, fox_k, fox_v, logf.reshape(nb, 1, FOX_HEADS), nsa_rows, win_state,
                                s5_state)


def _decode_tables(rel_bias, past_len, nb):
    half = MLA_ROPE // 2
    freq = ROPE_THETA ** (-jnp.arange(half, dtype=F32) / half)
    ang = jnp.full((nb, 1), past_len, jnp.int32).astype(F32) * freq[None, :]
    cos, sin = jnp.cos(ang), jnp.sin(ang)
    lane_pad = ((0, 0), (0, LANE - MLA_ROPE))

    def tab(dist):
        oh = jax.nn.one_hot(_rel_bucket(dist), REL_BUCKETS, dtype=F32)
        return _head_rows(jnp.einsum('rcb,bh->hrc', oh, rel_bias.astype(F32), precision=HIGHEST)[None])[0]
    n_cb = past_len // NSA_CMP_BLOCK
    n_sb = past_len // NSA_SEL_BLOCK
    cb_end = jnp.arange(n_cb, dtype=jnp.int32) * NSA_CMP_BLOCK + (NSA_CMP_BLOCK - 1)
    key_pos = (jnp.arange(n_sb, dtype=jnp.int32)[:, None] * NSA_SEL_BLOCK
               + jnp.arange(NSA_SEL_BLOCK, dtype=jnp.int32)[None, :])
    win_pos = past_len - NSA_WINDOW + jnp.arange(NSA_WINDOW, dtype=jnp.int32)
    j = np.arange(PAGE_ROWS)[:, None]
    k = np.arange(PAGE_ROWS)[None, :]
    r = np.arange(n_cb)[:, None] // (NSA_SEL_BLOCK // NSA_CMP_BLOCK)
    c = np.arange(n_sb)[None, :]
    return {'rope_c': jnp.pad(jnp.concatenate([cos, cos], axis=-1), lane_pad),
            'rope_s': jnp.pad(jnp.concatenate([-sin, sin], axis=-1), lane_pad),
            'fox_dt': jnp.asarray((j <= k).astype(np.float32)),
            'nsa_pair': jnp.asarray((r == c).astype(np.float32)),
            'nsa_bc': tab((past_len - cb_end)[None, :])[:, 0, :],
            'nsa_bsel': jnp.tile(jnp.moveaxis(tab(past_len - key_pos), 0, 1), (1, 1, PAGE_ROWS // NSA_SEL_BLOCK)),
            'nsa_bwin': tab((past_len - win_pos)[None, :])[:, 0, :],
            'nsa_b0': jnp.broadcast_to(tab(jnp.zeros((1, 1), jnp.int32))[:, 0, :], (SUB, LANE))}


def _prep_weights(w_in, mla_g_q, mla_g_kv, mla_w_uq, mla_w_uk, mla_w_uv, nsa_w_cmp_k, nsa_w_cmp_v, rel_bias, T):
    def cols(a, b):
        return w_in[:, :, a:b]

    def zeros(n):
        return jnp.zeros((DEPTH, D_MODEL, n), w_in.dtype)
    o = [0] + [int(v) for v in np.cumsum(IN_SPLITS)]
    (o_ql, o_kv, o_kr, o_s5, o_fq, o_fk, o_fv, o_ff, o_nq, o_kc, o_vc, o_ks, o_vs, o_kw, o_vw, o_ng, o_gate,
     o_end) = o
    half = MLA_ROPE // 2
    w_in_r = jnp.concatenate([
        cols(o_gate, o_end), cols(o_fq, o_fk), cols(o_fk, o_fv), cols(o_fv, o_ff), cols(o_nq, o_kc),
        cols(o_s5, o_fq), cols(o_ql, o_kv), zeros(2 * LANE - MLA_Q_LORA), cols(o_kv, o_kr),
        cols(o_kr, o_s5), zeros(LANE - MLA_ROPE),
        cols(o_kr + half, o_s5), cols(o_kr, o_kr + half), zeros(LANE - MLA_ROPE),
        cols(o_kc, o_ks), cols(o_ks, o_kw), cols(o_kw, o_ng), cols(o_ng, o_gate), cols(o_ff, o_nq),
        zeros(LANE - 3 * NSA_HEADS - FOX_HEADS)], axis=-1)
    assert w_in_r.shape[-1] == Z_COLS

    def rope_cols(r):
        return jnp.pad(r, ((0, 0), (0, 0), (0, 0), (0, LANE - MLA_ROPE))).reshape(DEPTH, MLA_Q_LORA, N_HEADS * LANE)
    nope = mla_w_uq[..., :MLA_NOPE].reshape(DEPTH, MLA_Q_LORA, MIX_WIDTH)
    rope = mla_w_uq[..., MLA_NOPE:]
    rope_sw = jnp.concatenate([rope[..., half:], rope[..., :half]], axis=-1)
    w1 = jnp.concatenate([nope, rope_cols(rope), rope_cols(rope_sw)], axis=-1)
    w1 = jnp.pad(w1, ((0, 0), (0, 2 * LANE - MLA_Q_LORA), (0, 0)))
    eye = jnp.eye(N_HEADS, dtype=mla_w_uk.dtype)
    wuk = jnp.einsum('dlhn,hg->dhngl', mla_w_uk, eye).reshape(DEPTH, MIX_WIDTH, N_HEADS * MLA_KV_LORA)
    wuv = jnp.einsum('dlhv,hg->dhlgv', mla_w_uv, eye).reshape(DEPTH, N_HEADS, MLA_KV_LORA, MIX_WIDTH)

    freq = ROPE_THETA ** (-jnp.arange(half, dtype=F32) / half)
    ang = jnp.arange(T, dtype=jnp.int32).astype(F32)[:, None] * freq[None, :]
    cos, sin = jnp.cos(ang), jnp.sin(ang)
    lane_pad = ((0, 0), (0, LANE - MLA_ROPE))
    wkv = jnp.concatenate([jnp.broadcast_to(nsa_w_cmp_k[:, :, None], (DEPTH, NSA_CMP_BLOCK, HEAD_DIM)),
                           jnp.broadcast_to(nsa_w_cmp_v[:, :, None], (DEPTH, NSA_CMP_BLOCK, HEAD_DIM))], axis=-1)
    tb, bc = _nsa_tables(rel_bias, T)
    return {'w_in': w_in_r.astype(BF16),
            'mla_w1': w1.astype(BF16), 'mla_wuk': wuk.astype(BF16), 'mla_wuv': wuv.astype(BF16),
            'mla_gq': jnp.pad(mla_g_q, ((0, 0), (0, 2 * LANE - MLA_Q_LORA))).reshape(DEPTH, 1, 2 * LANE),
            'mla_gkv': mla_g_kv.reshape(DEPTH, 1, MLA_KV_LORA),
            'rope_c': jnp.pad(jnp.concatenate([cos, cos], axis=-1), lane_pad),
            'rope_s': jnp.pad(jnp.concatenate([-sin, sin], axis=-1), lane_pad),
            'nsa_wkv': wkv.astype(F32), 'nsa_tb': tb, 'nsa_bc': bc}


def _forward(x, mods, pos, w, wc, caches, tm):
    B, T, _ = x.shape
    n = B * T
    xf = x.reshape(n, D_MODEL)
    per_row = T == 1
    outs = [[] for _ in range(7)]
    for l in range(DEPTH):
        m6 = mods[l].reshape(B, 6, D_MODEL)
        if per_row:
            sh1, sc1, g1, sh2, sc2, g2 = (m6[:, i].reshape(1, n, D_MODEL) for i in range(6))
        else:
            sh1, sc1, g1, sh2, sc2, g2 = (m6[:, i].reshape(B, 1, D_MODEL) for i in range(6))
        gains = w['norm_gains'][l]
        z = _inproj(xf, gains[0], sc1, sh1, wc['w_in'][l], tm)
        if caches is None:
            y_a, y_b, y_c, y_d, states = _mixers_prompt(z, B, T, l, w, wc)
        else:
            y_a, y_b, y_c, y_d, states = _mixers_decode(z, l, w, wc, caches)
        mla_rows, fox_k, fox_v, fox_logf, nsa_rows, win_state, s5_state = states
        ys = tuple(y.reshape(n, MIX_WIDTH) for y in (y_a, y_b, y_c, y_d))
        xf = _merge(ys, z, xf, wc['w_branch'][l], wc['w_out'][l], gains[1], g1, tm)
        xf = _ffn(xf, gains[2], sc2, sh2, wc['w_ffn_gate'][l], wc['w_ffn_up'][l], wc['w_ffn_down'][l],
                  gains[3], g2, tm)
        for o, s in zip(outs, (mla_rows, fox_k, fox_v, fox_logf, nsa_rows, win_state, s5_state)):
            o.append(s)
    return xf.reshape(B, T, D_MODEL), tuple(jnp.stack(o) for o in outs)


def kernel(x_prompt, x_sample, c_prompt, c_sample, cache_mla, cache_fox_k, cache_fox_v, cache_fox_logf, cache_nsa, state_nsa_win, state_s5, page_table, w_ada, b_ada, norm_gains, w_in, mla_g_q, mla_g_kv, mla_w_uq, mla_w_uk, mla_w_uv, s5_lambda_re, s5_lambda_im, s5_log_dt, s5_b_re, s5_b_im, s5_c_re, s5_c_im, s5_d, s5_w_glu, fox_b_f, nsa_w_cmp_k, nsa_w_cmp_v, rel_bias, w_branch, w_out, w_ffn_gate, w_ffn_up, w_ffn_down):
    w = {'norm_gains': norm_gains,
         'mla_g_q': mla_g_q, 'mla_g_kv': mla_g_kv, 'mla_w_uq': mla_w_uq, 'mla_w_uk': mla_w_uk,
         'mla_w_uv': mla_w_uv, 's5_lambda_re': s5_lambda_re, 's5_lambda_im': s5_lambda_im,
         's5_log_dt': s5_log_dt, 's5_b_re': s5_b_re, 's5_b_im': s5_b_im, 's5_c_re': s5_c_re,
         's5_c_im': s5_c_im, 's5_d': s5_d, 's5_w_glu': s5_w_glu, 'fox_b_f': fox_b_f,
         'nsa_w_cmp_k': nsa_w_cmp_k, 'nsa_w_cmp_v': nsa_w_cmp_v, 'rel_bias': rel_bias}
    wc = _prep_weights(w_in, mla_g_q, mla_g_kv, mla_w_uq, mla_w_uk, mla_w_uv, nsa_w_cmp_k, nsa_w_cmp_v, rel_bias,
                       x_prompt.shape[1])
    wc.update({'w_branch': w_branch.astype(BF16), 'w_out': w_out.astype(BF16),
               'w_ffn_gate': w_ffn_gate.astype(BF16), 'w_ffn_up': w_ffn_up.astype(BF16),
               'w_ffn_down': w_ffn_down.astype(BF16)})
    wc['dec'] = _decode_tables(rel_bias, page_table.shape[1] * cache_mla.shape[2], x_sample.shape[0])
    nb_p = c_prompt.shape[0]
    mods = _ada(jnp.concatenate([c_prompt, c_sample], axis=0), w_ada, b_ada)
    past_len = page_table.shape[1] * cache_mla.shape[2]
    pos_p = jnp.arange(x_prompt.shape[1], dtype=jnp.int32)
    pos_s = past_len + jnp.arange(x_sample.shape[1], dtype=jnp.int32)
    caches = (cache_mla, cache_fox_k, cache_fox_v, cache_fox_logf, cache_nsa,
              state_nsa_win, state_s5, page_table)
    y_prompt, st_p = _forward(x_prompt, mods[:, :nb_p], pos_p, w, wc, None, 512)
    y_sample, st_s = _forward(x_sample, mods[:, nb_p:], pos_s, w, wc, caches, x_sample.shape[0])
    mla_p, fox_k_p, fox_v_p, fox_logf_p, nsa_p, win_p, s5_p = st_p
    mla_s, fox_k_s, fox_v_s, fox_logf_s, nsa_s, win_s, s5_s = st_s
    return (y_prompt, y_sample, mla_p, mla_s, fox_k_p, fox_k_s, fox_v_p, fox_v_s,
            fox_logf_p, fox_logf_s, nsa_p, nsa_s, win_p, win_s, s5_p, s5_s)
```
